```python
import math
import jax, jax.numpy as jnp
from jax import lax
import numpy as np

D_MODEL = 2048
BATCH = 8
SEQ = 2048
DEPTH = 1

HG_HEADS = 8
HG_DK = 128
HG_DV = 128
HG_WIDTH = HG_HEADS * HG_DK
HG_CHUNK = 64

AT_HEADS = 16
AT_KV_HEADS = 4
AT_HEAD_DIM = 64
AT_GROUP = AT_HEADS // AT_KV_HEADS
AT_WIDTH = AT_HEADS * AT_HEAD_DIM
KV_WIDTH = AT_KV_HEADS * AT_HEAD_DIM
WINDOW = 128
BLOCK = 128

N_BUCKETS = 32
MAX_EXACT = N_BUCKETS // 2
MAX_DISTANCE = 128

D_FF = 4 * D_MODEL
N_BRANCH = 2
EPS = 1e-6
NEG_INF = -1e30

IN_WIDTH = 4 * HG_WIDTH + AT_WIDTH + 2 * KV_WIDTH + N_BRANCH * D_MODEL
IN_OFFSETS = (
    HG_WIDTH,
    2 * HG_WIDTH,
    3 * HG_WIDTH,
    4 * HG_WIDTH,
    4 * HG_WIDTH + AT_WIDTH,
    4 * HG_WIDTH + AT_WIDTH + KV_WIDTH,
    4 * HG_WIDTH + AT_WIDTH + 2 * KV_WIDTH,
    4 * HG_WIDTH + AT_WIDTH + 2 * KV_WIDTH + D_MODEL,
)

kernel_name = "hgrn2_swa_sink_gated_hybrid_block"


def rms_norm(x, g):
    xf = x.astype(jnp.float32)
    y = xf * lax.rsqrt(jnp.mean(xf * xf, axis=-1, keepdims=True) + EPS)
    return (y * g.astype(jnp.float32)).astype(x.dtype)


def modulate(h, shift, scale):
    return h * (1.0 + scale[:, None, :]) + shift[:, None, :]


def t5_causal_bucket(n):
    nf = jnp.maximum(n, 1).astype(jnp.float32)
    large = MAX_EXACT + (jnp.log(nf / MAX_EXACT) / math.log(MAX_DISTANCE / MAX_EXACT)
                         * (N_BUCKETS - MAX_EXACT)).astype(jnp.int32)
    large = jnp.minimum(large, N_BUCKETS - 1)
    return jnp.where(n < MAX_EXACT, n, large)


def band_geometry(n_blocks):
    i = jnp.arange(BLOCK, dtype=jnp.int32)[:, None]
    j = jnp.arange(2 * BLOCK, dtype=jnp.int32)[None, :]
    dist = i - j + BLOCK
    blk = jnp.arange(n_blocks, dtype=jnp.int32)[:, None, None]
    key_pos = blk * BLOCK - BLOCK + j
    mask = (dist >= 0) & (dist < WINDOW) & (key_pos >= 0)
    bucket = t5_causal_bucket(jnp.maximum(dist, 0))
    return mask, bucket


def hgrn2_chunkwise(q, log_f, k, v):
    B, H, L, DK = q.shape
    DV = v.shape[-1]
    C = HG_CHUNK
    N = L // C
    q = q.reshape(B, H, N, C, DK)
    k = k.reshape(B, H, N, C, DK)
    v = v.reshape(B, H, N, C, DV)
    b = jnp.cumsum(log_f.reshape(B, H, N, C, DK), axis=3)
    ref = b[:, :, :, C // 2 - 1:C // 2]
    b_last = b[:, :, :, C - 1:]
    scores = jnp.einsum('bhncd,bhnsd->bhncs', q * jnp.exp(b - ref), k * jnp.exp(ref - b))
    causal = jnp.tril(jnp.ones((C, C), dtype=bool))
    scores = jnp.where(causal, scores, 0.0)
    o = jnp.einsum('bhncs,bhnsv->bhncv', scores, v)
    upd = jnp.einsum('bhncd,bhncv->bhndv', k * jnp.exp(b_last - b), v)
    decay = jnp.exp(b_last[:, :, :, 0])

    def step(S, xs):
        dec, u = xs
        return dec[..., None] * S + u, S

    _, S_prev = lax.scan(step, jnp.zeros((B, H, DK, DV), q.dtype),
                         (jnp.moveaxis(decay, 2, 0), jnp.moveaxis(upd, 2, 0)))
    S_prev = jnp.moveaxis(S_prev, 0, 2)
    o = o + jnp.einsum('bhncd,bhndv->bhncv', q * jnp.exp(b), S_prev)
    return o.reshape(B, H, L, DV)


def sink_swa(q, k, v, sinks, rel_bias_table):
    B, L = q.shape[0], q.shape[1]
    nb = L // BLOCK
    qb = q.reshape(B, nb, BLOCK, AT_KV_HEADS, AT_GROUP, AT_HEAD_DIM)

    def band(t):
        tb = t.reshape(B, nb, BLOCK, AT_KV_HEADS, AT_HEAD_DIM)
        prev = jnp.pad(tb, ((0, 0), (1, 0), (0, 0), (0, 0), (0, 0)))[:, :-1]
        return jnp.concatenate([prev, tb], axis=2)

    kk, vv = band(k), band(v)
    mask, bucket = band_geometry(nb)
    bias = jnp.transpose(rel_bias_table[bucket], (2, 0, 1)).astype(jnp.float32)
    bias = bias.reshape(AT_KV_HEADS, AT_GROUP, BLOCK, 2 * BLOCK)
    scale = AT_HEAD_DIM ** -0.5
    logits = jnp.einsum('bnqkgd,bnskd->bnkgqs', qb, kk).astype(jnp.float32) * scale + bias
    logits = jnp.where(mask[None, :, None, None], logits, NEG_INF)
    sink = jnp.broadcast_to(sinks.astype(jnp.float32).reshape(AT_KV_HEADS, AT_GROUP, 1, 1),
                            logits.shape[:-1] + (1,))
    p = jax.nn.softmax(jnp.concatenate([logits, sink], axis=-1), axis=-1)[..., :2 * BLOCK]
    o = jnp.einsum('bnkgqs,bnskd->bnqkgd', p.astype(vv.dtype), vv)
    return o.reshape(B, L, AT_WIDTH)


def setup_inputs(seed: int = 0) -> dict:
    key = jax.random.key(seed)
    ks = jax.random.split(key, 20)

    def nrm(k, shape, s):
        return jax.random.normal(k, shape, jnp.float32) * s

    return {
        "x": nrm(ks[0], (BATCH, SEQ, D_MODEL), 1.0),
        "c": nrm(ks[1], (BATCH, D_MODEL), 1.0),
        "w_ada": nrm(ks[2], (DEPTH, D_MODEL, 6 * D_MODEL), 0.5 * D_MODEL ** -0.5),
        "b_ada": nrm(ks[3], (DEPTH, 6 * D_MODEL), 0.02),
        "norm1_g": 1.0 + nrm(ks[4], (DEPTH, D_MODEL), 0.02),
        "norm2_g": 1.0 + nrm(ks[5], (DEPTH, D_MODEL), 0.02),
        "w_in": nrm(ks[6], (DEPTH, D_MODEL, IN_WIDTH), D_MODEL ** -0.5),
        "hg_lb_logits": nrm(ks[7], (DEPTH + 1, HG_WIDTH), 1.0),
        "hg_out_norm_g": 1.0 + nrm(ks[8], (DEPTH, HG_DV), 0.02),
        "q_norm_g": 1.0 + nrm(ks[9], (DEPTH, AT_HEAD_DIM), 0.02),
        "k_norm_g": 1.0 + nrm(ks[10], (DEPTH, AT_HEAD_DIM), 0.02),
        "attn_sinks": nrm(ks[11], (DEPTH, AT_HEADS), 1.0),
        "rel_bias_table": nrm(ks[12], (N_BUCKETS, AT_HEADS), 0.5),
        "w_branch_hg": nrm(ks[13], (DEPTH, HG_WIDTH, D_MODEL), HG_WIDTH ** -0.5),
        "w_branch_attn": nrm(ks[14], (DEPTH, AT_WIDTH, D_MODEL), AT_WIDTH ** -0.5),
        "w_out": nrm(ks[15], (DEPTH, D_MODEL, D_MODEL), D_MODEL ** -0.5),
        "w_ff1": nrm(ks[16], (DEPTH, D_MODEL, D_FF), D_MODEL ** -0.5),
        "w_ff2": nrm(ks[17], (DEPTH, D_FF, D_MODEL), D_FF ** -0.5),
    }


def reference(x, c, w_ada, b_ada, norm1_g, norm2_g, w_in, hg_lb_logits, hg_out_norm_g,
              q_norm_g, k_norm_g, attn_sinks, rel_bias_table, w_branch_hg, w_branch_attn,
              w_out, w_ff1, w_ff2):
    B, L, _ = x.shape
    lb_all = jnp.cumsum(jax.nn.softmax(hg_lb_logits.astype(jnp.float32), axis=0), axis=0)
    c_act = jax.nn.silu(c)
    for l in range(DEPTH):
        ada = c_act @ w_ada[l] + b_ada[l]
        shift1, scale1, gate1, shift2, scale2, gate2 = jnp.split(ada, 6, axis=-1)

        h = modulate(rms_norm(x, norm1_g[l]), shift1, scale1)
        proj = h @ w_in[l]
        hq, hf, hi, hg, aq, ak, av, gate_hg, gate_at = jnp.split(proj, IN_OFFSETS, axis=-1)

        lb = lb_all[l]
        f = lb + (1.0 - lb) * jax.nn.sigmoid(hf.astype(jnp.float32))
        log_f = jnp.log(f)

        def to_heads(t):
            return t.reshape(B, L, HG_HEADS, HG_DK).transpose(0, 2, 1, 3)

        o_hg = hgrn2_chunkwise(to_heads(jax.nn.silu(hq.astype(jnp.float32))), to_heads(log_f),
                               to_heads(1.0 - f), to_heads(hi.astype(jnp.float32)))
        o_hg = o_hg.transpose(0, 2, 1, 3).astype(x.dtype)
        o_hg = rms_norm(o_hg, hg_out_norm_g[l]) * jax.nn.silu(hg.reshape(B, L, HG_HEADS, HG_DV))
        o_hg = o_hg.reshape(B, L, HG_WIDTH)

        q = rms_norm(aq.reshape(B, L, AT_HEADS, AT_HEAD_DIM), q_norm_g[l])
        k = rms_norm(ak.reshape(B, L, AT_KV_HEADS, AT_HEAD_DIM), k_norm_g[l])
        v = av.reshape(B, L, AT_KV_HEADS, AT_HEAD_DIM)
        o_at = sink_swa(q, k, v, attn_sinks[l], rel_bias_table)

        merged = (jax.nn.sigmoid(gate_hg) * (o_hg @ w_branch_hg[l])
                  + jax.nn.sigmoid(gate_at) * (o_at @ w_branch_attn[l]))
        x = x + gate1[:, None, :] * (merged @ w_out[l])

        h2 = modulate(rms_norm(x, norm2_g[l]), shift2, scale2)
        ff = jnp.square(jax.nn.relu(h2 @ w_ff1[l])) @ w_ff2[l]
        x = x + gate2[:, None, :] * ff
    return x
```

```python
import functools
import math

import jax
import jax.numpy as jnp
from jax import lax
from jax.experimental import pallas as pl
from jax.experimental.pallas import tpu as pltpu

F32 = jnp.float32
BF16 = jnp.bfloat16

D_MODEL = 2048
HG_HEADS = 8
HG_DK = 128
HG_DV = 128
HG_WIDTH = HG_HEADS * HG_DK
HG_CHUNK = 64
AT_HEADS = 16
AT_KV_HEADS = 4
AT_HEAD_DIM = 64
AT_GROUP = AT_HEADS // AT_KV_HEADS
AT_WIDTH = AT_HEADS * AT_HEAD_DIM
KV_WIDTH = AT_KV_HEADS * AT_HEAD_DIM
WINDOW = 128
BLOCK = 128
N_BUCKETS = 32
MAX_EXACT = N_BUCKETS // 2
MAX_DISTANCE = 128
D_FF = 4 * D_MODEL
EPS = 1e-6
NEG_INF = -1e30

GATE_OFF = 0
HG_OFF = 2 * D_MODEL
AT_OFF = HG_OFF + 4 * HG_WIDTH
IN_WIDTH = AT_OFF + AT_WIDTH + 2 * KV_WIDTH
REF_GATE_OFF = 4 * HG_WIDTH + AT_WIDTH + 2 * KV_WIDTH

V7X_VMEM_LIMIT_BYTES = 56 * 1024 * 1024


def _params(n_axes):
    return pltpu.CompilerParams(
        dimension_semantics=("arbitrary",) * n_axes,
        vmem_limit_bytes=V7X_VMEM_LIMIT_BYTES,
    )


def _rms_modulate(x, g, scale, shift):
    r = lax.rsqrt(jnp.mean(x * x, axis=-1, keepdims=True) + EPS)
    return (x * r * g) * (1.0 + scale) + shift


def _ada_kernel(c_ref, w_ref, b_ref, o_ref):
    c = c_ref[...]
    c_act = c * jax.nn.sigmoid(c)
    o_ref[...] = jnp.dot(c_act, w_ref[...], preferred_element_type=F32,
                         precision=lax.Precision.HIGHEST) + b_ref[...]


def _ada(c, w_ada, b_ada):
    bsz, d = c.shape
    n = w_ada.shape[-1]
    tn = 1024
    return pl.pallas_call(
        _ada_kernel,
        grid=(n // tn,),
        in_specs=[
            pl.BlockSpec((bsz, d), lambda j: (0, 0)),
            pl.BlockSpec((None, d, tn), lambda j: (0, 0, j)),
            pl.BlockSpec((1, tn), lambda j: (0, j)),
        ],
        out_specs=pl.BlockSpec((bsz, tn), lambda j: (0, j)),
        out_shape=jax.ShapeDtypeStruct((bsz, n), F32),
        compiler_params=_params(1),
        name="ada",
    )(c, w_ada, b_ada)


def _inproj_kernel(x_ref, g_ref, shift_ref, scale_ref, w_ref, o_ref):
    h = _rms_modulate(x_ref[...], g_ref[...], scale_ref[...], shift_ref[...])
    o_ref[...] = jnp.dot(h.astype(BF16), w_ref[...], preferred_element_type=F32).astype(o_ref.dtype)


def _inproj(x, norm_g, ada4, w_in_bf16):
    bsz, seq, d = x.shape
    n = w_in_bf16.shape[-1]
    tm, n_col_tiles = 512, 4
    tn = n // n_col_tiles
    tiles_per_seq = seq // tm
    row = lambda j, i: (i // tiles_per_seq, i % tiles_per_seq)
    return pl.pallas_call(
        _inproj_kernel,
        grid=(n_col_tiles, bsz * tiles_per_seq),
        in_specs=[
            pl.BlockSpec((None, tm, d), lambda j, i: (*row(j, i), 0)),
            pl.BlockSpec((1, d), lambda j, i: (0, 0)),
            pl.BlockSpec((None, None, 1, d), lambda j, i: (i // tiles_per_seq, 0, 0, 0)),
            pl.BlockSpec((None, None, 1, d), lambda j, i: (i // tiles_per_seq, 1, 0, 0)),
            pl.BlockSpec((d, tn), lambda j, i: (0, j)),
        ],
        out_specs=pl.BlockSpec((None, tm, tn), lambda j, i: (*row(j, i), j)),
        out_shape=jax.ShapeDtypeStruct((bsz, seq, n), BF16),
        compiler_params=_params(2),
        name="inproj",
    )(x, norm_g, ada4, ada4, w_in_bf16)


HG_UNROLL = 4


def _hgrn_kernel(q_ref, f_ref, i_ref, g_ref, lbl_ref, gn_ref, o_ref):
    seq = q_ref.shape[0]
    c = HG_CHUNK
    lbl = lbl_ref[...]
    e = jnp.exp(lbl - jnp.max(lbl, axis=0, keepdims=True))
    lb = e[0:1] / jnp.sum(e, axis=0, keepdims=True)
    gn = gn_ref[...]
    row = lax.broadcasted_iota(jnp.int32, (c, c), 0)
    col = lax.broadcasted_iota(jnp.int32, (c, c), 1)
    causal = col <= row
    tri = causal.astype(BF16)

    def chunk(start, s_t):
        sl = pl.ds(start, c)
        hf = f_ref[sl, :].astype(F32)
        hq = q_ref[sl, :].astype(F32)
        v = i_ref[sl, :]
        hg = g_ref[sl, :].astype(F32)
        f = lb + (1.0 - lb) * jax.nn.sigmoid(hf)
        log_f = jnp.log(f)
        k = 1.0 - f
        hi = log_f.astype(BF16)
        lo = (log_f - hi.astype(F32)).astype(BF16)
        b = (jnp.dot(tri, hi, preferred_element_type=F32)
             + jnp.dot(tri, lo, preferred_element_type=F32))
        b_ref = b[c // 2 - 1:c // 2]
        b_last = b[c - 1:c]
        qs = hq * jax.nn.sigmoid(hq)
        a = (qs * jnp.exp(b - b_ref)).astype(BF16)
        bm = (k * jnp.exp(b_ref - b)).astype(BF16)
        scores = lax.dot_general(a, bm, (((1,), (1,)), ((), ())), preferred_element_type=F32)
        scores = jnp.where(causal, scores, 0.0)
        o = jnp.dot(scores.astype(BF16), v, preferred_element_type=F32)
        qe = (qs * jnp.exp(b)).astype(BF16)
        o = o + lax.dot_general(qe, s_t.astype(BF16), (((1,), (1,)), ((), ())),
                                preferred_element_type=F32)
        kk = (k * jnp.exp(b_last - b)).astype(BF16)
        upd_t = lax.dot_general(v, kk, (((0,), (0,)), ((), ())), preferred_element_type=F32)
        s_t = s_t * jnp.exp(b_last) + upd_t
        y = o * lax.rsqrt(jnp.mean(o * o, axis=-1, keepdims=True) + EPS) * gn
        o_ref[sl, :] = (y * (hg * jax.nn.sigmoid(hg))).astype(o_ref.dtype)
        return s_t

    def body(it, s_t):
        base = pl.multiple_of(it * (c * HG_UNROLL), c * HG_UNROLL)
        for u in range(HG_UNROLL):
            s_t = chunk(base + u * c, s_t)
        return s_t

    lax.fori_loop(0, seq // (c * HG_UNROLL), body, jnp.zeros((HG_DV, HG_DK), F32))


def _hgrn(proj, lb_logits, out_norm_g):
    bsz, seq, _ = proj.shape
    base = HG_OFF // HG_DK
    col = lambda k: pl.BlockSpec((None, seq, HG_DK), lambda b, h: (b, 0, base + k * HG_HEADS + h))
    return pl.pallas_call(
        _hgrn_kernel,
        grid=(bsz, HG_HEADS),
        in_specs=[
            col(0), col(1), col(2), col(3),
            pl.BlockSpec((lb_logits.shape[0], HG_DK), lambda b, h: (0, h)),
            pl.BlockSpec((1, HG_DV), lambda b, h: (0, 0)),
        ],
        out_specs=pl.BlockSpec((None, seq, HG_DV), lambda b, h: (b, 0, h)),
        out_shape=jax.ShapeDtypeStruct((bsz, seq, HG_WIDTH), BF16),
        compiler_params=_params(2),
        name="hgrn",
    )(proj, proj, proj, proj, lb_logits, out_norm_g)


def _t5_causal_bucket(n):
    nf = jnp.maximum(n, 1).astype(F32)
    large = MAX_EXACT + (jnp.log(nf / MAX_EXACT) / math.log(MAX_DISTANCE / MAX_EXACT)
                         * (N_BUCKETS - MAX_EXACT)).astype(jnp.int32)
    large = jnp.minimum(large, N_BUCKETS - 1)
    return jnp.where(n < MAX_EXACT, n, large)


def _bias_kernel(table_ref, bucket_ref, o_ref):
    bucket = bucket_ref[...]
    i = lax.broadcasted_iota(jnp.int32, bucket.shape, 0)
    j = lax.broadcasted_iota(jnp.int32, bucket.shape, 1)
    dist = i - j + BLOCK
    band = (dist >= 0) & (dist < WINDOW)
    for h in range(AT_HEADS):
        acc = jnp.zeros(bucket.shape, F32)
        for r in range(N_BUCKETS):
            acc = jnp.where(bucket == r, table_ref[r, h], acc)
        o_ref[h] = jnp.where(band, acc, NEG_INF)


def _attn_bias(rel_bias_table):
    i = jnp.arange(BLOCK, dtype=jnp.int32)[:, None]
    j = jnp.arange(2 * BLOCK, dtype=jnp.int32)[None, :]
    bucket = _t5_causal_bucket(jnp.maximum(i - j + BLOCK, 0))
    return pl.pallas_call(
        _bias_kernel,
        in_specs=[
            pl.BlockSpec(memory_space=pltpu.SMEM),
            pl.BlockSpec((BLOCK, 2 * BLOCK), lambda: (0, 0)),
        ],
        out_specs=pl.BlockSpec((AT_HEADS, BLOCK, 2 * BLOCK), lambda: (0, 0, 0)),
        out_shape=jax.ShapeDtypeStruct((AT_HEADS, BLOCK, 2 * BLOCK), F32),
        name="attn_bias",
    )(rel_bias_table, bucket)


ATT_TQ = 512


def _head_mean_square(t, pool):
    return jnp.dot((t * t).astype(BF16), pool, preferred_element_type=F32)


def _attn_kernel(q_ref, k_ref, v_ref, kh_ref, vh_ref, bias_ref, sink_ref, qg_ref, kg_ref, o_ref):
    step = pl.program_id(1)
    d = AT_HEAD_DIM
    grp = AT_GROUP * d
    r = lax.broadcasted_iota(jnp.int32, (grp, grp), 0) // d
    cidx = lax.broadcasted_iota(jnp.int32, (grp, grp), 1) // d
    pool = jnp.where(r == cidx, 1.0 / d, 0.0).astype(BF16)

    k_all = jnp.concatenate([kh_ref[...], k_ref[...]], axis=0).astype(F32)
    v_all = jnp.concatenate([vh_ref[...], v_ref[...]], axis=0)
    k_n = (k_all * lax.rsqrt(_head_mean_square(k_all, pool) + EPS) * kg_ref[...]).astype(BF16)

    key_col = lax.broadcasted_iota(jnp.int32, (1, 2 * BLOCK), 1)
    pre_seq = jnp.where((step == 0) & (key_col < BLOCK), NEG_INF, 0.0)

    scale = d ** -0.5
    for kv in range(AT_KV_HEADS):
        q_g = q_ref[:, kv * grp:(kv + 1) * grp].astype(F32)
        q_n = (q_g * lax.rsqrt(_head_mean_square(q_g, pool) + EPS) * (qg_ref[...] * scale)).astype(BF16)
        bias = bias_ref[kv * AT_GROUP:(kv + 1) * AT_GROUP].reshape(AT_GROUP * BLOCK, 2 * BLOCK)
        sink = jnp.concatenate(
            [jnp.full((BLOCK, 1), sink_ref[kv * AT_GROUP + g], F32) for g in range(AT_GROUP)], axis=0)
        for nb in range(ATT_TQ // BLOCK):
            rows = slice(nb * BLOCK, (nb + 1) * BLOCK)
            keys = slice(nb * BLOCK, (nb + 2) * BLOCK)
            k_b = k_n[keys, kv * d:(kv + 1) * d]
            v_b = v_all[keys, kv * d:(kv + 1) * d]
            q4 = jnp.concatenate([q_n[rows, g * d:(g + 1) * d] for g in range(AT_GROUP)], axis=0)
            logits = lax.dot_general(q4, k_b, (((1,), (1,)), ((), ())), preferred_element_type=F32) + bias
            if nb == 0:
                logits = logits + pre_seq
            m = jnp.maximum(jnp.max(logits, axis=-1, keepdims=True), sink)
            e = jnp.exp(logits - m)
            den = jnp.sum(e, axis=-1, keepdims=True) + jnp.exp(sink - m)
            o4 = jnp.dot(e.astype(BF16), v_b, preferred_element_type=F32) / den
            o_ref[rows, kv * grp:(kv + 1) * grp] = jnp.concatenate(
                [o4[g * BLOCK:(g + 1) * BLOCK] for g in range(AT_GROUP)], axis=1).astype(o_ref.dtype)


def _attn(proj, bias, sinks, q_norm_g, k_norm_g):
    bsz, seq, _ = proj.shape
    tq = ATT_TQ
    blocks_per_step = tq // BLOCK
    q_col = AT_OFF // AT_WIDTH
    k_col = (AT_OFF + AT_WIDTH) // KV_WIDTH
    v_col = k_col + 1
    halo = lambda colblk: pl.BlockSpec(
        (None, BLOCK, KV_WIDTH), lambda b, i: (b, jnp.maximum(i * blocks_per_step - 1, 0), colblk))
    qg = jnp.tile(q_norm_g, (1, AT_GROUP))
    kg = jnp.tile(k_norm_g, (1, AT_KV_HEADS))
    return pl.pallas_call(
        _attn_kernel,
        grid=(bsz, seq // tq),
        in_specs=[
            pl.BlockSpec((None, tq, AT_WIDTH), lambda b, i: (b, i, q_col)),
            pl.BlockSpec((None, tq, KV_WIDTH), lambda b, i: (b, i, k_col)),
            pl.BlockSpec((None, tq, KV_WIDTH), lambda b, i: (b, i, v_col)),
            halo(k_col), halo(v_col),
            pl.BlockSpec((AT_HEADS, BLOCK, 2 * BLOCK), lambda b, i: (0, 0, 0)),
            pl.BlockSpec(memory_space=pltpu.SMEM),
            pl.BlockSpec((1, AT_GROUP * AT_HEAD_DIM), lambda b, i: (0, 0)),
            pl.BlockSpec((1, KV_WIDTH), lambda b, i: (0, 0)),
        ],
        out_specs=pl.BlockSpec((None, tq, AT_WIDTH), lambda b, i: (b, i, 0)),
        out_shape=jax.ShapeDtypeStruct((bsz, seq, AT_WIDTH), BF16),
        compiler_params=_params(2),
        name="attn",
    )(proj, proj, proj, proj, proj, bias, sinks, qg, kg)


def _merge_kernel(x_ref, ohg_ref, oat_ref, ghg_ref, gat_ref, wbh_ref, wba_ref, wo_ref,
                  gate1_ref, n2g_ref, shift2_ref, scale2_ref, x1_ref, h2_ref):
    m_hg = jnp.dot(ohg_ref[...], wbh_ref[...], preferred_element_type=F32)
    m_at = jnp.dot(oat_ref[...], wba_ref[...], preferred_element_type=F32)
    merged = (jax.nn.sigmoid(ghg_ref[...].astype(F32)) * m_hg
              + jax.nn.sigmoid(gat_ref[...].astype(F32)) * m_at)
    y = jnp.dot(merged.astype(BF16), wo_ref[...], preferred_element_type=F32)
    x1 = x_ref[...] + gate1_ref[...] * y
    x1_ref[...] = x1
    h2_ref[...] = _rms_modulate(x1, n2g_ref[...], scale2_ref[...], shift2_ref[...]).astype(h2_ref.dtype)


def _merge(x, o_hg, o_at, proj, w_bh, w_ba, w_o, ada4, norm2_g):
    bsz, seq, d = x.shape
    tm = 512
    resident = lambda shape: pl.BlockSpec(shape, lambda b, i: (0, 0), pipeline_mode=pl.Buffered(1))
    ada_row = lambda k: pl.BlockSpec((None, None, 1, d), lambda b, i: (b, k, 0, 0))
    rows = lambda w, colblk: pl.BlockSpec((None, tm, w), lambda b, i: (b, i, colblk))
    return pl.pallas_call(
        _merge_kernel,
        grid=(bsz, seq // tm),
        in_specs=[
            rows(d, 0), rows(HG_WIDTH, 0), rows(AT_WIDTH, 0),
            rows(d, GATE_OFF // d), rows(d, GATE_OFF // d + 1),
            resident(w_bh.shape), resident(w_ba.shape), resident(w_o.shape),
            ada_row(2), pl.BlockSpec((1, d), lambda b, i: (0, 0)), ada_row(3), ada_row(4),
        ],
        out_specs=[rows(d, 0), rows(d, 0)],
        out_shape=[jax.ShapeDtypeStruct((bsz, seq, d), F32), jax.ShapeDtypeStruct((bsz, seq, d), BF16)],
        compiler_params=_params(2),
        name="merge",
    )(x, o_hg, o_at, proj, proj, w_bh, w_ba, w_o, ada4, norm2_g, ada4, ada4)


def _ffn_kernel(h2_ref, w1_ref, w2_ref, x1_ref, gate2_ref, o_ref, acc_ref):
    j = pl.program_id(2)

    @pl.when(j == 0)
    def _():
        acc_ref[...] = jnp.zeros_like(acc_ref)

    hid = jnp.maximum(jnp.dot(h2_ref[...], w1_ref[...], preferred_element_type=F32), 0.0)
    acc_ref[...] += jnp.dot((hid * hid).astype(BF16), w2_ref[...], preferred_element_type=F32)

    @pl.when(j == pl.num_programs(2) - 1)
    def _():
        o_ref[...] = x1_ref[...] + gate2_ref[...] * acc_ref[...]


def _ffn(h2, x1, w1, w2, ada4):
    bsz, seq, d = x1.shape
    dff = w1.shape[-1]
    tm, tf = 512, 1024
    rows = lambda: pl.BlockSpec((None, tm, d), lambda b, i, j: (b, i, 0))
    return pl.pallas_call(
        _ffn_kernel,
        grid=(bsz, seq // tm, dff // tf),
        in_specs=[
            rows(),
            pl.BlockSpec((d, tf), lambda b, i, j: (0, j)),
            pl.BlockSpec((tf, d), lambda b, i, j: (j, 0)),
            rows(),
            pl.BlockSpec((None, None, 1, d), lambda b, i, j: (b, 5, 0, 0)),
        ],
        out_specs=rows(),
        out_shape=jax.ShapeDtypeStruct((bsz, seq, d), F32),
        scratch_shapes=[pltpu.VMEM((tm, d), F32)],
        compiler_params=_params(3),
        name="ffn",
    )(h2, w1, w2, x1, ada4)


def kernel(x, c, w_ada, b_ada, norm1_g, norm2_g, w_in, hg_lb_logits, hg_out_norm_g, q_norm_g, k_norm_g,
           attn_sinks, rel_bias_table, w_branch_hg, w_branch_attn, w_out, w_ff1, w_ff2):
    assert w_ada.shape[0] == 1, "single-layer block"
    bsz = x.shape[0]
    w_in_p = jnp.concatenate([w_in[0, :, REF_GATE_OFF:], w_in[0, :, :REF_GATE_OFF]], axis=1).astype(BF16)
    w_bh = w_branch_hg[0].astype(BF16)
    w_ba = w_branch_attn[0].astype(BF16)
    w_o = w_out[0].astype(BF16)
    w1 = w_ff1[0].astype(BF16)
    w2 = w_ff2[0].astype(BF16)

    ada = _ada(c, w_ada, b_ada)
    ada4 = ada.reshape(bsz, 6, 1, D_MODEL)
    proj = _inproj(x, norm1_g, ada4, w_in_p)
    o_hg = _hgrn(proj, hg_lb_logits, hg_out_norm_g)
    bias = _attn_bias(rel_bias_table)
    o_at = _attn(proj, bias, attn_sinks[0], q_norm_g, k_norm_g)
    x1, h2 = _merge(x, o_hg, o_at, proj, w_bh, w_ba, w_o, ada4, norm2_g)
    return _ffn(h2, x1, w1, w2, ada4)
```

```python
import functools
import math

import jax
import jax.numpy as jnp
from jax import lax
from jax.experimental import pallas as pl
from jax.experimental.pallas import tpu as pltpu

F32 = jnp.float32
BF16 = jnp.bfloat16

D_MODEL = 2048
HG_HEADS = 8
HG_DK = 128
HG_DV = 128
HG_WIDTH = HG_HEADS * HG_DK
HG_CHUNK = 64
AT_HEADS = 16
AT_KV_HEADS = 4
AT_HEAD_DIM = 64
AT_GROUP = AT_HEADS // AT_KV_HEADS
AT_WIDTH = AT_HEADS * AT_HEAD_DIM
KV_WIDTH = AT_KV_HEADS * AT_HEAD_DIM
WINDOW = 128
BLOCK = 128
N_BUCKETS = 32
MAX_EXACT = N_BUCKETS // 2
MAX_DISTANCE = 128
D_FF = 4 * D_MODEL
EPS = 1e-6
NEG_INF = -1e30

GATE_OFF = 0
HG_OFF = 2 * D_MODEL
AT_OFF = HG_OFF + 4 * HG_WIDTH
IN_WIDTH = AT_OFF + AT_WIDTH + 2 * KV_WIDTH
REF_GATE_OFF = 4 * HG_WIDTH + AT_WIDTH + 2 * KV_WIDTH

V7X_VMEM_LIMIT_BYTES = 56 * 1024 * 1024


def _params(n_axes):
    return pltpu.CompilerParams(
        dimension_semantics=("arbitrary",) * n_axes,
        vmem_limit_bytes=V7X_VMEM_LIMIT_BYTES,
    )


def _rms_modulate(x, g, scale, shift):
    r = lax.rsqrt(jnp.mean(x * x, axis=-1, keepdims=True) + EPS)
    return (x * r * g) * (1.0 + scale) + shift


def _ada_kernel(c_ref, w_ref, b_ref, o_ref):
    c = c_ref[...]
    c_act = c * jax.nn.sigmoid(c)
    o_ref[...] = jnp.dot(c_act, w_ref[...], preferred_element_type=F32,
                         precision=lax.Precision.HIGHEST) + b_ref[...]


def _ada(c, w_ada, b_ada):
    bsz, d = c.shape
    n = w_ada.shape[-1]
    tn = 1024
    return pl.pallas_call(
        _ada_kernel,
        grid=(n // tn,),
        in_specs=[
            pl.BlockSpec((bsz, d), lambda j: (0, 0)),
            pl.BlockSpec((None, d, tn), lambda j: (0, 0, j)),
            pl.BlockSpec((1, tn), lambda j: (0, j)),
        ],
        out_specs=pl.BlockSpec((bsz, tn), lambda j: (0, j)),
        out_shape=jax.ShapeDtypeStruct((bsz, n), F32),
        compiler_params=_params(1),
        name="ada",
    )(c, w_ada, b_ada)


def _inproj_kernel(x_ref, g_ref, shift_ref, scale_ref, w_ref, o_ref):
    h = _rms_modulate(x_ref[...], g_ref[...], scale_ref[...], shift_ref[...])
    o_ref[...] = jnp.dot(h.astype(BF16), w_ref[...], preferred_element_type=F32).astype(o_ref.dtype)


def _inproj(x, norm_g, ada4, w_in_bf16):
    bsz, seq, d = x.shape
    n = w_in_bf16.shape[-1]
    tm, n_col_tiles = 512, 4
    tn = n // n_col_tiles
    tiles_per_seq = seq // tm
    row = lambda j, i: (i // tiles_per_seq, i % tiles_per_seq)
    return pl.pallas_call(
        _inproj_kernel,
        grid=(n_col_tiles, bsz * tiles_per_seq),
        in_specs=[
            pl.BlockSpec((None, tm, d), lambda j, i: (*row(j, i), 0)),
            pl.BlockSpec((1, d), lambda j, i: (0, 0)),
            pl.BlockSpec((None, None, 1, d), lambda j, i: (i // tiles_per_seq, 0, 0, 0)),
            pl.BlockSpec((None, None, 1, d), lambda j, i: (i // tiles_per_seq, 1, 0, 0)),
            pl.BlockSpec((d, tn), lambda j, i: (0, j)),
        ],
        out_specs=pl.BlockSpec((None, tm, tn), lambda j, i: (*row(j, i), j)),
        out_shape=jax.ShapeDtypeStruct((bsz, seq, n), BF16),
        compiler_params=_params(2),
        name="inproj",
    )(x, norm_g, ada4, ada4, w_in_bf16)


HG_GROUP = 4
HG_ROWS = HG_GROUP * HG_CHUNK
HG_UNROLL = 4


def _hgrn_kernel(q_ref, f_ref, i_ref, g_ref, lbl_ref, gn_ref, o_ref):
    seq = q_ref.shape[0]
    c, n = HG_CHUNK, HG_ROWS
    lbl = lbl_ref[...]
    e = jnp.exp(lbl - jnp.max(lbl, axis=0, keepdims=True))
    lb = e[0:1] / jnp.sum(e, axis=0, keepdims=True)
    gn = gn_ref[...]

    t = lax.broadcasted_iota(jnp.int32, (n, n), 0)
    s = lax.broadcasted_iota(jnp.int32, (n, n), 1)
    same = (t // c) == (s // c)
    tri = same & (s <= t)
    r8 = lax.broadcasted_iota(jnp.int32, (8, n), 0)
    s8 = lax.broadcasted_iota(jnp.int32, (8, n), 1)
    in_chunk = (s8 // c) == (r8 % HG_GROUP)
    stat_rows = in_chunk & ((r8 >= HG_GROUP) | ((s8 % c) < c // 2))
    cum_ops = jnp.concatenate([tri.astype(F32), stat_rows.astype(F32)], axis=0).astype(BF16)
    per_chunk = lambda rows: jnp.concatenate(
        [jnp.broadcast_to(rows[cc:cc + 1], (c, HG_DK)) for cc in range(HG_GROUP)], axis=0)
    col_chunk = lax.broadcasted_iota(jnp.int32, (HG_DV, n), 1) // c
    col_masks = [(col_chunk == cc).astype(BF16) for cc in range(HG_GROUP)]
    zeros_rows = lambda rows: jnp.zeros((rows, HG_DK), BF16)

    def body(it, s_t):
        sl = pl.ds(pl.multiple_of(it * n, n), n)
        hf = f_ref[sl, :].astype(F32)
        hq = q_ref[sl, :].astype(F32)
        v = i_ref[sl, :]
        hg = g_ref[sl, :].astype(F32)
        f = lb + (1.0 - lb) * jax.nn.sigmoid(hf)
        log_f = jnp.log(f)
        k = 1.0 - f
        qs = hq * jax.nn.sigmoid(hq)
        hi = log_f.astype(BF16)
        lo = (log_f - hi.astype(F32)).astype(BF16)
        r = jnp.dot(cum_ops, jnp.concatenate([hi, lo], axis=1), preferred_element_type=F32)
        r = r[:, :HG_DK] + r[:, HG_DK:]
        b, b_last = r[0:n], r[n + HG_GROUP:n + 2 * HG_GROUP]
        b_rel = b - per_chunk(r[n:n + HG_GROUP])
        a = (qs * jnp.exp(b_rel)).astype(BF16)
        bm = (k * jnp.exp(-b_rel)).astype(BF16)
        kk = (k * jnp.exp(per_chunk(b_last) - b)).astype(BF16)
        qe = (qs * jnp.exp(b)).astype(BF16)
        scores = lax.dot_general(a, bm, (((1,), (1,)), ((), ())), preferred_element_type=F32)
        o = jnp.dot(jnp.where(tri, scores, 0.0).astype(BF16), v, preferred_element_type=F32)
        v_t = v.astype(F32).T.astype(BF16)
        v_t_blocks = jnp.concatenate([v_t * col_masks[cc] for cc in range(HG_GROUP)], axis=0)
        upd_t = jnp.dot(v_t_blocks, kk, preferred_element_type=F32)
        dec = jnp.exp(b_last)
        states = []
        for cc in range(HG_GROUP):
            states.append(s_t.astype(BF16))
            s_t = s_t * dec[cc:cc + 1] + upd_t[cc * HG_DV:(cc + 1) * HG_DV]
        qe_blocks = jnp.concatenate(
            [jnp.concatenate([p for p in (zeros_rows(cc * c), qe[cc * c:(cc + 1) * c],
                                          zeros_rows(n - (cc + 1) * c)) if p.shape[0]], axis=0)
             for cc in range(HG_GROUP)], axis=1)
        o = o + lax.dot_general(qe_blocks, jnp.concatenate(states, axis=1), (((1,), (1,)), ((), ())),
                                preferred_element_type=F32)
        y = o * lax.rsqrt(jnp.mean(o * o, axis=-1, keepdims=True) + EPS) * gn
        o_ref[sl, :] = (y * (hg * jax.nn.sigmoid(hg))).astype(o_ref.dtype)
        return s_t

    lax.fori_loop(0, seq // n, body, jnp.zeros((HG_DV, HG_DK), F32), unroll=HG_UNROLL)


def _hgrn(proj, lb_logits, out_norm_g):
    bsz, seq, _ = proj.shape
    base = HG_OFF // HG_DK
    col = lambda k: pl.BlockSpec((None, seq, HG_DK), lambda b, h: (b, 0, base + k * HG_HEADS + h))
    return pl.pallas_call(
        _hgrn_kernel,
        grid=(bsz, HG_HEADS),
        in_specs=[
            col(0), col(1), col(2), col(3),
            pl.BlockSpec((lb_logits.shape[0], HG_DK), lambda b, h: (0, h)),
            pl.BlockSpec((1, HG_DV), lambda b, h: (0, 0)),
        ],
        out_specs=pl.BlockSpec((None, seq, HG_DV), lambda b, h: (b, 0, h)),
        out_shape=jax.ShapeDtypeStruct((bsz, seq, HG_WIDTH), BF16),
        compiler_params=_params(2),
        name="hgrn",
    )(proj, proj, proj, proj, lb_logits, out_norm_g)


def _t5_causal_bucket(n):
    nf = jnp.maximum(n, 1).astype(F32)
    large = MAX_EXACT + (jnp.log(nf / MAX_EXACT) / math.log(MAX_DISTANCE / MAX_EXACT)
                         * (N_BUCKETS - MAX_EXACT)).astype(jnp.int32)
    large = jnp.minimum(large, N_BUCKETS - 1)
    return jnp.where(n < MAX_EXACT, n, large)


def _bias_kernel(table_ref, bucket_t_ref, o_ref):
    bucket_t = bucket_t_ref[...]
    j = lax.broadcasted_iota(jnp.int32, bucket_t.shape, 0)
    i = lax.broadcasted_iota(jnp.int32, bucket_t.shape, 1)
    dist = i - j + BLOCK
    band = (dist >= 0) & (dist < WINDOW)
    for h in range(AT_HEADS):
        acc = jnp.zeros(bucket_t.shape, F32)
        for r in range(N_BUCKETS):
            acc = jnp.where(bucket_t == r, table_ref[r, h], acc)
        g = h % AT_GROUP
        o_ref[h // AT_GROUP, :, g * BLOCK:(g + 1) * BLOCK] = jnp.where(band, acc, NEG_INF)


def _attn_bias(rel_bias_table):
    i = jnp.arange(BLOCK, dtype=jnp.int32)[None, :]
    j = jnp.arange(2 * BLOCK, dtype=jnp.int32)[:, None]
    bucket_t = _t5_causal_bucket(jnp.maximum(i - j + BLOCK, 0))
    shape = (AT_KV_HEADS, 2 * BLOCK, AT_GROUP * BLOCK)
    return pl.pallas_call(
        _bias_kernel,
        in_specs=[
            pl.BlockSpec(memory_space=pltpu.SMEM),
            pl.BlockSpec((2 * BLOCK, BLOCK), lambda: (0, 0)),
        ],
        out_specs=pl.BlockSpec(shape, lambda: (0, 0, 0)),
        out_shape=jax.ShapeDtypeStruct(shape, F32),
        name="attn_bias",
    )(rel_bias_table, bucket_t)


ATT_TQ = 512


def _head_mean_square(t, pool):
    return jnp.dot((t * t).astype(BF16), pool, preferred_element_type=F32)


def _attn_kernel(q_ref, k_ref, v_ref, kh_ref, vh_ref, bias_ref, sink_ref, qg_ref, kg_ref, o_ref):
    step = pl.program_id(1)
    d = AT_HEAD_DIM
    grp = AT_GROUP * d
    sub = lax.broadcasted_iota(jnp.int32, (grp, grp), 0)
    lane = lax.broadcasted_iota(jnp.int32, (grp, grp), 1)
    pool = jnp.where(sub // d == lane // d, 1.0 / d, 0.0).astype(BF16)

    k_all = jnp.concatenate([kh_ref[...], k_ref[...]], axis=0).astype(F32)
    k_n = (k_all * lax.rsqrt(_head_mean_square(k_all, pool) + EPS) * kg_ref[...]).astype(BF16)
    v_t = jnp.concatenate([vh_ref[...], v_ref[...]], axis=0).astype(F32).T.astype(BF16)

    lane_head = lax.broadcasted_iota(jnp.int32, (BLOCK, grp), 1) // d
    head_masks = [(lane_head == g).astype(BF16) for g in range(AT_GROUP)]
    pre_seq = jnp.where(step == 0, NEG_INF, 0.0)

    scale = d ** -0.5
    for kv in range(AT_KV_HEADS):
        spread = ((sub // d == kv) & (sub % d == lane % d)).astype(BF16)
        k_rep = jnp.dot(k_n, spread, preferred_element_type=F32).astype(BF16)
        v_kv = v_t[kv * d:(kv + 1) * d]
        q_g = q_ref[:, kv * grp:(kv + 1) * grp].astype(F32)
        q_n = (q_g * lax.rsqrt(_head_mean_square(q_g, pool) + EPS) * (qg_ref[...] * scale)).astype(BF16)
        bias = bias_ref[kv]
        sink = jnp.concatenate(
            [jnp.full((1, BLOCK), sink_ref[kv * AT_GROUP + g], F32) for g in range(AT_GROUP)], axis=1)
        for nb in range(ATT_TQ // BLOCK):
            rows = slice(nb * BLOCK, (nb + 1) * BLOCK)
            keys = slice(nb * BLOCK, (nb + 2) * BLOCK)
            q4 = jnp.concatenate([q_n[rows] * head_masks[g] for g in range(AT_GROUP)], axis=0)
            lg = lax.dot_general(k_rep[keys], q4, (((1,), (1,)), ((), ())), preferred_element_type=F32) + bias
            if nb == 0:
                lg = jnp.concatenate([lg[:BLOCK] + pre_seq, lg[BLOCK:]], axis=0)
            m = jnp.maximum(jnp.max(lg, axis=0, keepdims=True), sink)
            e = jnp.exp(lg - m)
            den = jnp.sum(e, axis=0, keepdims=True) + jnp.exp(sink - m)
            o_t = jnp.dot(v_kv[:, keys], e.astype(BF16), preferred_element_type=F32) * (1.0 / den)
            o = jnp.concatenate([o_t[:, g * BLOCK:(g + 1) * BLOCK] for g in range(AT_GROUP)], axis=0).T
            o_ref[rows, kv * grp:(kv + 1) * grp] = o.astype(o_ref.dtype)


def _attn(proj, bias, sinks, q_norm_g, k_norm_g):
    bsz, seq, _ = proj.shape
    tq = ATT_TQ
    blocks_per_step = tq // BLOCK
    q_col = AT_OFF // AT_WIDTH
    k_col = (AT_OFF + AT_WIDTH) // KV_WIDTH
    v_col = k_col + 1
    halo = lambda colblk: pl.BlockSpec(
        (None, BLOCK, KV_WIDTH), lambda b, i: (b, jnp.maximum(i * blocks_per_step - 1, 0), colblk))
    qg = jnp.tile(q_norm_g, (1, AT_GROUP))
    kg = jnp.tile(k_norm_g, (1, AT_KV_HEADS))
    return pl.pallas_call(
        _attn_kernel,
        grid=(bsz, seq // tq),
        in_specs=[
            pl.BlockSpec((None, tq, AT_WIDTH), lambda b, i: (b, i, q_col)),
            pl.BlockSpec((None, tq, KV_WIDTH), lambda b, i: (b, i, k_col)),
            pl.BlockSpec((None, tq, KV_WIDTH), lambda b, i: (b, i, v_col)),
            halo(k_col), halo(v_col),
            pl.BlockSpec((AT_KV_HEADS, 2 * BLOCK, AT_GROUP * BLOCK), lambda b, i: (0, 0, 0)),
            pl.BlockSpec(memory_space=pltpu.SMEM),
            pl.BlockSpec((1, AT_GROUP * AT_HEAD_DIM), lambda b, i: (0, 0)),
            pl.BlockSpec((1, KV_WIDTH), lambda b, i: (0, 0)),
        ],
        out_specs=pl.BlockSpec((None, tq, AT_WIDTH), lambda b, i: (b, i, 0)),
        out_shape=jax.ShapeDtypeStruct((bsz, seq, AT_WIDTH), BF16),
        compiler_params=_params(2),
        name="attn",
    )(proj, proj, proj, proj, proj, bias, sinks, qg, kg)


def _merge_kernel(x_ref, ohg_ref, oat_ref, ghg_ref, gat_ref, wbh_ref, wba_ref, wo_ref,
                  gate1_ref, n2g_ref, shift2_ref, scale2_ref, x1_ref, h2_ref):
    m_hg = jnp.dot(ohg_ref[...], wbh_ref[...], preferred_element_type=F32)
    m_at = jnp.dot(oat_ref[...], wba_ref[...], preferred_element_type=F32)
    merged = (jax.nn.sigmoid(ghg_ref[...].astype(F32)) * m_hg
              + jax.nn.sigmoid(gat_ref[...].astype(F32)) * m_at)
    y = jnp.dot(merged.astype(BF16), wo_ref[...], preferred_element_type=F32)
    x1 = x_ref[...] + gate1_ref[...] * y
    x1_ref[...] = x1
    h2_ref[...] = _rms_modulate(x1, n2g_ref[...], scale2_ref[...], shift2_ref[...]).astype(h2_ref.dtype)


def _merge(x, o_hg, o_at, proj, w_bh, w_ba, w_o, ada4, norm2_g):
    bsz, seq, d = x.shape
    tm = 512
    resident = lambda shape: pl.BlockSpec(shape, lambda b, i: (0, 0), pipeline_mode=pl.Buffered(1))
    ada_row = lambda k: pl.BlockSpec((None, None, 1, d), lambda b, i: (b, k, 0, 0))
    rows = lambda w, colblk: pl.BlockSpec((None, tm, w), lambda b, i: (b, i, colblk))
    return pl.pallas_call(
        _merge_kernel,
        grid=(bsz, seq // tm),
        in_specs=[
            rows(d, 0), rows(HG_WIDTH, 0), rows(AT_WIDTH, 0),
            rows(d, GATE_OFF // d), rows(d, GATE_OFF // d + 1),
            resident(w_bh.shape), resident(w_ba.shape), resident(w_o.shape),
            ada_row(2), pl.BlockSpec((1, d), lambda b, i: (0, 0)), ada_row(3), ada_row(4),
        ],
        out_specs=[rows(d, 0), rows(d, 0)],
        out_shape=[jax.ShapeDtypeStruct((bsz, seq, d), F32), jax.ShapeDtypeStruct((bsz, seq, d), BF16)],
        compiler_params=_params(2),
        name="merge",
    )(x, o_hg, o_at, proj, proj, w_bh, w_ba, w_o, ada4, norm2_g, ada4, ada4)


def _ffn_kernel(h2_ref, w1_ref, w2_ref, x1_ref, gate2_ref, o_ref, acc_ref):
    j = pl.program_id(2)

    @pl.when(j == 0)
    def _():
        acc_ref[...] = jnp.zeros_like(acc_ref)

    hid = jnp.maximum(jnp.dot(h2_ref[...], w1_ref[...], preferred_element_type=F32), 0.0)
    acc_ref[...] += jnp.dot((hid * hid).astype(BF16), w2_ref[...], preferred_element_type=F32)

    @pl.when(j == pl.num_programs(2) - 1)
    def _():
        o_ref[...] = x1_ref[...] + gate2_ref[...] * acc_ref[...]


def _ffn(h2, x1, w1, w2, ada4):
    bsz, seq, d = x1.shape
    dff = w1.shape[-1]
    tm, tf = 512, 1024
    rows = lambda: pl.BlockSpec((None, tm, d), lambda b, i, j: (b, i, 0))
    return pl.pallas_call(
        _ffn_kernel,
        grid=(bsz, seq // tm, dff // tf),
        in_specs=[
            rows(),
            pl.BlockSpec((d, tf), lambda b, i, j: (0, j)),
            pl.BlockSpec((tf, d), lambda b, i, j: (j, 0)),
            rows(),
            pl.BlockSpec((None, None, 1, d), lambda b, i, j: (b, 5, 0, 0)),
        ],
        out_specs=rows(),
        out_shape=jax.ShapeDtypeStruct((bsz, seq, d), F32),
        scratch_shapes=[pltpu.VMEM((tm, d), F32)],
        compiler_params=_params(3),
        name="ffn",
    )(h2, w1, w2, x1, ada4)


def kernel(x, c, w_ada, b_ada, norm1_g, norm2_g, w_in, hg_lb_logits, hg_out_norm_g, q_norm_g, k_norm_g,
           attn_sinks, rel_bias_table, w_branch_hg, w_branch_attn, w_out, w_ff1, w_ff2):
    assert w_ada.shape[0] == 1, "single-layer block"
    bsz = x.shape[0]
    w_in_p = jnp.concatenate([w_in[0, :, REF_GATE_OFF:], w_in[0, :, :REF_GATE_OFF]], axis=1).astype(BF16)
    w_bh = w_branch_hg[0].astype(BF16)
    w_ba = w_branch_attn[0].astype(BF16)
    w_o = w_out[0].astype(BF16)
    w1 = w_ff1[0].astype(BF16)
    w2 = w_ff2[0].astype(BF16)

    ada = _ada(c, w_ada, b_ada)
    ada4 = ada.reshape(bsz, 6, 1, D_MODEL)
    proj = _inproj(x, norm1_g, ada4, w_in_p)
    o_hg = _hgrn(proj, hg_lb_logits, hg_out_norm_g)
    bias = _attn_bias(rel_bias_table)
    o_at = _attn(proj, bias, attn_sinks[0], q_norm_g, k_norm_g)
    x1, h2 = _merge(x, o_hg, o_at, proj, w_bh, w_ba, w_o, ada4, norm2_g)
    return _ffn(h2, x1, w1, w2, ada4)
```

```python
import functools
import math

import jax
import jax.numpy as jnp
from jax import lax
from jax.experimental import pallas as pl
from jax.experimental.pallas import tpu as pltpu

F32 = jnp.float32
BF16 = jnp.bfloat16

D_MODEL = 2048
HG_HEADS = 8
HG_DK = 128
HG_DV = 128
HG_WIDTH = HG_HEADS * HG_DK
HG_CHUNK = 64
AT_HEADS = 16
AT_KV_HEADS = 4
AT_HEAD_DIM = 64
AT_GROUP = AT_HEADS // AT_KV_HEADS
AT_WIDTH = AT_HEADS * AT_HEAD_DIM
KV_WIDTH = AT_KV_HEADS * AT_HEAD_DIM
WINDOW = 128
BLOCK = 128
N_BUCKETS = 32
MAX_EXACT = N_BUCKETS // 2
MAX_DISTANCE = 128
D_FF = 4 * D_MODEL
EPS = 1e-6
NEG_INF = -1e30

GATE_OFF = 0
HG_OFF = 2 * D_MODEL
AT_OFF = HG_OFF + 4 * HG_WIDTH
IN_WIDTH = AT_OFF + AT_WIDTH + 2 * KV_WIDTH
REF_GATE_OFF = 4 * HG_WIDTH + AT_WIDTH + 2 * KV_WIDTH

V7X_VMEM_LIMIT_BYTES = 56 * 1024 * 1024


def _params(n_axes):
    return pltpu.CompilerParams(
        dimension_semantics=("arbitrary",) * n_axes,
        vmem_limit_bytes=V7X_VMEM_LIMIT_BYTES,
    )


def _rms_modulate(x, g, scale, shift):
    r = lax.rsqrt(jnp.mean(x * x, axis=-1, keepdims=True) + EPS)
    return (x * r * g) * (1.0 + scale) + shift


def _ada_kernel(c_ref, w_ref, b_ref, o_ref):
    c = c_ref[...]
    c_act = c * jax.nn.sigmoid(c)
    o_ref[...] = jnp.dot(c_act, w_ref[...], preferred_element_type=F32,
                         precision=lax.Precision.HIGHEST) + b_ref[...]


def _ada(c, w_ada, b_ada):
    bsz, d = c.shape
    n = w_ada.shape[-1]
    tn = 1024
    return pl.pallas_call(
        _ada_kernel,
        grid=(n // tn,),
        in_specs=[
            pl.BlockSpec((bsz, d), lambda j: (0, 0)),
            pl.BlockSpec((None, d, tn), lambda j: (0, 0, j)),
            pl.BlockSpec((1, tn), lambda j: (0, j)),
        ],
        out_specs=pl.BlockSpec((bsz, tn), lambda j: (0, j)),
        out_shape=jax.ShapeDtypeStruct((bsz, n), F32),
        compiler_params=_params(1),
        name="ada",
    )(c, w_ada, b_ada)


def _inproj_kernel(x_ref, g_ref, shift_ref, scale_ref, w_ref, o_ref):
    h = _rms_modulate(x_ref[...], g_ref[...], scale_ref[...], shift_ref[...])
    o_ref[...] = jnp.dot(h.astype(BF16), w_ref[...], preferred_element_type=F32).astype(o_ref.dtype)


def _inproj(x, norm_g, ada4, w_in_bf16):
    bsz, seq, d = x.shape
    n = w_in_bf16.shape[-1]
    tm, n_col_tiles = 512, 4
    tn = n // n_col_tiles
    tiles_per_seq = seq // tm
    row = lambda j, i: (i // tiles_per_seq, i % tiles_per_seq)
    return pl.pallas_call(
        _inproj_kernel,
        grid=(n_col_tiles, bsz * tiles_per_seq),
        in_specs=[
            pl.BlockSpec((None, tm, d), lambda j, i: (*row(j, i), 0)),
            pl.BlockSpec((1, d), lambda j, i: (0, 0)),
            pl.BlockSpec((None, None, 1, d), lambda j, i: (i // tiles_per_seq, 0, 0, 0)),
            pl.BlockSpec((None, None, 1, d), lambda j, i: (i // tiles_per_seq, 1, 0, 0)),
            pl.BlockSpec((d, tn), lambda j, i: (0, j)),
        ],
        out_specs=pl.BlockSpec((None, tm, tn), lambda j, i: (*row(j, i), j)),
        out_shape=jax.ShapeDtypeStruct((bsz, seq, n), BF16),
        compiler_params=_params(2),
        name="inproj",
    )(x, norm_g, ada4, ada4, w_in_bf16)


HG_GROUP = 4
HG_ROWS = HG_GROUP * HG_CHUNK
HG_UNROLL = 4


def _hgrn_kernel(q_ref, f_ref, i_ref, g_ref, lbl_ref, gn_ref, o_ref):
    seq = q_ref.shape[0]
    c, n = HG_CHUNK, HG_ROWS
    lbl = lbl_ref[...]
    e = jnp.exp(lbl - jnp.max(lbl, axis=0, keepdims=True))
    lb = e[0:1] / jnp.sum(e, axis=0, keepdims=True)
    gn = gn_ref[...]

    t = lax.broadcasted_iota(jnp.int32, (n, n), 0)
    s = lax.broadcasted_iota(jnp.int32, (n, n), 1)
    same = (t // c) == (s // c)
    tri = same & (s <= t)
    r8 = lax.broadcasted_iota(jnp.int32, (8, n), 0)
    s8 = lax.broadcasted_iota(jnp.int32, (8, n), 1)
    in_chunk = (s8 // c) == (r8 % HG_GROUP)
    stat_rows = in_chunk & ((r8 >= HG_GROUP) | ((s8 % c) < c // 2))
    cum_ops = jnp.concatenate([tri.astype(F32), stat_rows.astype(F32)], axis=0).astype(BF16)
    per_chunk = lambda rows: jnp.concatenate(
        [jnp.broadcast_to(rows[cc:cc + 1], (c, HG_DK)) for cc in range(HG_GROUP)], axis=0)
    col_chunk = lax.broadcasted_iota(jnp.int32, (HG_DV, n), 1) // c
    col_masks = [(col_chunk == cc).astype(BF16) for cc in range(HG_GROUP)]
    zeros_rows = lambda rows: jnp.zeros((rows, HG_DK), BF16)

    groups = range(HG_UNROLL)

    def body(it, s_t):
        base = pl.multiple_of(it * (n * HG_UNROLL), n * HG_UNROLL)
        sls = [pl.ds(base + u * n, n) for u in groups]
        ks, qss, hilo = [], [], []
        for u in groups:
            hf = f_ref[sls[u], :].astype(F32)
            hq = q_ref[sls[u], :].astype(F32)
            f = lb + (1.0 - lb) * jax.nn.sigmoid(hf)
            log_f = jnp.log(f)
            ks.append(1.0 - f)
            qss.append(hq * jax.nn.sigmoid(hq))
            hi = log_f.astype(BF16)
            lo = (log_f - hi.astype(F32)).astype(BF16)
            hilo.append(jnp.concatenate([hi, lo], axis=1))
        rs = [jnp.dot(cum_ops, hilo[u], preferred_element_type=F32) for u in groups]
        a_s, bms, kks, qes, decs = [], [], [], [], []
        for u in groups:
            r = rs[u][:, :HG_DK] + rs[u][:, HG_DK:]
            b, b_last = r[0:n], r[n + HG_GROUP:n + 2 * HG_GROUP]
            b_rel = b - per_chunk(r[n:n + HG_GROUP])
            a_s.append((qss[u] * jnp.exp(b_rel)).astype(BF16))
            bms.append((ks[u] * jnp.exp(-b_rel)).astype(BF16))
            kks.append((ks[u] * jnp.exp(per_chunk(b_last) - b)).astype(BF16))
            qes.append((qss[u] * jnp.exp(b)).astype(BF16))
            decs.append(jnp.exp(b_last))
        scores = [lax.dot_general(a_s[u], bms[u], (((1,), (1,)), ((), ())), preferred_element_type=F32)
                  for u in groups]
        vs = [i_ref[sls[u], :] for u in groups]
        os = [jnp.dot(jnp.where(tri, scores[u], 0.0).astype(BF16), vs[u], preferred_element_type=F32)
              for u in groups]
        upd_ts = []
        for u in groups:
            v_t = vs[u].astype(F32).T.astype(BF16)
            v_t_blocks = jnp.concatenate([v_t * col_masks[cc] for cc in range(HG_GROUP)], axis=0)
            upd_ts.append(jnp.dot(v_t_blocks, kks[u], preferred_element_type=F32))
        for u in groups:
            states = []
            for cc in range(HG_GROUP):
                states.append(s_t.astype(BF16))
                s_t = s_t * decs[u][cc:cc + 1] + upd_ts[u][cc * HG_DV:(cc + 1) * HG_DV]
            qe = qes[u]
            qe_blocks = jnp.concatenate(
                [jnp.concatenate([p for p in (zeros_rows(cc * c), qe[cc * c:(cc + 1) * c],
                                              zeros_rows(n - (cc + 1) * c)) if p.shape[0]], axis=0)
                 for cc in range(HG_GROUP)], axis=1)
            os[u] = os[u] + lax.dot_general(qe_blocks, jnp.concatenate(states, axis=1),
                                            (((1,), (1,)), ((), ())), preferred_element_type=F32)
        for u in groups:
            o = os[u]
            hg = g_ref[sls[u], :].astype(F32)
            y = o * lax.rsqrt(jnp.mean(o * o, axis=-1, keepdims=True) + EPS) * gn
            o_ref[sls[u], :] = (y * (hg * jax.nn.sigmoid(hg))).astype(o_ref.dtype)
        return s_t

    lax.fori_loop(0, seq // (n * HG_UNROLL), body, jnp.zeros((HG_DV, HG_DK), F32))


def _hgrn(proj, lb_logits, out_norm_g):
    bsz, seq, _ = proj.shape
    base = HG_OFF // HG_DK
    col = lambda k: pl.BlockSpec((None, seq, HG_DK), lambda b, h: (b, 0, base + k * HG_HEADS + h))
    return pl.pallas_call(
        _hgrn_kernel,
        grid=(bsz, HG_HEADS),
        in_specs=[
            col(0), col(1), col(2), col(3),
            pl.BlockSpec((lb_logits.shape[0], HG_DK), lambda b, h: (0, h)),
            pl.BlockSpec((1, HG_DV), lambda b, h: (0, 0)),
        ],
        out_specs=pl.BlockSpec((None, seq, HG_DV), lambda b, h: (b, 0, h)),
        out_shape=jax.ShapeDtypeStruct((bsz, seq, HG_WIDTH), BF16),
        compiler_params=_params(2),
        name="hgrn",
    )(proj, proj, proj, proj, lb_logits, out_norm_g)


def _t5_causal_bucket(n):
    nf = jnp.maximum(n, 1).astype(F32)
    large = MAX_EXACT + (jnp.log(nf / MAX_EXACT) / math.log(MAX_DISTANCE / MAX_EXACT)
                         * (N_BUCKETS - MAX_EXACT)).astype(jnp.int32)
    large = jnp.minimum(large, N_BUCKETS - 1)
    return jnp.where(n < MAX_EXACT, n, large)


def _bias_kernel(table_ref, bucket_t_ref, o_ref):
    bucket_t = bucket_t_ref[...]
    j = lax.broadcasted_iota(jnp.int32, bucket_t.shape, 0)
    i = lax.broadcasted_iota(jnp.int32, bucket_t.shape, 1)
    dist = i - j + BLOCK
    band = (dist >= 0) & (dist < WINDOW)
    for h in range(AT_HEADS):
        acc = jnp.zeros(bucket_t.shape, F32)
        for r in range(N_BUCKETS):
            acc = jnp.where(bucket_t == r, table_ref[r, h], acc)
        g = h % AT_GROUP
        o_ref[h // AT_GROUP, :, g * BLOCK:(g + 1) * BLOCK] = jnp.where(band, acc, NEG_INF)


def _attn_bias(rel_bias_table):
    i = jnp.arange(BLOCK, dtype=jnp.int32)[None, :]
    j = jnp.arange(2 * BLOCK, dtype=jnp.int32)[:, None]
    bucket_t = _t5_causal_bucket(jnp.maximum(i - j + BLOCK, 0))
    shape = (AT_KV_HEADS, 2 * BLOCK, AT_GROUP * BLOCK)
    return pl.pallas_call(
        _bias_kernel,
        in_specs=[
            pl.BlockSpec(memory_space=pltpu.SMEM),
            pl.BlockSpec((2 * BLOCK, BLOCK), lambda: (0, 0)),
        ],
        out_specs=pl.BlockSpec(shape, lambda: (0, 0, 0)),
        out_shape=jax.ShapeDtypeStruct(shape, F32),
        name="attn_bias",
    )(rel_bias_table, bucket_t)


ATT_TQ = 512


def _head_mean_square(t, pool):
    return jnp.dot((t * t).astype(BF16), pool, preferred_element_type=F32)


def _attn_kernel(q_ref, k_ref, v_ref, kh_ref, vh_ref, bias_ref, sink_ref, qg_ref, kg_ref, o_ref):
    step = pl.program_id(1)
    d = AT_HEAD_DIM
    grp = AT_GROUP * d
    sub = lax.broadcasted_iota(jnp.int32, (grp, grp), 0)
    lane = lax.broadcasted_iota(jnp.int32, (grp, grp), 1)
    pool = jnp.where(sub // d == lane // d, 1.0 / d, 0.0).astype(BF16)

    k_all = jnp.concatenate([kh_ref[...], k_ref[...]], axis=0).astype(F32)
    k_n = (k_all * lax.rsqrt(_head_mean_square(k_all, pool) + EPS) * kg_ref[...]).astype(BF16)
    v_t = jnp.concatenate([vh_ref[...], v_ref[...]], axis=0).astype(F32).T.astype(BF16)

    lane_head = lax.broadcasted_iota(jnp.int32, (BLOCK, grp), 1) // d
    head_masks = [(lane_head == g).astype(BF16) for g in range(AT_GROUP)]
    pre_seq = jnp.where(step == 0, NEG_INF, 0.0)

    scale = d ** -0.5
    k_reps, q_ns = [], []
    for kv in range(AT_KV_HEADS):
        spread = ((sub // d == kv) & (sub % d == lane % d)).astype(BF16)
        k_reps.append(jnp.dot(k_n, spread, preferred_element_type=F32).astype(BF16))
        q_g = q_ref[:, kv * grp:(kv + 1) * grp].astype(F32)
        q_ns.append((q_g * lax.rsqrt(_head_mean_square(q_g, pool) + EPS) * (qg_ref[...] * scale)).astype(BF16))

    def logits_t(kv, nb):
        rows = slice(nb * BLOCK, (nb + 1) * BLOCK)
        keys = slice(nb * BLOCK, (nb + 2) * BLOCK)
        q4 = jnp.concatenate([q_ns[kv][rows] * head_masks[g] for g in range(AT_GROUP)], axis=0)
        lg = lax.dot_general(k_reps[kv][keys], q4, (((1,), (1,)), ((), ())),
                             preferred_element_type=F32) + bias_ref[kv]
        if nb == 0:
            lg = jnp.concatenate([lg[:BLOCK] + pre_seq, lg[BLOCK:]], axis=0)
        return lg

    units = [(kv, nb) for kv in range(AT_KV_HEADS) for nb in range(ATT_TQ // BLOCK)]
    lg_next = logits_t(*units[0])
    for idx, (kv, nb) in enumerate(units):
        lg = lg_next
        if idx + 1 < len(units):
            lg_next = logits_t(*units[idx + 1])
        rows = slice(nb * BLOCK, (nb + 1) * BLOCK)
        keys = slice(nb * BLOCK, (nb + 2) * BLOCK)
        sink = jnp.concatenate(
            [jnp.full((1, BLOCK), sink_ref[kv * AT_GROUP + g], F32) for g in range(AT_GROUP)], axis=1)
        m = jnp.maximum(jnp.max(lg, axis=0, keepdims=True), sink)
        e = jnp.exp(lg - m)
        den = jnp.sum(e, axis=0, keepdims=True) + jnp.exp(sink - m)
        o_t = jnp.dot(v_t[kv * d:(kv + 1) * d, keys], e.astype(BF16),
                      preferred_element_type=F32) * (1.0 / den)
        o = jnp.concatenate([o_t[:, g * BLOCK:(g + 1) * BLOCK] for g in range(AT_GROUP)], axis=0).T
        o_ref[rows, kv * grp:(kv + 1) * grp] = o.astype(o_ref.dtype)


def _attn(proj, bias, sinks, q_norm_g, k_norm_g):
    bsz, seq, _ = proj.shape
    tq = ATT_TQ
    blocks_per_step = tq // BLOCK
    q_col = AT_OFF // AT_WIDTH
    k_col = (AT_OFF + AT_WIDTH) // KV_WIDTH
    v_col = k_col + 1
    halo = lambda colblk: pl.BlockSpec(
        (None, BLOCK, KV_WIDTH), lambda b, i: (b, jnp.maximum(i * blocks_per_step - 1, 0), colblk))
    qg = jnp.tile(q_norm_g, (1, AT_GROUP))
    kg = jnp.tile(k_norm_g, (1, AT_KV_HEADS))
    return pl.pallas_call(
        _attn_kernel,
        grid=(bsz, seq // tq),
        in_specs=[
            pl.BlockSpec((None, tq, AT_WIDTH), lambda b, i: (b, i, q_col)),
            pl.BlockSpec((None, tq, KV_WIDTH), lambda b, i: (b, i, k_col)),
            pl.BlockSpec((None, tq, KV_WIDTH), lambda b, i: (b, i, v_col)),
            halo(k_col), halo(v_col),
            pl.BlockSpec((AT_KV_HEADS, 2 * BLOCK, AT_GROUP * BLOCK), lambda b, i: (0, 0, 0)),
            pl.BlockSpec(memory_space=pltpu.SMEM),
            pl.BlockSpec((1, AT_GROUP * AT_HEAD_DIM), lambda b, i: (0, 0)),
            pl.BlockSpec((1, KV_WIDTH), lambda b, i: (0, 0)),
        ],
        out_specs=pl.BlockSpec((None, tq, AT_WIDTH), lambda b, i: (b, i, 0)),
        out_shape=jax.ShapeDtypeStruct((bsz, seq, AT_WIDTH), BF16),
        compiler_params=_params(2),
        name="attn",
    )(proj, proj, proj, proj, proj, bias, sinks, qg, kg)


def _merge_kernel(x_ref, ohg_ref, oat_ref, ghg_ref, gat_ref, wbh_ref, wba_ref, wo_ref,
                  gate1_ref, n2g_ref, shift2_ref, scale2_ref, x1_ref, h2_ref):
    m_hg = jnp.dot(ohg_ref[...], wbh_ref[...], preferred_element_type=F32)
    m_at = jnp.dot(oat_ref[...], wba_ref[...], preferred_element_type=F32)
    merged = (jax.nn.sigmoid(ghg_ref[...].astype(F32)) * m_hg
              + jax.nn.sigmoid(gat_ref[...].astype(F32)) * m_at)
    y = jnp.dot(merged.astype(BF16), wo_ref[...], preferred_element_type=F32)
    x1 = x_ref[...] + gate1_ref[...] * y
    x1_ref[...] = x1
    h2_ref[...] = _rms_modulate(x1, n2g_ref[...], scale2_ref[...], shift2_ref[...]).astype(h2_ref.dtype)


def _merge(x, o_hg, o_at, proj, w_bh, w_ba, w_o, ada4, norm2_g):
    bsz, seq, d = x.shape
    tm = 512
    resident = lambda shape: pl.BlockSpec(shape, lambda b, i: (0, 0), pipeline_mode=pl.Buffered(1))
    ada_row = lambda k: pl.BlockSpec((None, None, 1, d), lambda b, i: (b, k, 0, 0))
    rows = lambda w, colblk: pl.BlockSpec((None, tm, w), lambda b, i: (b, i, colblk))
    return pl.pallas_call(
        _merge_kernel,
        grid=(bsz, seq // tm),
        in_specs=[
            rows(d, 0), rows(HG_WIDTH, 0), rows(AT_WIDTH, 0),
            rows(d, GATE_OFF // d), rows(d, GATE_OFF // d + 1),
            resident(w_bh.shape), resident(w_ba.shape), resident(w_o.shape),
            ada_row(2), pl.BlockSpec((1, d), lambda b, i: (0, 0)), ada_row(3), ada_row(4),
        ],
        out_specs=[rows(d, 0), rows(d, 0)],
        out_shape=[jax.ShapeDtypeStruct((bsz, seq, d), F32), jax.ShapeDtypeStruct((bsz, seq, d), BF16)],
        compiler_params=_params(2),
        name="merge",
    )(x, o_hg, o_at, proj, proj, w_bh, w_ba, w_o, ada4, norm2_g, ada4, ada4)


def _ffn_kernel(h2_ref, w1_ref, w2_ref, x1_ref, gate2_ref, o_ref, acc_ref):
    j = pl.program_id(2)

    @pl.when(j == 0)
    def _():
        acc_ref[...] = jnp.zeros_like(acc_ref)

    hid = jnp.maximum(jnp.dot(h2_ref[...], w1_ref[...], preferred_element_type=F32), 0.0)
    acc_ref[...] += jnp.dot((hid * hid).astype(BF16), w2_ref[...], preferred_element_type=F32)

    @pl.when(j == pl.num_programs(2) - 1)
    def _():
        o_ref[...] = x1_ref[...] + gate2_ref[...] * acc_ref[...]


def _ffn(h2, x1, w1, w2, ada4):
    bsz, seq, d = x1.shape
    dff = w1.shape[-1]
    tm, tf = 512, 1024
    rows = lambda: pl.BlockSpec((None, tm, d), lambda b, i, j: (b, i, 0))
    return pl.pallas_call(
        _ffn_kernel,
        grid=(bsz, seq // tm, dff // tf),
        in_specs=[
            rows(),
            pl.BlockSpec((d, tf), lambda b, i, j: (0, j)),
            pl.BlockSpec((tf, d), lambda b, i, j: (j, 0)),
            rows(),
            pl.BlockSpec((None, None, 1, d), lambda b, i, j: (b, 5, 0, 0)),
        ],
        out_specs=rows(),
        out_shape=jax.ShapeDtypeStruct((bsz, seq, d), F32),
        scratch_shapes=[pltpu.VMEM((tm, d), F32)],
        compiler_params=_params(3),
        name="ffn",
    )(h2, w1, w2, x1, ada4)


def kernel(x, c, w_ada, b_ada, norm1_g, norm2_g, w_in, hg_lb_logits, hg_out_norm_g, q_norm_g, k_norm_g,
           attn_sinks, rel_bias_table, w_branch_hg, w_branch_attn, w_out, w_ff1, w_ff2):
    assert w_ada.shape[0] == 1, "single-layer block"
    bsz = x.shape[0]
    w_in_p = jnp.concatenate([w_in[0, :, REF_GATE_OFF:], w_in[0, :, :REF_GATE_OFF]], axis=1).astype(BF16)
    w_bh = w_branch_hg[0].astype(BF16)
    w_ba = w_branch_attn[0].astype(BF16)
    w_o = w_out[0].astype(BF16)
    w1 = w_ff1[0].astype(BF16)
    w2 = w_ff2[0].astype(BF16)

    ada = _ada(c, w_ada, b_ada)
    ada4 = ada.reshape(bsz, 6, 1, D_MODEL)
    proj = _inproj(x, norm1_g, ada4, w_in_p)
    o_hg = _hgrn(proj, hg_lb_logits, hg_out_norm_g)
    bias = _attn_bias(rel_bias_table)
    o_at = _attn(proj, bias, attn_sinks[0], q_norm_g, k_norm_g)
    x1, h2 = _merge(x, o_hg, o_at, proj, w_bh, w_ba, w_o, ada4, norm2_g)
    return _ffn(h2, x1, w1, w2, ada4)
```

```python
import functools
import math

import jax
import jax.numpy as jnp
from jax import lax
from jax.experimental import pallas as pl
from jax.experimental.pallas import tpu as pltpu

F32 = jnp.float32
BF16 = jnp.bfloat16

D_MODEL = 2048
HG_HEADS = 8
HG_DK = 128
HG_DV = 128
HG_WIDTH = HG_HEADS * HG_DK
HG_CHUNK = 64
AT_HEADS = 16
AT_KV_HEADS = 4
AT_HEAD_DIM = 64
AT_GROUP = AT_HEADS // AT_KV_HEADS
AT_WIDTH = AT_HEADS * AT_HEAD_DIM
KV_WIDTH = AT_KV_HEADS * AT_HEAD_DIM
WINDOW = 128
BLOCK = 128
N_BUCKETS = 32
MAX_EXACT = N_BUCKETS // 2
MAX_DISTANCE = 128
D_FF = 4 * D_MODEL
EPS = 1e-6
NEG_INF = -1e30

HG_OFF = 0
AT_OFF = 4 * HG_WIDTH
GATE_OFF = AT_OFF + AT_WIDTH + 2 * KV_WIDTH
IN_WIDTH = GATE_OFF + 2 * D_MODEL
GATE_BLOCK = 512

V7X_VMEM_LIMIT_BYTES = 60 * 1024 * 1024
BF16_SUBLANES = 16


def _params(n_axes):
    return pltpu.CompilerParams(
        dimension_semantics=("arbitrary",) * n_axes,
        vmem_limit_bytes=V7X_VMEM_LIMIT_BYTES,
    )


def _rms_modulate(x, g, scale, shift):
    r = lax.rsqrt(jnp.mean(x * x, axis=-1, keepdims=True) + EPS)
    return (x * r * g) * (1.0 + scale) + shift


def _ada_kernel(c_ref, w_ref, b_ref, o_ref):
    c = c_ref[...]
    c_act = c * jax.nn.sigmoid(c)
    o_ref[...] = jnp.dot(c_act, w_ref[...], preferred_element_type=F32,
                         precision=lax.Precision.HIGHEST) + b_ref[...]


def _ada(c, w_ada, b_ada):
    bsz, d = c.shape
    n = w_ada.shape[-1]
    tn = 1024
    return pl.pallas_call(
        _ada_kernel,
        grid=(n // tn,),
        in_specs=[
            pl.BlockSpec((bsz, d), lambda j: (0, 0)),
            pl.BlockSpec((None, d, tn), lambda j: (0, 0, j)),
            pl.BlockSpec((1, tn), lambda j: (0, j)),
        ],
        out_specs=pl.BlockSpec((bsz, tn), lambda j: (0, j)),
        out_shape=jax.ShapeDtypeStruct((bsz, n), F32),
        compiler_params=_params(1),
        name="ada",
    )(c, w_ada, b_ada)


def _inproj_kernel(n_late, x_ref, g_ref, shift_ref, scale_ref, w_ref, *refs):
    late_in, (o_ref, *late_out) = refs[:n_late], refs[n_late:]
    h = _rms_modulate(x_ref[...], g_ref[...], scale_ref[...], shift_ref[...])
    o_ref[...] = jnp.dot(h.astype(BF16), w_ref[...], preferred_element_type=F32).astype(o_ref.dtype)
    for src, dst in zip(late_in, late_out):
        dst[...] = src[...].astype(dst.dtype)


def _inproj(x, norm_g, ada4, w_in_bf16, late_weights):
    bsz, seq, d = x.shape
    n = w_in_bf16.shape[-1]
    tm, n_col_tiles = 512, 4
    tn = n // n_col_tiles
    tiles_per_seq = seq // tm
    n_row_tiles = bsz * tiles_per_seq
    n_steps = n_col_tiles * n_row_tiles
    row = lambda j, i: (i // tiles_per_seq, i % tiles_per_seq)
    late_in_specs, late_out_specs, late_shapes = [], [], []
    for w in late_weights:
        _, rows, cols = w.shape
        slab = max(rows // n_steps, BF16_SUBLANES)
        assert rows % slab == 0
        idx = lambda j, i, last=rows // slab - 1: (jnp.minimum(j * n_row_tiles + i, last), 0)
        late_in_specs.append(pl.BlockSpec((None, slab, cols), lambda j, i, idx=idx: (0, *idx(j, i))))
        late_out_specs.append(pl.BlockSpec((slab, cols), idx))
        late_shapes.append(jax.ShapeDtypeStruct((rows, cols), BF16))
    outs = pl.pallas_call(
        functools.partial(_inproj_kernel, len(late_weights)),
        grid=(n_col_tiles, n_row_tiles),
        in_specs=[
            pl.BlockSpec((None, tm, d), lambda j, i: (*row(j, i), 0)),
            pl.BlockSpec((1, d), lambda j, i: (0, 0)),
            pl.BlockSpec((None, None, 1, d), lambda j, i: (i // tiles_per_seq, 0, 0, 0)),
            pl.BlockSpec((None, None, 1, d), lambda j, i: (i // tiles_per_seq, 1, 0, 0)),
            pl.BlockSpec((d, tn), lambda j, i: (0, j)),
            *late_in_specs,
        ],
        out_specs=[pl.BlockSpec((None, tm, tn), lambda j, i: (*row(j, i), j)), *late_out_specs],
        out_shape=[jax.ShapeDtypeStruct((bsz, seq, n), BF16), *late_shapes],
        compiler_params=_params(2),
        name="inproj",
    )(x, norm_g, ada4, ada4, w_in_bf16, *late_weights)
    return outs[0], outs[1:]


HG_GROUP = 4
HG_ROWS = HG_GROUP * HG_CHUNK
HG_UNROLL = 4


def _hgrn_kernel(q_ref, f_ref, i_ref, g_ref, lbl_ref, gn_ref, o_ref):
    seq = q_ref.shape[0]
    c, n = HG_CHUNK, HG_ROWS
    lbl = lbl_ref[...]
    e = jnp.exp(lbl - jnp.max(lbl, axis=0, keepdims=True))
    lb = e[0:1] / jnp.sum(e, axis=0, keepdims=True)
    gn = gn_ref[...]

    t = lax.broadcasted_iota(jnp.int32, (n, n), 0)
    s = lax.broadcasted_iota(jnp.int32, (n, n), 1)
    same = (t // c) == (s // c)
    tri = same & (s <= t)
    r8 = lax.broadcasted_iota(jnp.int32, (8, n), 0)
    s8 = lax.broadcasted_iota(jnp.int32, (8, n), 1)
    in_chunk = (s8 // c) == (r8 % HG_GROUP)
    stat_rows = in_chunk & ((r8 >= HG_GROUP) | ((s8 % c) < c // 2))
    cum_ops = jnp.concatenate([tri.astype(F32), stat_rows.astype(F32)], axis=0).astype(BF16)
    per_chunk = lambda rows: jnp.concatenate(
        [jnp.broadcast_to(rows[cc:cc + 1], (c, HG_DK)) for cc in range(HG_GROUP)], axis=0)
    col_chunk = lax.broadcasted_iota(jnp.int32, (HG_DV, n), 1) // c
    col_masks = [(col_chunk == cc).astype(BF16) for cc in range(HG_GROUP)]
    zeros_rows = lambda rows: jnp.zeros((rows, HG_DK), BF16)

    groups = range(HG_UNROLL)

    def body(it, s_t):
        base = pl.multiple_of(it * (n * HG_UNROLL), n * HG_UNROLL)
        sls = [pl.ds(base + u * n, n) for u in groups]
        ks, qss, hilo = [], [], []
        for u in groups:
            hf = f_ref[sls[u], :].astype(F32)
            hq = q_ref[sls[u], :].astype(F32)
            f = lb + (1.0 - lb) * jax.nn.sigmoid(hf)
            log_f = jnp.log(f)
            ks.append(1.0 - f)
            qss.append(hq * jax.nn.sigmoid(hq))
            hi = log_f.astype(BF16)
            lo = (log_f - hi.astype(F32)).astype(BF16)
            hilo.append(jnp.concatenate([hi, lo], axis=1))
        rs = [jnp.dot(cum_ops, hilo[u], preferred_element_type=F32) for u in groups]
        a_s, bms, kks, qes, decs = [], [], [], [], []
        for u in groups:
            r = rs[u][:, :HG_DK] + rs[u][:, HG_DK:]
            b, b_last = r[0:n], r[n + HG_GROUP:n + 2 * HG_GROUP]
            b_rel = b - per_chunk(r[n:n + HG_GROUP])
            a_s.append((qss[u] * jnp.exp(b_rel)).astype(BF16))
            bms.append((ks[u] * jnp.exp(-b_rel)).astype(BF16))
            kks.append((ks[u] * jnp.exp(per_chunk(b_last) - b)).astype(BF16))
            qes.append((qss[u] * jnp.exp(b)).astype(BF16))
            decs.append(jnp.exp(b_last))
        scores = [lax.dot_general(a_s[u], bms[u], (((1,), (1,)), ((), ())), preferred_element_type=F32)
                  for u in groups]
        vs = [i_ref[sls[u], :] for u in groups]
        os = [jnp.dot(jnp.where(tri, scores[u], 0.0).astype(BF16), vs[u], preferred_element_type=F32)
              for u in groups]
        upd_ts = []
        for u in groups:
            v_t = vs[u].astype(F32).T.astype(BF16)
            v_t_blocks = jnp.concatenate([v_t * col_masks[cc] for cc in range(HG_GROUP)], axis=0)
            upd_ts.append(jnp.dot(v_t_blocks, kks[u], preferred_element_type=F32))
        for u in groups:
            states = []
            for cc in range(HG_GROUP):
                states.append(s_t.astype(BF16))
                s_t = s_t * decs[u][cc:cc + 1] + upd_ts[u][cc * HG_DV:(cc + 1) * HG_DV]
            qe = qes[u]
            qe_blocks = jnp.concatenate(
                [jnp.concatenate([p for p in (zeros_rows(cc * c), qe[cc * c:(cc + 1) * c],
                                              zeros_rows(n - (cc + 1) * c)) if p.shape[0]], axis=0)
                 for cc in range(HG_GROUP)], axis=1)
            os[u] = os[u] + lax.dot_general(qe_blocks, jnp.concatenate(states, axis=1),
                                            (((1,), (1,)), ((), ())), preferred_element_type=F32)
        for u in groups:
            o = os[u]
            hg = g_ref[sls[u], :].astype(F32)
            y = o * lax.rsqrt(jnp.mean(o * o, axis=-1, keepdims=True) + EPS) * gn
            o_ref[sls[u], :] = (y * (hg * jax.nn.sigmoid(hg))).astype(o_ref.dtype)
        return s_t

    lax.fori_loop(0, seq // (n * HG_UNROLL), body, jnp.zeros((HG_DV, HG_DK), F32))


def _hgrn(proj, lb_logits, out_norm_g):
    bsz, seq, _ = proj.shape
    base = HG_OFF // HG_DK
    col = lambda k: pl.BlockSpec((None, seq, HG_DK), lambda b, h: (b, 0, base + k * HG_HEADS + h))
    return pl.pallas_call(
        _hgrn_kernel,
        grid=(bsz, HG_HEADS),
        in_specs=[
            col(0), col(1), col(2), col(3),
            pl.BlockSpec((lb_logits.shape[0], HG_DK), lambda b, h: (0, h)),
            pl.BlockSpec((1, HG_DV), lambda b, h: (0, 0)),
        ],
        out_specs=pl.BlockSpec((None, seq, HG_DV), lambda b, h: (b, 0, h)),
        out_shape=jax.ShapeDtypeStruct((bsz, seq, HG_WIDTH), BF16),
        compiler_params=_params(2),
        name="hgrn",
    )(proj, proj, proj, proj, lb_logits, out_norm_g)


def _t5_causal_bucket(n):
    nf = jnp.maximum(n, 1).astype(F32)
    large = MAX_EXACT + (jnp.log(nf / MAX_EXACT) / math.log(MAX_DISTANCE / MAX_EXACT)
                         * (N_BUCKETS - MAX_EXACT)).astype(jnp.int32)
    large = jnp.minimum(large, N_BUCKETS - 1)
    return jnp.where(n < MAX_EXACT, n, large)


def _bias_kernel(table_ref, bucket_t_ref, o_ref):
    bucket_t = bucket_t_ref[...]
    j = lax.broadcasted_iota(jnp.int32, bucket_t.shape, 0)
    i = lax.broadcasted_iota(jnp.int32, bucket_t.shape, 1)
    dist = i - j + BLOCK
    band = (dist >= 0) & (dist < WINDOW)
    for h in range(AT_HEADS):
        acc = jnp.zeros(bucket_t.shape, F32)
        for r in range(N_BUCKETS):
            acc = jnp.where(bucket_t == r, table_ref[r, h], acc)
        g = h % AT_GROUP
        o_ref[h // AT_GROUP, :, g * BLOCK:(g + 1) * BLOCK] = jnp.where(band, acc, NEG_INF)


def _attn_bias(rel_bias_table):
    i = jnp.arange(BLOCK, dtype=jnp.int32)[None, :]
    j = jnp.arange(2 * BLOCK, dtype=jnp.int32)[:, None]
    bucket_t = _t5_causal_bucket(jnp.maximum(i - j + BLOCK, 0))
    shape = (AT_KV_HEADS, 2 * BLOCK, AT_GROUP * BLOCK)
    return pl.pallas_call(
        _bias_kernel,
        in_specs=[
            pl.BlockSpec(memory_space=pltpu.SMEM),
            pl.BlockSpec((2 * BLOCK, BLOCK), lambda: (0, 0)),
        ],
        out_specs=pl.BlockSpec(shape, lambda: (0, 0, 0)),
        out_shape=jax.ShapeDtypeStruct(shape, F32),
        name="attn_bias",
    )(rel_bias_table, bucket_t)


ATT_TQ = 512


def _head_mean_square(t, pool):
    return jnp.dot((t * t).astype(BF16), pool, preferred_element_type=F32)


def _attn_kernel(q_ref, k_ref, v_ref, kh_ref, vh_ref, bias_ref, sink_ref, qg_ref, kg_ref, o_ref):
    step = pl.program_id(1)
    d = AT_HEAD_DIM
    grp = AT_GROUP * d
    sub = lax.broadcasted_iota(jnp.int32, (grp, grp), 0)
    lane = lax.broadcasted_iota(jnp.int32, (grp, grp), 1)
    pool = jnp.where(sub // d == lane // d, 1.0 / d, 0.0).astype(BF16)

    k_all = jnp.concatenate([kh_ref[...], k_ref[...]], axis=0).astype(F32)
    k_n = (k_all * lax.rsqrt(_head_mean_square(k_all, pool) + EPS) * kg_ref[...]).astype(BF16)
    v_t = jnp.concatenate([vh_ref[...], v_ref[...]], axis=0).astype(F32).T.astype(BF16)

    lane_head = lax.broadcasted_iota(jnp.int32, (BLOCK, grp), 1) // d
    head_masks = [(lane_head == g).astype(BF16) for g in range(AT_GROUP)]
    pre_seq = jnp.where(step == 0, NEG_INF, 0.0)

    scale = d ** -0.5
    k_reps, q_ns = [], []
    for kv in range(AT_KV_HEADS):
        spread = ((sub // d == kv) & (sub % d == lane % d)).astype(BF16)
        k_reps.append(jnp.dot(k_n, spread, preferred_element_type=F32).astype(BF16))
        q_g = q_ref[:, kv * grp:(kv + 1) * grp].astype(F32)
        q_ns.append((q_g * lax.rsqrt(_head_mean_square(q_g, pool) + EPS) * (qg_ref[...] * scale)).astype(BF16))

    def logits_t(kv, nb):
        rows = slice(nb * BLOCK, (nb + 1) * BLOCK)
        keys = slice(nb * BLOCK, (nb + 2) * BLOCK)
        q4 = jnp.concatenate([q_ns[kv][rows] * head_masks[g] for g in range(AT_GROUP)], axis=0)
        lg = lax.dot_general(k_reps[kv][keys], q4, (((1,), (1,)), ((), ())),
                             preferred_element_type=F32) + bias_ref[kv]
        if nb == 0:
            lg = jnp.concatenate([lg[:BLOCK] + pre_seq, lg[BLOCK:]], axis=0)
        return lg

    units = [(kv, nb) for kv in range(AT_KV_HEADS) for nb in range(ATT_TQ // BLOCK)]
    lg_next = logits_t(*units[0])
    for idx, (kv, nb) in enumerate(units):
        lg = lg_next
        if idx + 1 < len(units):
            lg_next = logits_t(*units[idx + 1])
        rows = slice(nb * BLOCK, (nb + 1) * BLOCK)
        keys = slice(nb * BLOCK, (nb + 2) * BLOCK)
        sink = jnp.concatenate(
            [jnp.full((1, BLOCK), sink_ref[kv * AT_GROUP + g], F32) for g in range(AT_GROUP)], axis=1)
        m = jnp.maximum(jnp.max(lg, axis=0, keepdims=True), sink)
        e = jnp.exp(lg - m)
        den = jnp.sum(e, axis=0, keepdims=True) + jnp.exp(sink - m)
        o_t = jnp.dot(v_t[kv * d:(kv + 1) * d, keys], e.astype(BF16),
                      preferred_element_type=F32) * (1.0 / den)
        o = jnp.concatenate([o_t[:, g * BLOCK:(g + 1) * BLOCK] for g in range(AT_GROUP)], axis=0).T
        o_ref[rows, kv * grp:(kv + 1) * grp] = o.astype(o_ref.dtype)


def _attn(proj, bias, sinks, q_norm_g, k_norm_g):
    bsz, seq, _ = proj.shape
    tq = ATT_TQ
    blocks_per_step = tq // BLOCK
    q_col = AT_OFF // AT_WIDTH
    k_col = (AT_OFF + AT_WIDTH) // KV_WIDTH
    v_col = k_col + 1
    halo = lambda colblk: pl.BlockSpec(
        (None, BLOCK, KV_WIDTH), lambda b, i: (b, jnp.maximum(i * blocks_per_step - 1, 0), colblk))
    qg = jnp.tile(q_norm_g, (1, AT_GROUP))
    kg = jnp.tile(k_norm_g, (1, AT_KV_HEADS))
    return pl.pallas_call(
        _attn_kernel,
        grid=(bsz, seq // tq),
        in_specs=[
            pl.BlockSpec((None, tq, AT_WIDTH), lambda b, i: (b, i, q_col)),
            pl.BlockSpec((None, tq, KV_WIDTH), lambda b, i: (b, i, k_col)),
            pl.BlockSpec((None, tq, KV_WIDTH), lambda b, i: (b, i, v_col)),
            halo(k_col), halo(v_col),
            pl.BlockSpec((AT_KV_HEADS, 2 * BLOCK, AT_GROUP * BLOCK), lambda b, i: (0, 0, 0)),
            pl.BlockSpec(memory_space=pltpu.SMEM),
            pl.BlockSpec((1, AT_GROUP * AT_HEAD_DIM), lambda b, i: (0, 0)),
            pl.BlockSpec((1, KV_WIDTH), lambda b, i: (0, 0)),
        ],
        out_specs=pl.BlockSpec((None, tq, AT_WIDTH), lambda b, i: (b, i, 0)),
        out_shape=jax.ShapeDtypeStruct((bsz, seq, AT_WIDTH), BF16),
        compiler_params=_params(2),
        name="attn",
    )(proj, proj, proj, proj, proj, bias, sinks, qg, kg)


MERGE_SPLIT = 2


def _merge_kernel(n_gate, x_ref, ohg_ref, oat_ref, *refs):
    ghg_refs, gat_refs = refs[:n_gate], refs[n_gate:2 * n_gate]
    (wbh_ref, wba_ref, wo_ref, gate1_ref, n2g_ref, shift2_ref, scale2_ref, x1_ref, h2_ref) = refs[2 * n_gate:]
    part = x_ref.shape[0] // MERGE_SPLIT
    parts = [slice(p * part, (p + 1) * part) for p in range(MERGE_SPLIT)]
    gate = lambda g_refs, rows: jax.nn.sigmoid(
        jnp.concatenate([g[rows, :] for g in g_refs], axis=1).astype(F32))
    m_hg = [jnp.dot(ohg_ref[rows, :], wbh_ref[...], preferred_element_type=F32) for rows in parts]
    m_at = [jnp.dot(oat_ref[rows, :], wba_ref[...], preferred_element_type=F32) for rows in parts]
    ys = []
    for p, rows in enumerate(parts):
        merged = gate(ghg_refs, rows) * m_hg[p] + gate(gat_refs, rows) * m_at[p]
        ys.append(jnp.dot(merged.astype(BF16), wo_ref[...], preferred_element_type=F32))
    for p, rows in enumerate(parts):
        x1 = x_ref[rows, :] + gate1_ref[...] * ys[p]
        x1_ref[rows, :] = x1
        h2_ref[rows, :] = _rms_modulate(x1, n2g_ref[...], scale2_ref[...], shift2_ref[...]).astype(h2_ref.dtype)


def _merge(x, o_hg, o_at, proj, w_bh, w_ba, w_o, ada4, norm2_g):
    bsz, seq, d = x.shape
    tm = 512
    n_gate = d // GATE_BLOCK
    resident = lambda shape: pl.BlockSpec(shape, lambda b, i: (0, 0), pipeline_mode=pl.Buffered(1))
    ada_row = lambda k: pl.BlockSpec((None, None, 1, d), lambda b, i: (b, k, 0, 0))
    rows = lambda w, colblk: pl.BlockSpec((None, tm, w), lambda b, i: (b, i, colblk))
    gates = lambda branch: [rows(GATE_BLOCK, GATE_OFF // GATE_BLOCK + branch * n_gate + k) for k in range(n_gate)]
    return pl.pallas_call(
        functools.partial(_merge_kernel, n_gate),
        grid=(bsz, seq // tm),
        in_specs=[
            rows(d, 0), rows(HG_WIDTH, 0), rows(AT_WIDTH, 0),
            *gates(0), *gates(1),
            resident(w_bh.shape), resident(w_ba.shape), resident(w_o.shape),
            ada_row(2), pl.BlockSpec((1, d), lambda b, i: (0, 0)), ada_row(3), ada_row(4),
        ],
        out_specs=[rows(d, 0), rows(d, 0)],
        out_shape=[jax.ShapeDtypeStruct((bsz, seq, d), F32), jax.ShapeDtypeStruct((bsz, seq, d), BF16)],
        compiler_params=_params(2),
        name="merge",
    )(x, o_hg, o_at, *([proj] * (2 * n_gate)), w_bh, w_ba, w_o, ada4, norm2_g, ada4, ada4)


def _ffn_kernel(h2_ref, w1_ref, w2_ref, x1_ref, gate2_ref, o_ref, acc_ref):
    j = pl.program_id(2)

    @pl.when(j == 0)
    def _():
        acc_ref[...] = jnp.zeros_like(acc_ref)

    hid = jnp.maximum(jnp.dot(h2_ref[...], w1_ref[...], preferred_element_type=F32), 0.0)
    acc_ref[...] += jnp.dot((hid * hid).astype(BF16), w2_ref[...], preferred_element_type=F32)

    @pl.when(j == pl.num_programs(2) - 1)
    def _():
        o_ref[...] = x1_ref[...] + gate2_ref[...] * acc_ref[...]


def _ffn(h2, x1, w1, w2, ada4):
    bsz, seq, d = x1.shape
    dff = w1.shape[-1]
    tm, tf = 512, 1024
    rows = lambda: pl.BlockSpec((None, tm, d), lambda b, i, j: (b, i, 0))
    return pl.pallas_call(
        _ffn_kernel,
        grid=(bsz, seq // tm, dff // tf),
        in_specs=[
            rows(),
            pl.BlockSpec((d, tf), lambda b, i, j: (0, j)),
            pl.BlockSpec((tf, d), lambda b, i, j: (j, 0)),
            rows(),
            pl.BlockSpec((None, None, 1, d), lambda b, i, j: (b, 5, 0, 0)),
        ],
        out_specs=rows(),
        out_shape=jax.ShapeDtypeStruct((bsz, seq, d), F32),
        scratch_shapes=[pltpu.VMEM((tm, d), F32)],
        compiler_params=_params(3),
        name="ffn",
    )(h2, w1, w2, x1, ada4)


def kernel(x, c, w_ada, b_ada, norm1_g, norm2_g, w_in, hg_lb_logits, hg_out_norm_g, q_norm_g, k_norm_g,
           attn_sinks, rel_bias_table, w_branch_hg, w_branch_attn, w_out, w_ff1, w_ff2):
    assert w_ada.shape[0] == 1, "single-layer block"
    bsz = x.shape[0]
    ada = _ada(c, w_ada, b_ada)
    ada4 = ada.reshape(bsz, 6, 1, D_MODEL)
    proj, (w_bh, w_ba, w_o, w1, w2) = _inproj(
        x, norm1_g, ada4, w_in[0].astype(BF16), (w_branch_hg, w_branch_attn, w_out, w_ff1, w_ff2))
    o_hg = _hgrn(proj, hg_lb_logits, hg_out_norm_g)
    bias = _attn_bias(rel_bias_table)
    o_at = _attn(proj, bias, attn_sinks[0], q_norm_g, k_norm_g)
    x1, h2 = _merge(x, o_hg, o_at, proj, w_bh, w_ba, w_o, ada4, norm2_g)
    return _ffn(h2, x1, w1, w2, ada4)
```

```python
import functools
import math

import jax
import jax.numpy as jnp
from jax import lax
from jax.experimental import pallas as pl
from jax.experimental.pallas import tpu as pltpu

F32 = jnp.float32
BF16 = jnp.bfloat16

D_MODEL = 2048
HG_HEADS = 8
HG_DK = 128
HG_DV = 128
HG_WIDTH = HG_HEADS * HG_DK
HG_CHUNK = 64
AT_HEADS = 16
AT_KV_HEADS = 4
AT_HEAD_DIM = 64
AT_GROUP = AT_HEADS // AT_KV_HEADS
AT_WIDTH = AT_HEADS * AT_HEAD_DIM
KV_WIDTH = AT_KV_HEADS * AT_HEAD_DIM
WINDOW = 128
BLOCK = 128
N_BUCKETS = 32
MAX_EXACT = N_BUCKETS // 2
MAX_DISTANCE = 128
D_FF = 4 * D_MODEL
EPS = 1e-6
NEG_INF = -1e30

HG_OFF = 0
AT_OFF = 4 * HG_WIDTH
GATE_OFF = AT_OFF + AT_WIDTH + 2 * KV_WIDTH
IN_WIDTH = GATE_OFF + 2 * D_MODEL
MXU_COLS = 256

V7X_VMEM_LIMIT_BYTES = 60 * 1024 * 1024
BF16_SUBLANES = 16


def _params(n_axes):
    return pltpu.CompilerParams(
        dimension_semantics=("arbitrary",) * n_axes,
        vmem_limit_bytes=V7X_VMEM_LIMIT_BYTES,
    )


def _rms_modulate(x, g, scale, shift):
    r = lax.rsqrt(jnp.mean(x * x, axis=-1, keepdims=True) + EPS)
    return (x * r * g) * (1.0 + scale) + shift


def _ada_kernel(c_ref, w_ref, b_ref, o_ref):
    c = c_ref[...]
    c_act = c * jax.nn.sigmoid(c)
    o_ref[...] = jnp.dot(c_act, w_ref[...], preferred_element_type=F32,
                         precision=lax.Precision.HIGHEST) + b_ref[...]


def _ada(c, w_ada, b_ada):
    bsz, d = c.shape
    n = w_ada.shape[-1]
    tn = 1024
    return pl.pallas_call(
        _ada_kernel,
        grid=(n // tn,),
        in_specs=[
            pl.BlockSpec((bsz, d), lambda j: (0, 0)),
            pl.BlockSpec((None, d, tn), lambda j: (0, 0, j)),
            pl.BlockSpec((1, tn), lambda j: (0, j)),
        ],
        out_specs=pl.BlockSpec((bsz, tn), lambda j: (0, j)),
        out_shape=jax.ShapeDtypeStruct((bsz, n), F32),
        compiler_params=_params(1),
        name="ada",
    )(c, w_ada, b_ada)


def _inproj_kernel(n_late, x_ref, g_ref, shift_ref, scale_ref, w_ref, *refs):
    late_in, (o_ref, *late_out) = refs[:n_late], refs[n_late:]
    h = _rms_modulate(x_ref[...], g_ref[...], scale_ref[...], shift_ref[...])
    o_ref[...] = jnp.dot(h.astype(BF16), w_ref[...], preferred_element_type=F32).astype(o_ref.dtype)
    for src, dst in zip(late_in, late_out):
        dst[...] = src[...].astype(dst.dtype)


def _inproj(x, norm_g, ada4, w_bf16, late_weights):
    bsz, seq, d = x.shape
    n = w_bf16.shape[-1]
    tm, n_col_tiles = 512, 2
    tn = n // n_col_tiles
    assert tn % MXU_COLS == 0
    tiles_per_seq = seq // tm
    n_row_tiles = bsz * tiles_per_seq
    n_steps = n_col_tiles * n_row_tiles
    row = lambda j, i: (i // tiles_per_seq, i % tiles_per_seq)
    late_in_specs, late_out_specs, late_shapes = [], [], []
    for w in late_weights:
        _, rows, cols = w.shape
        slab = max(rows // n_steps, BF16_SUBLANES)
        assert rows % slab == 0
        idx = lambda j, i, last=rows // slab - 1: (jnp.minimum(j * n_row_tiles + i, last), 0)
        late_in_specs.append(pl.BlockSpec((None, slab, cols), lambda j, i, idx=idx: (0, *idx(j, i))))
        late_out_specs.append(pl.BlockSpec((slab, cols), idx))
        late_shapes.append(jax.ShapeDtypeStruct((rows, cols), BF16))
    outs = pl.pallas_call(
        functools.partial(_inproj_kernel, len(late_weights)),
        grid=(n_col_tiles, n_row_tiles),
        in_specs=[
            pl.BlockSpec((None, tm, d), lambda j, i: (*row(j, i), 0)),
            pl.BlockSpec((1, d), lambda j, i: (0, 0)),
            pl.BlockSpec((None, None, 1, d), lambda j, i: (i // tiles_per_seq, 0, 0, 0)),
            pl.BlockSpec((None, None, 1, d), lambda j, i: (i // tiles_per_seq, 1, 0, 0)),
            pl.BlockSpec((d, tn), lambda j, i: (0, j)),
            *late_in_specs,
        ],
        out_specs=[pl.BlockSpec((None, tm, tn), lambda j, i: (*row(j, i), j)), *late_out_specs],
        out_shape=[jax.ShapeDtypeStruct((bsz, seq, n), BF16), *late_shapes],
        compiler_params=_params(2),
        name="inproj",
    )(x, norm_g, ada4, ada4, w_bf16, *late_weights)
    return outs[0], outs[1:]


HG_GROUP = 4
HG_ROWS = HG_GROUP * HG_CHUNK
GATE_CHUNK = 256


def _mixer_kernel(x_ref, g_ref, shift_ref, scale_ref, wg_ref, q_ref, f_ref, i_ref, hg_ref, lbl_ref, gn_ref,
                  gates_ref, o_ref, h_ref, state_ref):
    c, n = HG_CHUNK, HG_ROWS
    heads = range(HG_HEADS)
    lanes = [slice(u * HG_DK, (u + 1) * HG_DK) for u in heads]

    @pl.when(pl.program_id(1) == 0)
    def _():
        state_ref[...] = jnp.zeros_like(state_ref)

    h_ref[...] = _rms_modulate(x_ref[...], g_ref[...], scale_ref[...], shift_ref[...]).astype(BF16)
    pending = list(range(wg_ref.shape[1] // GATE_CHUNK))

    def gate_chunks(count):
        for _ in range(min(count, len(pending))):
            k = pending.pop(0)
            cols = slice(k * GATE_CHUNK, (k + 1) * GATE_CHUNK)
            gates_ref[:, cols] = jnp.dot(h_ref[...], wg_ref[:, cols],
                                         preferred_element_type=F32).astype(gates_ref.dtype)

    lbl = lbl_ref[...]
    e = jnp.exp(lbl - jnp.max(lbl, axis=0, keepdims=True))
    lb = e[0:1] / jnp.sum(e, axis=0, keepdims=True)
    gn = gn_ref[...]

    t = lax.broadcasted_iota(jnp.int32, (n, n), 0)
    s = lax.broadcasted_iota(jnp.int32, (n, n), 1)
    tri = ((t // c) == (s // c)) & (s <= t)
    r8 = lax.broadcasted_iota(jnp.int32, (8, n), 0)
    s8 = lax.broadcasted_iota(jnp.int32, (8, n), 1)
    in_chunk = (s8 // c) == (r8 % HG_GROUP)
    stat_rows = in_chunk & ((r8 >= HG_GROUP) | ((s8 % c) < c // 2))
    cum_ops = jnp.concatenate([tri.astype(F32), stat_rows.astype(F32)], axis=0).astype(BF16)
    per_chunk = lambda rows: jnp.concatenate(
        [jnp.broadcast_to(rows[cc:cc + 1], (c, HG_DK)) for cc in range(HG_GROUP)], axis=0)
    col_chunk = lax.broadcasted_iota(jnp.int32, (HG_DV, n), 1) // c
    col_masks = [(col_chunk == cc).astype(BF16) for cc in range(HG_GROUP)]
    zeros_rows = lambda rows: jnp.zeros((rows, HG_DK), BF16)

    gate_chunks(3)
    ks, qss, hilo = [], [], []
    for u in heads:
        hf = f_ref[:, lanes[u]].astype(F32)
        hq = q_ref[:, lanes[u]].astype(F32)
        f = lb[:, lanes[u]] + (1.0 - lb[:, lanes[u]]) * jax.nn.sigmoid(hf)
        log_f = jnp.log(f)
        ks.append(1.0 - f)
        qss.append(hq * jax.nn.sigmoid(hq))
        hi = log_f.astype(BF16)
        lo = (log_f - hi.astype(F32)).astype(BF16)
        hilo.append(jnp.concatenate([hi, lo], axis=1))
    rs = [jnp.dot(cum_ops, hilo[u], preferred_element_type=F32) for u in heads]
    gate_chunks(3)
    a_s, bms, kks, qes, decs = [], [], [], [], []
    for u in heads:
        r = rs[u][:, :HG_DK] + rs[u][:, HG_DK:]
        b, b_last = r[0:n], r[n + HG_GROUP:n + 2 * HG_GROUP]
        b_rel = b - per_chunk(r[n:n + HG_GROUP])
        a_s.append((qss[u] * jnp.exp(b_rel)).astype(BF16))
        bms.append((ks[u] * jnp.exp(-b_rel)).astype(BF16))
        kks.append((ks[u] * jnp.exp(per_chunk(b_last) - b)).astype(BF16))
        qes.append((qss[u] * jnp.exp(b)).astype(BF16))
        decs.append(jnp.exp(b_last))
    scores = [lax.dot_general(a_s[u], bms[u], (((1,), (1,)), ((), ())), preferred_element_type=F32)
              for u in heads]
    gate_chunks(3)
    vs = [i_ref[:, lanes[u]] for u in heads]
    os = [jnp.dot(jnp.where(tri, scores[u], 0.0).astype(BF16), vs[u], preferred_element_type=F32)
          for u in heads]
    upd_ts = []
    for u in heads:
        v_t = vs[u].astype(F32).T.astype(BF16)
        v_t_blocks = jnp.concatenate([v_t * col_masks[cc] for cc in range(HG_GROUP)], axis=0)
        upd_ts.append(jnp.dot(v_t_blocks, kks[u], preferred_element_type=F32))
    gate_chunks(3)
    for u in heads:
        s_t = state_ref[u]
        states = []
        for cc in range(HG_GROUP):
            states.append(s_t.astype(BF16))
            s_t = s_t * decs[u][cc:cc + 1] + upd_ts[u][cc * HG_DV:(cc + 1) * HG_DV]
        state_ref[u] = s_t
        qe = qes[u]
        qe_blocks = jnp.concatenate(
            [jnp.concatenate([p for p in (zeros_rows(cc * c), qe[cc * c:(cc + 1) * c],
                                          zeros_rows(n - (cc + 1) * c)) if p.shape[0]], axis=0)
             for cc in range(HG_GROUP)], axis=1)
        os[u] = os[u] + lax.dot_general(qe_blocks, jnp.concatenate(states, axis=1),
                                        (((1,), (1,)), ((), ())), preferred_element_type=F32)
    gate_chunks(len(pending))
    for u in heads:
        o = os[u]
        hg = hg_ref[:, lanes[u]].astype(F32)
        y = o * lax.rsqrt(jnp.mean(o * o, axis=-1, keepdims=True) + EPS) * gn
        o_ref[:, lanes[u]] = (y * (hg * jax.nn.sigmoid(hg))).astype(o_ref.dtype)


def _mixer(x, norm_g, ada4, w_gate_bf16, proj, lb_logits, out_norm_g):
    bsz, seq, d = x.shape
    n_gate = w_gate_bf16.shape[-1]
    tm = HG_ROWS
    rows = lambda w, colblk: pl.BlockSpec((None, tm, w), lambda b, i: (b, i, colblk))
    ada_row = lambda k: pl.BlockSpec((None, None, 1, d), lambda b, i: (b, k, 0, 0))
    hg_base = HG_OFF // HG_WIDTH
    return pl.pallas_call(
        _mixer_kernel,
        grid=(bsz, seq // tm),
        in_specs=[
            rows(d, 0), pl.BlockSpec((1, d), lambda b, i: (0, 0)), ada_row(0), ada_row(1),
            pl.BlockSpec(w_gate_bf16.shape, lambda b, i: (0, 0), pipeline_mode=pl.Buffered(1)),
            rows(HG_WIDTH, hg_base), rows(HG_WIDTH, hg_base + 1), rows(HG_WIDTH, hg_base + 2),
            rows(HG_WIDTH, hg_base + 3),
            pl.BlockSpec(lb_logits.shape, lambda b, i: (0, 0)),
            pl.BlockSpec((1, HG_DV), lambda b, i: (0, 0)),
        ],
        out_specs=[rows(n_gate, 0), rows(HG_WIDTH, 0)],
        out_shape=[jax.ShapeDtypeStruct((bsz, seq, n_gate), BF16),
                   jax.ShapeDtypeStruct((bsz, seq, HG_WIDTH), BF16)],
        scratch_shapes=[pltpu.VMEM((tm, d), BF16), pltpu.VMEM((HG_HEADS, HG_DV, HG_DK), F32)],
        compiler_params=_params(2),
        name="mixer",
    )(x, norm_g, ada4, ada4, w_gate_bf16, proj, proj, proj, proj, lb_logits, out_norm_g)


def _t5_causal_bucket(n):
    nf = jnp.maximum(n, 1).astype(F32)
    large = MAX_EXACT + (jnp.log(nf / MAX_EXACT) / math.log(MAX_DISTANCE / MAX_EXACT)
                         * (N_BUCKETS - MAX_EXACT)).astype(jnp.int32)
    large = jnp.minimum(large, N_BUCKETS - 1)
    return jnp.where(n < MAX_EXACT, n, large)


def _bias_kernel(table_ref, bucket_t_ref, o_ref):
    bucket_t = bucket_t_ref[...]
    j = lax.broadcasted_iota(jnp.int32, bucket_t.shape, 0)
    i = lax.broadcasted_iota(jnp.int32, bucket_t.shape, 1)
    dist = i - j + BLOCK
    band = (dist >= 0) & (dist < WINDOW)
    for h in range(AT_HEADS):
        acc = jnp.zeros(bucket_t.shape, F32)
        for r in range(N_BUCKETS):
            acc = jnp.where(bucket_t == r, table_ref[r, h], acc)
        g = h % AT_GROUP
        o_ref[h // AT_GROUP, :, g * BLOCK:(g + 1) * BLOCK] = jnp.where(band, acc, NEG_INF)


def _attn_bias(rel_bias_table):
    i = jnp.arange(BLOCK, dtype=jnp.int32)[None, :]
    j = jnp.arange(2 * BLOCK, dtype=jnp.int32)[:, None]
    bucket_t = _t5_causal_bucket(jnp.maximum(i - j + BLOCK, 0))
    shape = (AT_KV_HEADS, 2 * BLOCK, AT_GROUP * BLOCK)
    return pl.pallas_call(
        _bias_kernel,
        in_specs=[
            pl.BlockSpec(memory_space=pltpu.SMEM),
            pl.BlockSpec((2 * BLOCK, BLOCK), lambda: (0, 0)),
        ],
        out_specs=pl.BlockSpec(shape, lambda: (0, 0, 0)),
        out_shape=jax.ShapeDtypeStruct(shape, F32),
        name="attn_bias",
    )(rel_bias_table, bucket_t)


ATT_TQ = 512


def _head_mean_square(t, pool):
    return jnp.dot((t * t).astype(BF16), pool, preferred_element_type=F32)


def _attn_kernel(q_ref, k_ref, v_ref, kh_ref, vh_ref, bias_ref, sink_ref, qg_ref, kg_ref, o_ref):
    step = pl.program_id(1)
    d = AT_HEAD_DIM
    grp = AT_GROUP * d
    sub = lax.broadcasted_iota(jnp.int32, (grp, grp), 0)
    lane = lax.broadcasted_iota(jnp.int32, (grp, grp), 1)
    pool = jnp.where(sub // d == lane // d, 1.0 / d, 0.0).astype(BF16)

    k_all = jnp.concatenate([kh_ref[...], k_ref[...]], axis=0).astype(F32)
    k_n = (k_all * lax.rsqrt(_head_mean_square(k_all, pool) + EPS) * kg_ref[...]).astype(BF16)
    v_t = jnp.concatenate([vh_ref[...], v_ref[...]], axis=0).astype(F32).T.astype(BF16)

    lane_head = lax.broadcasted_iota(jnp.int32, (BLOCK, grp), 1) // d
    head_masks = [(lane_head == g).astype(BF16) for g in range(AT_GROUP)]
    pre_seq = jnp.where(step == 0, NEG_INF, 0.0)

    scale = d ** -0.5
    k_reps, q_ns = [], []
    for kv in range(AT_KV_HEADS):
        spread = ((sub // d == kv) & (sub % d == lane % d)).astype(BF16)
        k_reps.append(jnp.dot(k_n, spread, preferred_element_type=F32).astype(BF16))
        q_g = q_ref[:, kv * grp:(kv + 1) * grp].astype(F32)
        q_ns.append((q_g * lax.rsqrt(_head_mean_square(q_g, pool) + EPS) * (qg_ref[...] * scale)).astype(BF16))

    def logits_t(kv, nb):
        rows = slice(nb * BLOCK, (nb + 1) * BLOCK)
        keys = slice(nb * BLOCK, (nb + 2) * BLOCK)
        q4 = jnp.concatenate([q_ns[kv][rows] * head_masks[g] for g in range(AT_GROUP)], axis=0)
        lg = lax.dot_general(k_reps[kv][keys], q4, (((1,), (1,)), ((), ())),
                             preferred_element_type=F32) + bias_ref[kv]
        if nb == 0:
            lg = jnp.concatenate([lg[:BLOCK] + pre_seq, lg[BLOCK:]], axis=0)
        return lg

    units = [(kv, nb) for kv in range(AT_KV_HEADS) for nb in range(ATT_TQ // BLOCK)]
    lg_next = logits_t(*units[0])
    for idx, (kv, nb) in enumerate(units):
        lg = lg_next
        if idx + 1 < len(units):
            lg_next = logits_t(*units[idx + 1])
        rows = slice(nb * BLOCK, (nb + 1) * BLOCK)
        keys = slice(nb * BLOCK, (nb + 2) * BLOCK)
        sink = jnp.concatenate(
            [jnp.full((1, BLOCK), sink_ref[kv * AT_GROUP + g], F32) for g in range(AT_GROUP)], axis=1)
        m = jnp.maximum(jnp.max(lg, axis=0, keepdims=True), sink)
        e = jnp.exp(lg - m)
        den = jnp.sum(e, axis=0, keepdims=True) + jnp.exp(sink - m)
        o_t = jnp.dot(v_t[kv * d:(kv + 1) * d, keys], e.astype(BF16),
                      preferred_element_type=F32) * (1.0 / den)
        o = jnp.concatenate([o_t[:, g * BLOCK:(g + 1) * BLOCK] for g in range(AT_GROUP)], axis=0).T
        o_ref[rows, kv * grp:(kv + 1) * grp] = o.astype(o_ref.dtype)


def _attn(proj, bias, sinks, q_norm_g, k_norm_g):
    bsz, seq, _ = proj.shape
    tq = ATT_TQ
    blocks_per_step = tq // BLOCK
    q_col = AT_OFF // AT_WIDTH
    k_col = (AT_OFF + AT_WIDTH) // KV_WIDTH
    v_col = k_col + 1
    halo = lambda colblk: pl.BlockSpec(
        (None, BLOCK, KV_WIDTH), lambda b, i: (b, jnp.maximum(i * blocks_per_step - 1, 0), colblk))
    qg = jnp.tile(q_norm_g, (1, AT_GROUP))
    kg = jnp.tile(k_norm_g, (1, AT_KV_HEADS))
    return pl.pallas_call(
        _attn_kernel,
        grid=(bsz, seq // tq),
        in_specs=[
            pl.BlockSpec((None, tq, AT_WIDTH), lambda b, i: (b, i, q_col)),
            pl.BlockSpec((None, tq, KV_WIDTH), lambda b, i: (b, i, k_col)),
            pl.BlockSpec((None, tq, KV_WIDTH), lambda b, i: (b, i, v_col)),
            halo(k_col), halo(v_col),
            pl.BlockSpec((AT_KV_HEADS, 2 * BLOCK, AT_GROUP * BLOCK), lambda b, i: (0, 0, 0)),
            pl.BlockSpec(memory_space=pltpu.SMEM),
            pl.BlockSpec((1, AT_GROUP * AT_HEAD_DIM), lambda b, i: (0, 0)),
            pl.BlockSpec((1, KV_WIDTH), lambda b, i: (0, 0)),
        ],
        out_specs=pl.BlockSpec((None, tq, AT_WIDTH), lambda b, i: (b, i, 0)),
        out_shape=jax.ShapeDtypeStruct((bsz, seq, AT_WIDTH), BF16),
        compiler_params=_params(2),
        name="attn",
    )(proj, proj, proj, proj, proj, bias, sinks, qg, kg)


MERGE_SPLIT = 2


def _merge_kernel(x_ref, ohg_ref, oat_ref, ghg_ref, gat_ref, wbh_ref, wba_ref, wo_ref,
                  gate1_ref, n2g_ref, shift2_ref, scale2_ref, x1_ref, h2_ref):
    part = x_ref.shape[0] // MERGE_SPLIT
    parts = [slice(p * part, (p + 1) * part) for p in range(MERGE_SPLIT)]
    m_hg = [jnp.dot(ohg_ref[rows, :], wbh_ref[...], preferred_element_type=F32) for rows in parts]
    m_at = [jnp.dot(oat_ref[rows, :], wba_ref[...], preferred_element_type=F32) for rows in parts]
    ys = []
    for p, rows in enumerate(parts):
        merged = (jax.nn.sigmoid(ghg_ref[rows, :].astype(F32)) * m_hg[p]
                  + jax.nn.sigmoid(gat_ref[rows, :].astype(F32)) * m_at[p])
        ys.append(jnp.dot(merged.astype(BF16), wo_ref[...], preferred_element_type=F32))
    for p, rows in enumerate(parts):
        x1 = x_ref[rows, :] + gate1_ref[...] * ys[p]
        x1_ref[rows, :] = x1
        h2_ref[rows, :] = _rms_modulate(x1, n2g_ref[...], scale2_ref[...], shift2_ref[...]).astype(h2_ref.dtype)


def _merge(x, o_hg, o_at, gates, w_bh, w_ba, w_o, ada4, norm2_g):
    bsz, seq, d = x.shape
    tm = 512
    resident = lambda shape: pl.BlockSpec(shape, lambda b, i: (0, 0), pipeline_mode=pl.Buffered(1))
    ada_row = lambda k: pl.BlockSpec((None, None, 1, d), lambda b, i: (b, k, 0, 0))
    rows = lambda w, colblk: pl.BlockSpec((None, tm, w), lambda b, i: (b, i, colblk))
    return pl.pallas_call(
        _merge_kernel,
        grid=(bsz, seq // tm),
        in_specs=[
            rows(d, 0), rows(HG_WIDTH, 0), rows(AT_WIDTH, 0), rows(d, 0), rows(d, 1),
            resident(w_bh.shape), resident(w_ba.shape), resident(w_o.shape),
            ada_row(2), pl.BlockSpec((1, d), lambda b, i: (0, 0)), ada_row(3), ada_row(4),
        ],
        out_specs=[rows(d, 0), rows(d, 0)],
        out_shape=[jax.ShapeDtypeStruct((bsz, seq, d), F32), jax.ShapeDtypeStruct((bsz, seq, d), BF16)],
        compiler_params=_params(2),
        name="merge",
    )(x, o_hg, o_at, gates, gates, w_bh, w_ba, w_o, ada4, norm2_g, ada4, ada4)


def _ffn_kernel(h2_ref, w1_ref, w2_ref, x1_ref, gate2_ref, o_ref, acc_ref):
    j = pl.program_id(2)

    @pl.when(j == 0)
    def _():
        acc_ref[...] = jnp.zeros_like(acc_ref)

    hid = jnp.maximum(jnp.dot(h2_ref[...], w1_ref[...], preferred_element_type=F32), 0.0)
    acc_ref[...] += jnp.dot((hid * hid).astype(BF16), w2_ref[...], preferred_element_type=F32)

    @pl.when(j == pl.num_programs(2) - 1)
    def _():
        o_ref[...] = x1_ref[...] + gate2_ref[...] * acc_ref[...]


def _ffn(h2, x1, w1, w2, ada4):
    bsz, seq, d = x1.shape
    dff = w1.shape[-1]
    tm, tf = 512, 1024
    rows = lambda: pl.BlockSpec((None, tm, d), lambda b, i, j: (b, i, 0))
    return pl.pallas_call(
        _ffn_kernel,
        grid=(bsz, seq // tm, dff // tf),
        in_specs=[
            rows(),
            pl.BlockSpec((d, tf), lambda b, i, j: (0, j)),
            pl.BlockSpec((tf, d), lambda b, i, j: (j, 0)),
            rows(),
            pl.BlockSpec((None, None, 1, d), lambda b, i, j: (b, 5, 0, 0)),
        ],
        out_specs=rows(),
        out_shape=jax.ShapeDtypeStruct((bsz, seq, d), F32),
        scratch_shapes=[pltpu.VMEM((tm, d), F32)],
        compiler_params=_params(3),
        name="ffn",
    )(h2, w1, w2, x1, ada4)


def kernel(x, c, w_ada, b_ada, norm1_g, norm2_g, w_in, hg_lb_logits, hg_out_norm_g, q_norm_g, k_norm_g,
           attn_sinks, rel_bias_table, w_branch_hg, w_branch_attn, w_out, w_ff1, w_ff2):
    assert w_ada.shape[0] == 1, "single-layer block"
    bsz = x.shape[0]
    ada = _ada(c, w_ada, b_ada)
    ada4 = ada.reshape(bsz, 6, 1, D_MODEL)
    proj, (w_bh, w_ba, w_o, w1, w2) = _inproj(
        x, norm1_g, ada4, w_in[0, :, :GATE_OFF].astype(BF16), (w_branch_hg, w_branch_attn, w_out, w_ff1, w_ff2))
    gates, o_hg = _mixer(x, norm1_g, ada4, w_in[0, :, GATE_OFF:].astype(BF16), proj, hg_lb_logits, hg_out_norm_g)
    bias = _attn_bias(rel_bias_table)
    o_at = _attn(proj, bias, attn_sinks[0], q_norm_g, k_norm_g)
    x1, h2 = _merge(x, o_hg, o_at, gates, w_bh, w_ba, w_o, ada4, norm2_g)
    return _ffn(h2, x1, w1, w2, ada4)
```

```python
import functools
import math

import jax
import jax.numpy as jnp
from jax import lax
from jax.experimental import pallas as pl
from jax.experimental.pallas import tpu as pltpu

F32 = jnp.float32
BF16 = jnp.bfloat16

D_MODEL = 2048
HG_HEADS = 8
HG_DK = 128
HG_DV = 128
HG_WIDTH = HG_HEADS * HG_DK
HG_CHUNK = 64
AT_HEADS = 16
AT_KV_HEADS = 4
AT_HEAD_DIM = 64
AT_GROUP = AT_HEADS // AT_KV_HEADS
AT_WIDTH = AT_HEADS * AT_HEAD_DIM
KV_WIDTH = AT_KV_HEADS * AT_HEAD_DIM
WINDOW = 128
BLOCK = 128
N_BUCKETS = 32
MAX_EXACT = N_BUCKETS // 2
MAX_DISTANCE = 128
D_FF = 4 * D_MODEL
EPS = 1e-6
NEG_INF = -1e30

HG_OFF = 0
AT_OFF = 4 * HG_WIDTH
GATE_OFF = AT_OFF + AT_WIDTH + 2 * KV_WIDTH
IN_WIDTH = GATE_OFF + 2 * D_MODEL
MXU_COLS = 256

V7X_VMEM_LIMIT_BYTES = 60 * 1024 * 1024
BF16_SUBLANES = 16


def _params(n_axes):
    return pltpu.CompilerParams(
        dimension_semantics=("arbitrary",) * n_axes,
        vmem_limit_bytes=V7X_VMEM_LIMIT_BYTES,
    )


def _rms_modulate(x, g, scale, shift):
    r = lax.rsqrt(jnp.mean(x * x, axis=-1, keepdims=True) + EPS)
    return (x * r * g) * (1.0 + scale) + shift


def _ada_kernel(c_ref, w_ref, b_ref, o_ref):
    c = c_ref[...]
    c_act = c * jax.nn.sigmoid(c)
    o_ref[...] = jnp.dot(c_act, w_ref[...], preferred_element_type=F32,
                         precision=lax.Precision.HIGHEST) + b_ref[...]


def _ada(c, w_ada, b_ada):
    bsz, d = c.shape
    n = w_ada.shape[-1]
    tn = 1024
    return pl.pallas_call(
        _ada_kernel,
        grid=(n // tn,),
        in_specs=[
            pl.BlockSpec((bsz, d), lambda j: (0, 0)),
            pl.BlockSpec((None, d, tn), lambda j: (0, 0, j)),
            pl.BlockSpec((1, tn), lambda j: (0, j)),
        ],
        out_specs=pl.BlockSpec((bsz, tn), lambda j: (0, j)),
        out_shape=jax.ShapeDtypeStruct((bsz, n), F32),
        compiler_params=_params(1),
        name="ada",
    )(c, w_ada, b_ada)


def _inproj_kernel(n_late, x_ref, g_ref, shift_ref, scale_ref, w_ref, *refs):
    late_in, (o_ref, *late_out) = refs[:n_late], refs[n_late:]
    h = _rms_modulate(x_ref[...], g_ref[...], scale_ref[...], shift_ref[...])
    o_ref[...] = jnp.dot(h.astype(BF16), w_ref[...], preferred_element_type=F32).astype(o_ref.dtype)
    for src, dst in zip(late_in, late_out):
        dst[...] = src[...].astype(dst.dtype)


def _inproj(x, norm_g, ada4, w_bf16, n, late_weights):
    bsz, seq, d = x.shape
    tm, n_col_tiles = 512, 2
    tn = n // n_col_tiles
    assert tn % MXU_COLS == 0
    tiles_per_seq = seq // tm
    n_row_tiles = bsz * tiles_per_seq
    n_steps = n_col_tiles * n_row_tiles
    row = lambda j, i: (i // tiles_per_seq, i % tiles_per_seq)
    late_in_specs, late_out_specs, late_shapes = [], [], []
    for w in late_weights:
        _, rows, cols = w.shape
        slab = max(rows // n_steps, BF16_SUBLANES)
        assert rows % slab == 0
        idx = lambda j, i, last=rows // slab - 1: (jnp.minimum(j * n_row_tiles + i, last), 0)
        late_in_specs.append(pl.BlockSpec((None, slab, cols), lambda j, i, idx=idx: (0, *idx(j, i))))
        late_out_specs.append(pl.BlockSpec((slab, cols), idx))
        late_shapes.append(jax.ShapeDtypeStruct((rows, cols), BF16))
    outs = pl.pallas_call(
        functools.partial(_inproj_kernel, len(late_weights)),
        grid=(n_col_tiles, n_row_tiles),
        in_specs=[
            pl.BlockSpec((None, tm, d), lambda j, i: (*row(j, i), 0)),
            pl.BlockSpec((1, d), lambda j, i: (0, 0)),
            pl.BlockSpec((None, None, 1, d), lambda j, i: (i // tiles_per_seq, 0, 0, 0)),
            pl.BlockSpec((None, None, 1, d), lambda j, i: (i // tiles_per_seq, 1, 0, 0)),
            pl.BlockSpec((d, tn), lambda j, i: (0, j)),
            *late_in_specs,
        ],
        out_specs=[pl.BlockSpec((None, tm, tn), lambda j, i: (*row(j, i), j)), *late_out_specs],
        out_shape=[jax.ShapeDtypeStruct((bsz, seq, n), BF16), *late_shapes],
        compiler_params=_params(2),
        name="inproj",
    )(x, norm_g, ada4, ada4, w_bf16, *late_weights)
    return outs[0], outs[1:]


HG_GROUP = 4
HG_ROWS = HG_GROUP * HG_CHUNK
GATE_CHUNK = 256
GATE_W_BLOCK = 512


def _mixer_kernel(n_wg, x_ref, g_ref, shift_ref, scale_ref, *refs):
    wg_refs = refs[:n_wg]
    (q_ref, f_ref, i_ref, hg_ref, lbl_ref, gn_ref, *attn_in,
     gates_ref, o_ref, oat_ref, h_ref, state_ref) = refs[n_wg:]
    c, n = HG_CHUNK, HG_ROWS
    heads = range(HG_HEADS)
    lanes = [slice(u * HG_DK, (u + 1) * HG_DK) for u in heads]

    @pl.when(pl.program_id(1) == 0)
    def _():
        state_ref[...] = jnp.zeros_like(state_ref)

    h_ref[...] = _rms_modulate(x_ref[...], g_ref[...], scale_ref[...], shift_ref[...]).astype(BF16)
    wg_cols = wg_refs[0].shape[1]
    pending = list(range(n_wg * wg_cols // GATE_CHUNK))

    def gate_chunks(count):
        for _ in range(min(count, len(pending))):
            k = pending.pop(0)
            w_ref, off = wg_refs[k * GATE_CHUNK // wg_cols], k * GATE_CHUNK % wg_cols
            gates_ref[:, k * GATE_CHUNK:(k + 1) * GATE_CHUNK] = jnp.dot(
                h_ref[...], w_ref[:, off:off + GATE_CHUNK], preferred_element_type=F32).astype(gates_ref.dtype)

    attention = _attention_steps(*attn_in, oat_ref, pl.program_id(1) == 0)
    attend = lambda: next(attention, None)

    def interleave(count):
        for _ in range(count):
            attend()
            gate_chunks(1)

    lbl = lbl_ref[...]
    e = jnp.exp(lbl - jnp.max(lbl, axis=0, keepdims=True))
    lb = e[0:1] / jnp.sum(e, axis=0, keepdims=True)
    gn = gn_ref[...]

    t = lax.broadcasted_iota(jnp.int32, (n, n), 0)
    s = lax.broadcasted_iota(jnp.int32, (n, n), 1)
    tri = ((t // c) == (s // c)) & (s <= t)
    r8 = lax.broadcasted_iota(jnp.int32, (8, n), 0)
    s8 = lax.broadcasted_iota(jnp.int32, (8, n), 1)
    in_chunk = (s8 // c) == (r8 % HG_GROUP)
    stat_rows = in_chunk & ((r8 >= HG_GROUP) | ((s8 % c) < c // 2))
    cum_ops = jnp.concatenate([tri.astype(F32), stat_rows.astype(F32)], axis=0).astype(BF16)
    per_chunk = lambda rows: jnp.concatenate(
        [jnp.broadcast_to(rows[cc:cc + 1], (c, HG_DK)) for cc in range(HG_GROUP)], axis=0)
    col_chunk = lax.broadcasted_iota(jnp.int32, (HG_DV, n), 1) // c
    col_masks = [(col_chunk == cc).astype(BF16) for cc in range(HG_GROUP)]
    zeros_rows = lambda rows: jnp.zeros((rows, HG_DK), BF16)

    gate_chunks(2)
    attend()
    gate_chunks(1)
    ks, qss, hilo = [], [], []
    for u in heads:
        hf = f_ref[:, lanes[u]].astype(F32)
        hq = q_ref[:, lanes[u]].astype(F32)
        f = lb[:, lanes[u]] + (1.0 - lb[:, lanes[u]]) * jax.nn.sigmoid(hf)
        log_f = jnp.log(f)
        ks.append(1.0 - f)
        qss.append(hq * jax.nn.sigmoid(hq))
        hi = log_f.astype(BF16)
        lo = (log_f - hi.astype(F32)).astype(BF16)
        hilo.append(jnp.concatenate([hi, lo], axis=1))
    rs = [jnp.dot(cum_ops, hilo[u], preferred_element_type=F32) for u in heads]
    interleave(2)
    a_s, bms, kks, qes, decs = [], [], [], [], []
    for u in heads:
        r = rs[u][:, :HG_DK] + rs[u][:, HG_DK:]
        b, b_last = r[0:n], r[n + HG_GROUP:n + 2 * HG_GROUP]
        b_rel = b - per_chunk(r[n:n + HG_GROUP])
        a_s.append((qss[u] * jnp.exp(b_rel)).astype(BF16))
        bms.append((ks[u] * jnp.exp(-b_rel)).astype(BF16))
        kks.append((ks[u] * jnp.exp(per_chunk(b_last) - b)).astype(BF16))
        qes.append((qss[u] * jnp.exp(b)).astype(BF16))
        decs.append(jnp.exp(b_last))
    scores = [lax.dot_general(a_s[u], bms[u], (((1,), (1,)), ((), ())), preferred_element_type=F32)
              for u in heads]
    interleave(2)
    vs = [i_ref[:, lanes[u]] for u in heads]
    os = [jnp.dot(jnp.where(tri, scores[u], 0.0).astype(BF16), vs[u], preferred_element_type=F32)
          for u in heads]
    interleave(2)
    upd_ts = []
    for u in heads:
        v_t = vs[u].astype(F32).T.astype(BF16)
        v_t_blocks = jnp.concatenate([v_t * col_masks[cc] for cc in range(HG_GROUP)], axis=0)
        upd_ts.append(jnp.dot(v_t_blocks, kks[u], preferred_element_type=F32))
    interleave(2)
    for u in heads:
        s_t = state_ref[u]
        states = []
        for cc in range(HG_GROUP):
            states.append(s_t.astype(BF16))
            s_t = s_t * decs[u][cc:cc + 1] + upd_ts[u][cc * HG_DV:(cc + 1) * HG_DV]
        state_ref[u] = s_t
        qe = qes[u]
        qe_blocks = jnp.concatenate(
            [jnp.concatenate([p for p in (zeros_rows(cc * c), qe[cc * c:(cc + 1) * c],
                                          zeros_rows(n - (cc + 1) * c)) if p.shape[0]], axis=0)
             for cc in range(HG_GROUP)], axis=1)
        os[u] = os[u] + lax.dot_general(qe_blocks, jnp.concatenate(states, axis=1),
                                        (((1,), (1,)), ((), ())), preferred_element_type=F32)
    interleave(len(pending) - 2)
    for _ in attention:
        pass
    gate_chunks(len(pending))
    for u in heads:
        o = os[u]
        hg = hg_ref[:, lanes[u]].astype(F32)
        y = o * lax.rsqrt(jnp.mean(o * o, axis=-1, keepdims=True) + EPS) * gn
        o_ref[:, lanes[u]] = (y * (hg * jax.nn.sigmoid(hg))).astype(o_ref.dtype)


def _mixer(x, norm_g, ada4, w_in_bf16, proj, lb_logits, out_norm_g, bias, sinks, q_norm_g, k_norm_g):
    bsz, seq, d = x.shape
    n_gate = w_in_bf16.shape[-1] - GATE_OFF
    tm = HG_ROWS
    blocks_per_step = tm // BLOCK
    rows = lambda w, colblk: pl.BlockSpec((None, tm, w), lambda b, i: (b, i, colblk))
    ada_row = lambda k: pl.BlockSpec((None, None, 1, d), lambda b, i: (b, k, 0, 0))
    const = lambda shape: pl.BlockSpec(shape, lambda b, i: (0,) * len(shape))
    n_wg = n_gate // GATE_W_BLOCK
    w_gate = [pl.BlockSpec((d, GATE_W_BLOCK), lambda b, i, k=k: (0, GATE_OFF // GATE_W_BLOCK + k),
                           pipeline_mode=pl.Buffered(1)) for k in range(n_wg)]
    hg_base = HG_OFF // HG_WIDTH
    k_col = (AT_OFF + AT_WIDTH) // KV_WIDTH
    halo = lambda colblk: pl.BlockSpec(
        (None, BLOCK, KV_WIDTH), lambda b, i: (b, jnp.maximum(i * blocks_per_step - 1, 0), colblk))
    qg = jnp.tile(q_norm_g, (1, AT_GROUP))
    kg = jnp.tile(k_norm_g, (1, AT_KV_HEADS))
    return pl.pallas_call(
        functools.partial(_mixer_kernel, n_wg),
        grid=(bsz, seq // tm),
        in_specs=[
            rows(d, 0), const((1, d)), ada_row(0), ada_row(1), *w_gate,
            rows(HG_WIDTH, hg_base), rows(HG_WIDTH, hg_base + 1), rows(HG_WIDTH, hg_base + 2),
            rows(HG_WIDTH, hg_base + 3), const(lb_logits.shape), const((1, HG_DV)),
            rows(AT_WIDTH, AT_OFF // AT_WIDTH), rows(KV_WIDTH, k_col), rows(KV_WIDTH, k_col + 1),
            halo(k_col), halo(k_col + 1),
            pl.BlockSpec(bias.shape, lambda b, i: (0, 0, 0), pipeline_mode=pl.Buffered(1)),
            pl.BlockSpec(memory_space=pltpu.SMEM), const((1, AT_GROUP * AT_HEAD_DIM)), const((1, KV_WIDTH)),
        ],
        out_specs=[rows(n_gate, 0), rows(HG_WIDTH, 0), rows(AT_WIDTH, 0)],
        out_shape=[jax.ShapeDtypeStruct((bsz, seq, n_gate), BF16),
                   jax.ShapeDtypeStruct((bsz, seq, HG_WIDTH), BF16),
                   jax.ShapeDtypeStruct((bsz, seq, AT_WIDTH), BF16)],
        scratch_shapes=[pltpu.VMEM((tm, d), BF16), pltpu.VMEM((HG_HEADS, HG_DV, HG_DK), F32)],
        compiler_params=_params(2),
        name="mixer",
    )(x, norm_g, ada4, ada4, *([w_in_bf16] * n_wg), proj, proj, proj, proj, lb_logits, out_norm_g,
      proj, proj, proj, proj, proj, bias, sinks, qg, kg)


def _t5_causal_bucket(n):
    nf = jnp.maximum(n, 1).astype(F32)
    large = MAX_EXACT + (jnp.log(nf / MAX_EXACT) / math.log(MAX_DISTANCE / MAX_EXACT)
                         * (N_BUCKETS - MAX_EXACT)).astype(jnp.int32)
    large = jnp.minimum(large, N_BUCKETS - 1)
    return jnp.where(n < MAX_EXACT, n, large)


def _bias_kernel(table_ref, bucket_t_ref, o_ref):
    bucket_t = bucket_t_ref[...]
    j = lax.broadcasted_iota(jnp.int32, bucket_t.shape, 0)
    i = lax.broadcasted_iota(jnp.int32, bucket_t.shape, 1)
    dist = i - j + BLOCK
    band = (dist >= 0) & (dist < WINDOW)
    for h in range(AT_HEADS):
        acc = jnp.zeros(bucket_t.shape, F32)
        for r in range(N_BUCKETS):
            acc = jnp.where(bucket_t == r, table_ref[r, h], acc)
        g = h % AT_GROUP
        o_ref[h // AT_GROUP, :, g * BLOCK:(g + 1) * BLOCK] = jnp.where(band, acc, NEG_INF)


def _attn_bias(rel_bias_table):
    i = jnp.arange(BLOCK, dtype=jnp.int32)[None, :]
    j = jnp.arange(2 * BLOCK, dtype=jnp.int32)[:, None]
    bucket_t = _t5_causal_bucket(jnp.maximum(i - j + BLOCK, 0))
    shape = (AT_KV_HEADS, 2 * BLOCK, AT_GROUP * BLOCK)
    return pl.pallas_call(
        _bias_kernel,
        in_specs=[
            pl.BlockSpec(memory_space=pltpu.SMEM),
            pl.BlockSpec((2 * BLOCK, BLOCK), lambda: (0, 0)),
        ],
        out_specs=pl.BlockSpec(shape, lambda: (0, 0, 0)),
        out_shape=jax.ShapeDtypeStruct(shape, F32),
        name="attn_bias",
    )(rel_bias_table, bucket_t)


def _head_mean_square(t, pool):
    return jnp.dot((t * t).astype(BF16), pool, preferred_element_type=F32)


def _attention_steps(q_ref, k_ref, v_ref, kh_ref, vh_ref, bias_ref, sink_ref, qg_ref, kg_ref, o_ref, first_block):
    n_blocks = q_ref.shape[0] // BLOCK
    d = AT_HEAD_DIM
    grp = AT_GROUP * d
    sub = lax.broadcasted_iota(jnp.int32, (grp, grp), 0)
    lane = lax.broadcasted_iota(jnp.int32, (grp, grp), 1)
    pool = jnp.where(sub // d == lane // d, 1.0 / d, 0.0).astype(BF16)

    k_all = jnp.concatenate([kh_ref[...], k_ref[...]], axis=0).astype(F32)
    k_n = (k_all * lax.rsqrt(_head_mean_square(k_all, pool) + EPS) * kg_ref[...]).astype(BF16)
    v_t = jnp.concatenate([vh_ref[...], v_ref[...]], axis=0).astype(F32).T.astype(BF16)

    lane_head = lax.broadcasted_iota(jnp.int32, (BLOCK, grp), 1) // d
    head_masks = [(lane_head == g).astype(BF16) for g in range(AT_GROUP)]
    pre_seq = jnp.where(first_block, NEG_INF, 0.0)

    scale = d ** -0.5
    k_reps, q_ns = [], []
    for kv in range(AT_KV_HEADS):
        spread = ((sub // d == kv) & (sub % d == lane % d)).astype(BF16)
        k_reps.append(jnp.dot(k_n, spread, preferred_element_type=F32).astype(BF16))
        q_g = q_ref[:, kv * grp:(kv + 1) * grp].astype(F32)
        q_ns.append((q_g * lax.rsqrt(_head_mean_square(q_g, pool) + EPS) * (qg_ref[...] * scale)).astype(BF16))
    yield

    def logits_t(kv, nb):
        rows = slice(nb * BLOCK, (nb + 1) * BLOCK)
        keys = slice(nb * BLOCK, (nb + 2) * BLOCK)
        q4 = jnp.concatenate([q_ns[kv][rows] * head_masks[g] for g in range(AT_GROUP)], axis=0)
        lg = lax.dot_general(k_reps[kv][keys], q4, (((1,), (1,)), ((), ())),
                             preferred_element_type=F32) + bias_ref[kv]
        if nb == 0:
            lg = jnp.concatenate([lg[:BLOCK] + pre_seq, lg[BLOCK:]], axis=0)
        return lg

    units = [(kv, nb) for kv in range(AT_KV_HEADS) for nb in range(n_blocks)]
    lg_next = logits_t(*units[0])
    yield
    for idx, (kv, nb) in enumerate(units):
        lg = lg_next
        if idx + 1 < len(units):
            lg_next = logits_t(*units[idx + 1])
            yield
        rows = slice(nb * BLOCK, (nb + 1) * BLOCK)
        keys = slice(nb * BLOCK, (nb + 2) * BLOCK)
        sink = jnp.concatenate(
            [jnp.full((1, BLOCK), sink_ref[kv * AT_GROUP + g], F32) for g in range(AT_GROUP)], axis=1)
        m = jnp.maximum(jnp.max(lg, axis=0, keepdims=True), sink)
        e = jnp.exp(lg - m)
        den = jnp.sum(e, axis=0, keepdims=True) + jnp.exp(sink - m)
        o_t = jnp.dot(v_t[kv * d:(kv + 1) * d, keys], e.astype(BF16),
                      preferred_element_type=F32) * (1.0 / den)
        o = jnp.concatenate([o_t[:, g * BLOCK:(g + 1) * BLOCK] for g in range(AT_GROUP)], axis=0).T
        o_ref[rows, kv * grp:(kv + 1) * grp] = o.astype(o_ref.dtype)
        yield


MERGE_SPLIT = 2


def _merge_kernel(x_ref, ohg_ref, oat_ref, ghg_ref, gat_ref, wbh_ref, wba_ref, wo_ref,
                  gate1_ref, n2g_ref, shift2_ref, scale2_ref, x1_ref, h2_ref):
    part = x_ref.shape[0] // MERGE_SPLIT
    parts = [slice(p * part, (p + 1) * part) for p in range(MERGE_SPLIT)]
    m_hg = [jnp.dot(ohg_ref[rows, :], wbh_ref[...], preferred_element_type=F32) for rows in parts]
    m_at = [jnp.dot(oat_ref[rows, :], wba_ref[...], preferred_element_type=F32) for rows in parts]
    ys = []
    for p, rows in enumerate(parts):
        merged = (jax.nn.sigmoid(ghg_ref[rows, :].astype(F32)) * m_hg[p]
                  + jax.nn.sigmoid(gat_ref[rows, :].astype(F32)) * m_at[p])
        ys.append(jnp.dot(merged.astype(BF16), wo_ref[...], preferred_element_type=F32))
    for p, rows in enumerate(parts):
        x1 = x_ref[rows, :] + gate1_ref[...] * ys[p]
        x1_ref[rows, :] = x1
        h2_ref[rows, :] = _rms_modulate(x1, n2g_ref[...], scale2_ref[...], shift2_ref[...]).astype(h2_ref.dtype)


def _merge(x, o_hg, o_at, gates, w_bh, w_ba, w_o, ada4, norm2_g):
    bsz, seq, d = x.shape
    tm = 512
    resident = lambda shape: pl.BlockSpec(shape, lambda b, i: (0, 0), pipeline_mode=pl.Buffered(1))
    ada_row = lambda k: pl.BlockSpec((None, None, 1, d), lambda b, i: (b, k, 0, 0))
    rows = lambda w, colblk: pl.BlockSpec((None, tm, w), lambda b, i: (b, i, colblk))
    return pl.pallas_call(
        _merge_kernel,
        grid=(bsz, seq // tm),
        in_specs=[
            rows(d, 0), rows(HG_WIDTH, 0), rows(AT_WIDTH, 0), rows(d, 0), rows(d, 1),
            resident(w_bh.shape), resident(w_ba.shape), resident(w_o.shape),
            ada_row(2), pl.BlockSpec((1, d), lambda b, i: (0, 0)), ada_row(3), ada_row(4),
        ],
        out_specs=[rows(d, 0), rows(d, 0)],
        out_shape=[jax.ShapeDtypeStruct((bsz, seq, d), F32), jax.ShapeDtypeStruct((bsz, seq, d), BF16)],
        compiler_params=_params(2),
        name="merge",
    )(x, o_hg, o_at, gates, gates, w_bh, w_ba, w_o, ada4, norm2_g, ada4, ada4)


def _ffn_kernel(h2_ref, w1_ref, w2_ref, x1_ref, gate2_ref, o_ref, acc_ref):
    j = pl.program_id(2)

    @pl.when(j == 0)
    def _():
        acc_ref[...] = jnp.zeros_like(acc_ref)

    hid = jnp.maximum(jnp.dot(h2_ref[...], w1_ref[...], preferred_element_type=F32), 0.0)
    acc_ref[...] += jnp.dot((hid * hid).astype(BF16), w2_ref[...], preferred_element_type=F32)

    @pl.when(j == pl.num_programs(2) - 1)
    def _():
        o_ref[...] = x1_ref[...] + gate2_ref[...] * acc_ref[...]


def _ffn(h2, x1, w1, w2, ada4):
    bsz, seq, d = x1.shape
    dff = w1.shape[-1]
    tm, tf = 512, 1024
    rows = lambda: pl.BlockSpec((None, tm, d), lambda b, i, j: (b, i, 0))
    return pl.pallas_call(
        _ffn_kernel,
        grid=(bsz, seq // tm, dff // tf),
        in_specs=[
            rows(),
            pl.BlockSpec((d, tf), lambda b, i, j: (0, j)),
            pl.BlockSpec((tf, d), lambda b, i, j: (j, 0)),
            rows(),
            pl.BlockSpec((None, None, 1, d), lambda b, i, j: (b, 5, 0, 0)),
        ],
        out_specs=rows(),
        out_shape=jax.ShapeDtypeStruct((bsz, seq, d), F32),
        scratch_shapes=[pltpu.VMEM((tm, d), F32)],
        compiler_params=_params(3),
        name="ffn",
    )(h2, w1, w2, x1, ada4)


def kernel(x, c, w_ada, b_ada, norm1_g, norm2_g, w_in, hg_lb_logits, hg_out_norm_g, q_norm_g, k_norm_g,
           attn_sinks, rel_bias_table, w_branch_hg, w_branch_attn, w_out, w_ff1, w_ff2):
    assert w_ada.shape[0] == 1, "single-layer block"
    bsz = x.shape[0]
    ada = _ada(c, w_ada, b_ada)
    ada4 = ada.reshape(bsz, 6, 1, D_MODEL)
    w_in_bf16 = w_in[0].astype(BF16)
    proj, (w_bh, w_ba, w_o, w1, w2) = _inproj(
        x, norm1_g, ada4, w_in_bf16, GATE_OFF, (w_branch_hg, w_branch_attn, w_out, w_ff1, w_ff2))
    bias = _attn_bias(rel_bias_table)
    gates, o_hg, o_at = _mixer(x, norm1_g, ada4, w_in_bf16, proj, hg_lb_logits, hg_out_norm_g,
                               bias, attn_sinks[0], q_norm_g, k_norm_g)
    x1, h2 = _merge(x, o_hg, o_at, gates, w_bh, w_ba, w_o, ada4, norm2_g)
    return _ffn(h2, x1, w1, w2, ada4)
```

```python
import functools
import math

import jax
import jax.numpy as jnp
from jax import lax
from jax.experimental import pallas as pl
from jax.experimental.pallas import tpu as pltpu

F32 = jnp.float32
BF16 = jnp.bfloat16

D_MODEL = 2048
HG_HEADS = 8
HG_DK = 128
HG_DV = 128
HG_WIDTH = HG_HEADS * HG_DK
HG_CHUNK = 64
AT_HEADS = 16
AT_KV_HEADS = 4
AT_HEAD_DIM = 64
AT_GROUP = AT_HEADS // AT_KV_HEADS
AT_WIDTH = AT_HEADS * AT_HEAD_DIM
KV_WIDTH = AT_KV_HEADS * AT_HEAD_DIM
WINDOW = 128
BLOCK = 128
N_BUCKETS = 32
MAX_EXACT = N_BUCKETS // 2
MAX_DISTANCE = 128
D_FF = 4 * D_MODEL
EPS = 1e-6
NEG_INF = -1e30

HG_OFF = 0
AT_OFF = 4 * HG_WIDTH
GATE_OFF = AT_OFF + AT_WIDTH + 2 * KV_WIDTH
IN_WIDTH = GATE_OFF + 2 * D_MODEL
MXU_COLS = 256

V7X_VMEM_LIMIT_BYTES = 60 * 1024 * 1024
BF16_SUBLANES = 16


def _params(n_axes):
    return pltpu.CompilerParams(
        dimension_semantics=("arbitrary",) * n_axes,
        vmem_limit_bytes=V7X_VMEM_LIMIT_BYTES,
    )


def _rms_modulate(x, g, scale, shift):
    r = lax.rsqrt(jnp.mean(x * x, axis=-1, keepdims=True) + EPS)
    return (x * r * g) * (1.0 + scale) + shift


def _ada_kernel(c_ref, w_ref, b_ref, o_ref):
    c = c_ref[...]
    c_act = c * jax.nn.sigmoid(c)
    o_ref[...] = jnp.dot(c_act, w_ref[...], preferred_element_type=F32,
                         precision=lax.Precision.HIGHEST) + b_ref[...]


def _ada(c, w_ada, b_ada):
    bsz, d = c.shape
    n = w_ada.shape[-1]
    tn = 1024
    return pl.pallas_call(
        _ada_kernel,
        grid=(n // tn,),
        in_specs=[
            pl.BlockSpec((bsz, d), lambda j: (0, 0)),
            pl.BlockSpec((None, d, tn), lambda j: (0, 0, j)),
            pl.BlockSpec((1, tn), lambda j: (0, j)),
        ],
        out_specs=pl.BlockSpec((bsz, tn), lambda j: (0, j)),
        out_shape=jax.ShapeDtypeStruct((bsz, n), F32),
        compiler_params=_params(1),
        name="ada",
    )(c, w_ada, b_ada)


def _inproj_kernel(n_late, x_ref, g_ref, shift_ref, scale_ref, w_ref, *refs):
    late_in, (o_ref, *late_out) = refs[:n_late], refs[n_late:]
    h = _rms_modulate(x_ref[...], g_ref[...], scale_ref[...], shift_ref[...])
    o_ref[...] = jnp.dot(h.astype(BF16), w_ref[...], preferred_element_type=F32).astype(o_ref.dtype)
    for src, dst in zip(late_in, late_out):
        dst[...] = src[...].astype(dst.dtype)


def _inproj(x, norm_g, ada4, w_bf16, n, late_weights):
    bsz, seq, d = x.shape
    tm, n_col_tiles = 512, 2
    tn = n // n_col_tiles
    assert tn % MXU_COLS == 0
    tiles_per_seq = seq // tm
    n_row_tiles = bsz * tiles_per_seq
    n_steps = n_col_tiles * n_row_tiles
    row = lambda j, i: (i // tiles_per_seq, i % tiles_per_seq)
    late_in_specs, late_out_specs, late_shapes = [], [], []
    for w in late_weights:
        _, rows, cols = w.shape
        slab = max(rows // n_steps, BF16_SUBLANES)
        assert rows % slab == 0
        idx = lambda j, i, last=rows // slab - 1: (jnp.minimum(j * n_row_tiles + i, last), 0)
        late_in_specs.append(pl.BlockSpec((None, slab, cols), lambda j, i, idx=idx: (0, *idx(j, i))))
        late_out_specs.append(pl.BlockSpec((slab, cols), idx))
        late_shapes.append(jax.ShapeDtypeStruct((rows, cols), BF16))
    outs = pl.pallas_call(
        functools.partial(_inproj_kernel, len(late_weights)),
        grid=(n_col_tiles, n_row_tiles),
        in_specs=[
            pl.BlockSpec((None, tm, d), lambda j, i: (*row(j, i), 0)),
            pl.BlockSpec((1, d), lambda j, i: (0, 0)),
            pl.BlockSpec((None, None, 1, d), lambda j, i: (i // tiles_per_seq, 0, 0, 0)),
            pl.BlockSpec((None, None, 1, d), lambda j, i: (i // tiles_per_seq, 1, 0, 0)),
            pl.BlockSpec((d, tn), lambda j, i: (0, j)),
            *late_in_specs,
        ],
        out_specs=[pl.BlockSpec((None, tm, tn), lambda j, i: (*row(j, i), j)), *late_out_specs],
        out_shape=[jax.ShapeDtypeStruct((bsz, seq, n), BF16), *late_shapes],
        compiler_params=_params(2),
        name="inproj",
    )(x, norm_g, ada4, ada4, w_bf16, *late_weights)
    return outs[0], outs[1:]


HG_GROUP = 4
HG_ROWS = HG_GROUP * HG_CHUNK
GATE_CHUNK = 256
GATE_W_BLOCK = 512


def _mixer_kernel(n_wg, x_ref, g_ref, shift_ref, scale_ref, *refs):
    wg_refs = refs[:n_wg]
    (q_ref, f_ref, i_ref, hg_ref, lbl_ref, gn_ref, *attn_in,
     gates_ref, o_ref, oat_ref, h_ref, state_ref) = refs[n_wg:]
    c, n = HG_CHUNK, HG_ROWS
    heads = range(HG_HEADS)
    lanes = [slice(u * HG_DK, (u + 1) * HG_DK) for u in heads]

    @pl.when(pl.program_id(1) == 0)
    def _():
        state_ref[...] = jnp.zeros_like(state_ref)

    h_ref[...] = _rms_modulate(x_ref[...], g_ref[...], scale_ref[...], shift_ref[...]).astype(BF16)
    wg_cols = wg_refs[0].shape[1]
    pending = list(range(n_wg * wg_cols // GATE_CHUNK))

    def gate_chunks(count):
        for _ in range(min(count, len(pending))):
            k = pending.pop(0)
            w_ref, off = wg_refs[k * GATE_CHUNK // wg_cols], k * GATE_CHUNK % wg_cols
            gates_ref[:, k * GATE_CHUNK:(k + 1) * GATE_CHUNK] = jnp.dot(
                h_ref[...], w_ref[:, off:off + GATE_CHUNK], preferred_element_type=F32).astype(gates_ref.dtype)

    attention = _attention_steps(*attn_in, oat_ref, pl.program_id(1) == 0)
    attend = lambda: next(attention, None)

    def interleave(count):
        for _ in range(count):
            attend()
            gate_chunks(1)

    lbl = lbl_ref[...]
    e = jnp.exp(lbl - jnp.max(lbl, axis=0, keepdims=True))
    lb = e[0:1] / jnp.sum(e, axis=0, keepdims=True)
    gn = gn_ref[...]

    t = lax.broadcasted_iota(jnp.int32, (n, n), 0)
    s = lax.broadcasted_iota(jnp.int32, (n, n), 1)
    tri = ((t // c) == (s // c)) & (s <= t)
    r8 = lax.broadcasted_iota(jnp.int32, (8, n), 0)
    s8 = lax.broadcasted_iota(jnp.int32, (8, n), 1)
    in_chunk = (s8 // c) == (r8 % HG_GROUP)
    stat_rows = in_chunk & ((r8 >= HG_GROUP) | ((s8 % c) < c // 2))
    cum_ops = jnp.concatenate([tri.astype(F32), stat_rows.astype(F32)], axis=0).astype(BF16)
    per_chunk = lambda rows: jnp.concatenate(
        [jnp.broadcast_to(rows[cc:cc + 1], (c, HG_DK)) for cc in range(HG_GROUP)], axis=0)
    col_chunk = lax.broadcasted_iota(jnp.int32, (HG_DV, n), 1) // c
    col_masks = [(col_chunk == cc).astype(BF16) for cc in range(HG_GROUP)]
    zeros_rows = lambda rows: jnp.zeros((rows, HG_DK), BF16)

    gate_chunks(2)
    attend()
    gate_chunks(1)
    ks, qss, hilo = [], [], []
    for u in heads:
        hf = f_ref[:, lanes[u]].astype(F32)
        hq = q_ref[:, lanes[u]].astype(F32)
        f = lb[:, lanes[u]] + (1.0 - lb[:, lanes[u]]) * jax.nn.sigmoid(hf)
        log_f = jnp.log(f)
        ks.append(1.0 - f)
        qss.append(hq * jax.nn.sigmoid(hq))
        hi = log_f.astype(BF16)
        lo = (log_f - hi.astype(F32)).astype(BF16)
        hilo.append(jnp.concatenate([hi, lo], axis=1))
    rs = [jnp.dot(cum_ops, hilo[u], preferred_element_type=F32) for u in heads]
    interleave(2)
    a_s, bms, kks, qes, decs = [], [], [], [], []
    for u in heads:
        r = rs[u][:, :HG_DK] + rs[u][:, HG_DK:]
        b, b_last = r[0:n], r[n + HG_GROUP:n + 2 * HG_GROUP]
        b_rel = b - per_chunk(r[n:n + HG_GROUP])
        a_s.append((qss[u] * jnp.exp(b_rel)).astype(BF16))
        bms.append((ks[u] * jnp.exp(-b_rel)).astype(BF16))
        kks.append((ks[u] * jnp.exp(per_chunk(b_last) - b)).astype(BF16))
        qes.append((qss[u] * jnp.exp(b)).astype(BF16))
        decs.append(jnp.exp(b_last))
    scores = [lax.dot_general(a_s[u], bms[u], (((1,), (1,)), ((), ())), preferred_element_type=F32)
              for u in heads]
    interleave(2)
    vs = [i_ref[:, lanes[u]] for u in heads]
    os = [jnp.dot(jnp.where(tri, scores[u], 0.0).astype(BF16), vs[u], preferred_element_type=F32)
          for u in heads]
    interleave(2)
    upd_ts = []
    for u in heads:
        v_t = vs[u].astype(F32).T.astype(BF16)
        v_t_blocks = jnp.concatenate([v_t * col_masks[cc] for cc in range(HG_GROUP)], axis=0)
        upd_ts.append(jnp.dot(v_t_blocks, kks[u], preferred_element_type=F32))
    interleave(2)
    for u in heads:
        s_t = state_ref[u]
        states = []
        for cc in range(HG_GROUP):
            states.append(s_t.astype(BF16))
            s_t = s_t * decs[u][cc:cc + 1] + upd_ts[u][cc * HG_DV:(cc + 1) * HG_DV]
        state_ref[u] = s_t
        qe = qes[u]
        qe_blocks = jnp.concatenate(
            [jnp.concatenate([p for p in (zeros_rows(cc * c), qe[cc * c:(cc + 1) * c],
                                          zeros_rows(n - (cc + 1) * c)) if p.shape[0]], axis=0)
             for cc in range(HG_GROUP)], axis=1)
        os[u] = os[u] + lax.dot_general(qe_blocks, jnp.concatenate(states, axis=1),
                                        (((1,), (1,)), ((), ())), preferred_element_type=F32)
    interleave(len(pending) - 2)
    for _ in attention:
        pass
    gate_chunks(len(pending))
    for u in heads:
        o = os[u]
        hg = hg_ref[:, lanes[u]].astype(F32)
        y = o * lax.rsqrt(jnp.mean(o * o, axis=-1, keepdims=True) + EPS) * gn
        o_ref[:, lanes[u]] = (y * (hg * jax.nn.sigmoid(hg))).astype(o_ref.dtype)


def _mixer(x, norm_g, ada4, w_in_bf16, proj, lb_logits, out_norm_g, bias, sinks, q_norm_g, k_norm_g):
    bsz, seq, d = x.shape
    n_gate = w_in_bf16.shape[-1] - GATE_OFF
    tm = HG_ROWS
    blocks_per_step = tm // BLOCK
    rows = lambda w, colblk: pl.BlockSpec((None, tm, w), lambda b, i: (b, i, colblk))
    ada_row = lambda k: pl.BlockSpec((None, None, 1, d), lambda b, i: (b, k, 0, 0))
    const = lambda shape: pl.BlockSpec(shape, lambda b, i: (0,) * len(shape))
    n_wg = n_gate // GATE_W_BLOCK
    w_gate = [pl.BlockSpec((d, GATE_W_BLOCK), lambda b, i, k=k: (0, GATE_OFF // GATE_W_BLOCK + k),
                           pipeline_mode=pl.Buffered(1)) for k in range(n_wg)]
    hg_base = HG_OFF // HG_WIDTH
    k_col = (AT_OFF + AT_WIDTH) // KV_WIDTH
    halo = lambda colblk: pl.BlockSpec(
        (None, BLOCK, KV_WIDTH), lambda b, i: (b, jnp.maximum(i * blocks_per_step - 1, 0), colblk))
    qg = jnp.tile(q_norm_g, (1, AT_GROUP))
    kg = jnp.tile(k_norm_g, (1, AT_KV_HEADS))
    return pl.pallas_call(
        functools.partial(_mixer_kernel, n_wg),
        grid=(bsz, seq // tm),
        in_specs=[
            rows(d, 0), const((1, d)), ada_row(0), ada_row(1), *w_gate,
            rows(HG_WIDTH, hg_base), rows(HG_WIDTH, hg_base + 1), rows(HG_WIDTH, hg_base + 2),
            rows(HG_WIDTH, hg_base + 3), const(lb_logits.shape), const((1, HG_DV)),
            rows(AT_WIDTH, AT_OFF // AT_WIDTH), rows(KV_WIDTH, k_col), rows(KV_WIDTH, k_col + 1),
            halo(k_col), halo(k_col + 1),
            pl.BlockSpec(bias.shape, lambda b, i: (0, 0, 0), pipeline_mode=pl.Buffered(1)),
            pl.BlockSpec(memory_space=pltpu.SMEM), const((1, AT_GROUP * AT_HEAD_DIM)), const((1, KV_WIDTH)),
        ],
        out_specs=[rows(n_gate, 0), rows(HG_WIDTH, 0), rows(AT_WIDTH, 0)],
        out_shape=[jax.ShapeDtypeStruct((bsz, seq, n_gate), BF16),
                   jax.ShapeDtypeStruct((bsz, seq, HG_WIDTH), BF16),
                   jax.ShapeDtypeStruct((bsz, seq, AT_WIDTH), BF16)],
        scratch_shapes=[pltpu.VMEM((tm, d), BF16), pltpu.VMEM((HG_HEADS, HG_DV, HG_DK), F32)],
        compiler_params=_params(2),
        name="mixer",
    )(x, norm_g, ada4, ada4, *([w_in_bf16] * n_wg), proj, proj, proj, proj, lb_logits, out_norm_g,
      proj, proj, proj, proj, proj, bias, sinks, qg, kg)


def _t5_causal_bucket(n):
    nf = jnp.maximum(n, 1).astype(F32)
    large = MAX_EXACT + (jnp.log(nf / MAX_EXACT) / math.log(MAX_DISTANCE / MAX_EXACT)
                         * (N_BUCKETS - MAX_EXACT)).astype(jnp.int32)
    large = jnp.minimum(large, N_BUCKETS - 1)
    return jnp.where(n < MAX_EXACT, n, large)


def _bias_kernel(table_ref, bucket_t_ref, o_ref):
    bucket_t = bucket_t_ref[...]
    j = lax.broadcasted_iota(jnp.int32, bucket_t.shape, 0)
    i = lax.broadcasted_iota(jnp.int32, bucket_t.shape, 1)
    dist = i - j + BLOCK
    band = (dist >= 0) & (dist < WINDOW)
    for h in range(AT_HEADS):
        acc = jnp.zeros(bucket_t.shape, F32)
        for r in range(N_BUCKETS):
            acc = jnp.where(bucket_t == r, table_ref[r, h], acc)
        g = h % AT_GROUP
        o_ref[h // AT_GROUP, :, g * BLOCK:(g + 1) * BLOCK] = jnp.where(band, acc, NEG_INF)


def _attn_bias(rel_bias_table):
    i = jnp.arange(BLOCK, dtype=jnp.int32)[None, :]
    j = jnp.arange(2 * BLOCK, dtype=jnp.int32)[:, None]
    bucket_t = _t5_causal_bucket(jnp.maximum(i - j + BLOCK, 0))
    shape = (AT_KV_HEADS, 2 * BLOCK, AT_GROUP * BLOCK)
    return pl.pallas_call(
        _bias_kernel,
        in_specs=[
            pl.BlockSpec(memory_space=pltpu.SMEM),
            pl.BlockSpec((2 * BLOCK, BLOCK), lambda: (0, 0)),
        ],
        out_specs=pl.BlockSpec(shape, lambda: (0, 0, 0)),
        out_shape=jax.ShapeDtypeStruct(shape, F32),
        name="attn_bias",
    )(rel_bias_table, bucket_t)


def _head_mean_square(t, pool):
    return jnp.dot((t * t).astype(BF16), pool, preferred_element_type=F32)


def _attention_steps(q_ref, k_ref, v_ref, kh_ref, vh_ref, bias_ref, sink_ref, qg_ref, kg_ref, o_ref, first_block):
    n_blocks = q_ref.shape[0] // BLOCK
    d = AT_HEAD_DIM
    grp = AT_GROUP * d
    sub = lax.broadcasted_iota(jnp.int32, (grp, grp), 0)
    lane = lax.broadcasted_iota(jnp.int32, (grp, grp), 1)
    pool = jnp.where(sub // d == lane // d, 1.0 / d, 0.0).astype(BF16)

    k_all = jnp.concatenate([kh_ref[...], k_ref[...]], axis=0).astype(F32)
    k_n = (k_all * lax.rsqrt(_head_mean_square(k_all, pool) + EPS) * kg_ref[...]).astype(BF16)
    v_t = jnp.concatenate([vh_ref[...], v_ref[...]], axis=0).astype(F32).T.astype(BF16)

    lane_head = lax.broadcasted_iota(jnp.int32, (BLOCK, grp), 1) // d
    head_masks = [(lane_head == g).astype(BF16) for g in range(AT_GROUP)]
    pre_seq = jnp.where(first_block, NEG_INF, 0.0)

    scale = d ** -0.5
    k_reps, q_ns = [], []
    for kv in range(AT_KV_HEADS):
        spread = ((sub // d == kv) & (sub % d == lane % d)).astype(BF16)
        k_reps.append(jnp.dot(k_n, spread, preferred_element_type=F32).astype(BF16))
        q_g = q_ref[:, kv * grp:(kv + 1) * grp].astype(F32)
        q_ns.append((q_g * lax.rsqrt(_head_mean_square(q_g, pool) + EPS) * (qg_ref[...] * scale)).astype(BF16))
    yield

    def logits_t(kv, nb):
        rows = slice(nb * BLOCK, (nb + 1) * BLOCK)
        keys = slice(nb * BLOCK, (nb + 2) * BLOCK)
        q4 = jnp.concatenate([q_ns[kv][rows] * head_masks[g] for g in range(AT_GROUP)], axis=0)
        lg = lax.dot_general(k_reps[kv][keys], q4, (((1,), (1,)), ((), ())),
                             preferred_element_type=F32) + bias_ref[kv]
        if nb == 0:
            lg = jnp.concatenate([lg[:BLOCK] + pre_seq, lg[BLOCK:]], axis=0)
        return lg

    units = [(kv, nb) for kv in range(AT_KV_HEADS) for nb in range(n_blocks)]
    lg_next = logits_t(*units[0])
    yield
    for idx, (kv, nb) in enumerate(units):
        lg = lg_next
        if idx + 1 < len(units):
            lg_next = logits_t(*units[idx + 1])
            yield
        rows = slice(nb * BLOCK, (nb + 1) * BLOCK)
        keys = slice(nb * BLOCK, (nb + 2) * BLOCK)
        sink = jnp.concatenate(
            [jnp.full((1, BLOCK), sink_ref[kv * AT_GROUP + g], F32) for g in range(AT_GROUP)], axis=1)
        m = jnp.maximum(jnp.max(lg, axis=0, keepdims=True), sink)
        e = jnp.exp(lg - m)
        den = jnp.sum(e, axis=0, keepdims=True) + jnp.exp(sink - m)
        o_t = jnp.dot(v_t[kv * d:(kv + 1) * d, keys], e.astype(BF16),
                      preferred_element_type=F32) * (1.0 / den)
        o = jnp.concatenate([o_t[:, g * BLOCK:(g + 1) * BLOCK] for g in range(AT_GROUP)], axis=0).T
        o_ref[rows, kv * grp:(kv + 1) * grp] = o.astype(o_ref.dtype)
        yield


MERGE_SPLIT = 2


def _merge_kernel(x_ref, ohg_ref, oat_ref, ghg_ref, gat_ref, wbh_ref, wba_ref, wo_ref,
                  gate1_ref, n2g_ref, shift2_ref, scale2_ref, x1_ref, h2_ref):
    part = x_ref.shape[0] // MERGE_SPLIT
    parts = [slice(p * part, (p + 1) * part) for p in range(MERGE_SPLIT)]
    m_hg = [jnp.dot(ohg_ref[rows, :], wbh_ref[...], preferred_element_type=F32) for rows in parts]
    m_at = [jnp.dot(oat_ref[rows, :], wba_ref[...], preferred_element_type=F32) for rows in parts]
    ys = []
    for p, rows in enumerate(parts):
        merged = (jax.nn.sigmoid(ghg_ref[rows, :].astype(F32)) * m_hg[p]
                  + jax.nn.sigmoid(gat_ref[rows, :].astype(F32)) * m_at[p])
        ys.append(jnp.dot(merged.astype(BF16), wo_ref[...], preferred_element_type=F32))
    for p, rows in enumerate(parts):
        x1 = x_ref[rows, :] + gate1_ref[...] * ys[p]
        x1_ref[rows, :] = x1
        h2_ref[rows, :] = _rms_modulate(x1, n2g_ref[...], scale2_ref[...], shift2_ref[...]).astype(h2_ref.dtype)


def _merge(x, o_hg, o_at, gates, w_bh, w_ba, w_o, ada4, norm2_g):
    bsz, seq, d = x.shape
    tm = 512
    resident = lambda shape: pl.BlockSpec(shape, lambda b, i: (0, 0), pipeline_mode=pl.Buffered(1))
    ada_row = lambda k: pl.BlockSpec((None, None, 1, d), lambda b, i: (b, k, 0, 0))
    rows = lambda w, colblk: pl.BlockSpec((None, tm, w), lambda b, i: (b, i, colblk))
    return pl.pallas_call(
        _merge_kernel,
        grid=(bsz, seq // tm),
        in_specs=[
            rows(d, 0), rows(HG_WIDTH, 0), rows(AT_WIDTH, 0), rows(d, 0), rows(d, 1),
            resident(w_bh.shape), resident(w_ba.shape), resident(w_o.shape),
            ada_row(2), pl.BlockSpec((1, d), lambda b, i: (0, 0)), ada_row(3), ada_row(4),
        ],
        out_specs=[rows(d, 0), rows(d, 0)],
        out_shape=[jax.ShapeDtypeStruct((bsz, seq, d), F32), jax.ShapeDtypeStruct((bsz, seq, d), BF16)],
        compiler_params=_params(2),
        name="merge",
    )(x, o_hg, o_at, gates, gates, w_bh, w_ba, w_o, ada4, norm2_g, ada4, ada4)


FFN_TF = 1024


def _ffn_kernel(h2_ref, x1_ref, gate2_ref, w1_hbm, w2_hbm, o_ref, w1_buf, w2_buf, sem):
    tf = w1_buf.shape[2]
    n_tiles = w1_hbm.shape[1] // tf
    step = pl.program_id(0) * pl.num_programs(1) + pl.program_id(1)
    n_steps = pl.num_programs(0) * pl.num_programs(1)

    def tile_copies(t, slot):
        return (pltpu.make_async_copy(w1_hbm.at[:, pl.ds(t * tf, tf)], w1_buf.at[slot], sem.at[0, slot]),
                pltpu.make_async_copy(w2_hbm.at[pl.ds(t * tf, tf), :], w2_buf.at[slot], sem.at[1, slot]))

    def start(t, slot):
        for copy in tile_copies(t, slot):
            copy.start()

    @pl.when(step == 0)
    def _():
        start(0, 0)

    assert n_tiles % 2 == 0
    for t in range(n_tiles):
        slot = t % 2
        if t + 1 < n_tiles:
            start(t + 1, 1 - slot)
        else:
            @pl.when(step + 1 < n_steps)
            def _():
                start(0, 1 - slot)
        for copy in tile_copies(t, slot):
            copy.wait()
        hid = jnp.maximum(jnp.dot(h2_ref[...], w1_buf[slot], preferred_element_type=F32), 0.0)
        down = jnp.dot((hid * hid).astype(BF16), w2_buf[slot], preferred_element_type=F32)
        if t == 0:
            o_ref[...] = down
        else:
            o_ref[...] += down
    o_ref[...] = x1_ref[...] + gate2_ref[...] * o_ref[...]


def _ffn(h2, x1, w1, w2, ada4):
    bsz, seq, d = x1.shape
    tm, tf = 512, FFN_TF
    rows = lambda: pl.BlockSpec((None, tm, d), lambda b, i: (b, i, 0))
    return pl.pallas_call(
        _ffn_kernel,
        grid=(bsz, seq // tm),
        in_specs=[
            rows(), rows(),
            pl.BlockSpec((None, None, 1, d), lambda b, i: (b, 5, 0, 0)),
            pl.BlockSpec(memory_space=pl.ANY), pl.BlockSpec(memory_space=pl.ANY),
        ],
        out_specs=rows(),
        out_shape=jax.ShapeDtypeStruct((bsz, seq, d), F32),
        scratch_shapes=[pltpu.VMEM((2, d, tf), BF16), pltpu.VMEM((2, tf, d), BF16),
                        pltpu.SemaphoreType.DMA((2, 2))],
        compiler_params=_params(2),
        name="ffn",
    )(h2, x1, ada4, w1, w2)


def kernel(x, c, w_ada, b_ada, norm1_g, norm2_g, w_in, hg_lb_logits, hg_out_norm_g, q_norm_g, k_norm_g,
           attn_sinks, rel_bias_table, w_branch_hg, w_branch_attn, w_out, w_ff1, w_ff2):
    assert w_ada.shape[0] == 1, "single-layer block"
    bsz = x.shape[0]
    ada = _ada(c, w_ada, b_ada)
    ada4 = ada.reshape(bsz, 6, 1, D_MODEL)
    w_in_bf16 = w_in[0].astype(BF16)
    proj, (w_bh, w_ba, w_o, w1, w2) = _inproj(
        x, norm1_g, ada4, w_in_bf16, GATE_OFF, (w_branch_hg, w_branch_attn, w_out, w_ff1, w_ff2))
    bias = _attn_bias(rel_bias_table)
    gates, o_hg, o_at = _mixer(x, norm1_g, ada4, w_in_bf16, proj, hg_lb_logits, hg_out_norm_g,
                               bias, attn_sinks[0], q_norm_g, k_norm_g)
    x1, h2 = _merge(x, o_hg, o_at, gates, w_bh, w_ba, w_o, ada4, norm2_g)
    return _ffn(h2, x1, w1, w2, ada4)
```

```python
import functools
import math

import jax
import jax.numpy as jnp
from jax import lax
from jax.experimental import pallas as pl
from jax.experimental.pallas import tpu as pltpu

F32 = jnp.float32
BF16 = jnp.bfloat16

D_MODEL = 2048
HG_HEADS = 8
HG_DK = 128
HG_DV = 128
HG_WIDTH = HG_HEADS * HG_DK
HG_CHUNK = 64
AT_HEADS = 16
AT_KV_HEADS = 4
AT_HEAD_DIM = 64
AT_GROUP = AT_HEADS // AT_KV_HEADS
AT_WIDTH = AT_HEADS * AT_HEAD_DIM
KV_WIDTH = AT_KV_HEADS * AT_HEAD_DIM
WINDOW = 128
BLOCK = 128
N_BUCKETS = 32
MAX_EXACT = N_BUCKETS // 2
MAX_DISTANCE = 128
D_FF = 4 * D_MODEL
EPS = 1e-6
NEG_INF = -1e30

HG_OFF = 0
AT_OFF = 4 * HG_WIDTH
GATE_OFF = AT_OFF + AT_WIDTH + 2 * KV_WIDTH
IN_WIDTH = GATE_OFF + 2 * D_MODEL
MXU_COLS = 256

V7X_VMEM_LIMIT_BYTES = 60 * 1024 * 1024
BF16_SUBLANES = 16


def _params(n_axes):
    return pltpu.CompilerParams(
        dimension_semantics=("arbitrary",) * n_axes,
        vmem_limit_bytes=V7X_VMEM_LIMIT_BYTES,
    )


def _rms_modulate(x, g, scale, shift):
    r = lax.rsqrt(jnp.mean(x * x, axis=-1, keepdims=True) + EPS)
    return (x * r * g) * (1.0 + scale) + shift


def _ada_kernel(c_ref, w_ref, b_ref, o_ref):
    c = c_ref[...]
    c_act = c * jax.nn.sigmoid(c)
    o_ref[...] = jnp.dot(c_act, w_ref[...], preferred_element_type=F32,
                         precision=lax.Precision.HIGHEST) + b_ref[...]


def _ada(c, w_ada, b_ada):
    bsz, d = c.shape
    n = w_ada.shape[-1]
    tn = 1024
    return pl.pallas_call(
        _ada_kernel,
        grid=(n // tn,),
        in_specs=[
            pl.BlockSpec((bsz, d), lambda j: (0, 0)),
            pl.BlockSpec((None, d, tn), lambda j: (0, 0, j)),
            pl.BlockSpec((1, tn), lambda j: (0, j)),
        ],
        out_specs=pl.BlockSpec((bsz, tn), lambda j: (0, j)),
        out_shape=jax.ShapeDtypeStruct((bsz, n), F32),
        compiler_params=_params(1),
        name="ada",
    )(c, w_ada, b_ada)


def _inproj_kernel(n_late, x_ref, g_ref, shift_ref, scale_ref, w_ref, *refs):
    late_in, (o_ref, *late_out) = refs[:n_late], refs[n_late:]
    h = _rms_modulate(x_ref[...], g_ref[...], scale_ref[...], shift_ref[...])
    o_ref[...] = jnp.dot(h.astype(BF16), w_ref[...], preferred_element_type=F32).astype(o_ref.dtype)
    for src, dst in zip(late_in, late_out):
        if len(dst.shape) == 2:
            dst[...] = src[...].astype(dst.dtype)
        else:
            ct = dst.shape[2]
            for t in range(dst.shape[0]):
                dst[t] = src[:, t * ct:(t + 1) * ct].astype(dst.dtype)


def _inproj(x, norm_g, ada4, w_bf16, n, late_weights):
    bsz, seq, d = x.shape
    tm, n_col_tiles = 512, 2
    tn = n // n_col_tiles
    assert tn % MXU_COLS == 0
    tiles_per_seq = seq // tm
    n_row_tiles = bsz * tiles_per_seq
    n_steps = n_col_tiles * n_row_tiles
    row = lambda j, i: (i // tiles_per_seq, i % tiles_per_seq)
    late_in_specs, late_out_specs, late_shapes = [], [], []
    for w, col_tile in late_weights:
        _, rows, cols = w.shape
        slab = max(rows // n_steps, BF16_SUBLANES)
        assert rows % slab == 0
        idx = lambda j, i, last=rows // slab - 1: (jnp.minimum(j * n_row_tiles + i, last), 0)
        late_in_specs.append(pl.BlockSpec((None, slab, cols), lambda j, i, idx=idx: (0, *idx(j, i))))
        if col_tile is None:
            late_out_specs.append(pl.BlockSpec((slab, cols), idx))
            late_shapes.append(jax.ShapeDtypeStruct((rows, cols), BF16))
        else:
            late_out_specs.append(pl.BlockSpec((cols // col_tile, slab, col_tile),
                                               lambda j, i, idx=idx: (0, *idx(j, i))))
            late_shapes.append(jax.ShapeDtypeStruct((cols // col_tile, rows, col_tile), BF16))
    outs = pl.pallas_call(
        functools.partial(_inproj_kernel, len(late_weights)),
        grid=(n_col_tiles, n_row_tiles),
        in_specs=[
            pl.BlockSpec((None, tm, d), lambda j, i: (*row(j, i), 0)),
            pl.BlockSpec((1, d), lambda j, i: (0, 0)),
            pl.BlockSpec((None, None, 1, d), lambda j, i: (i // tiles_per_seq, 0, 0, 0)),
            pl.BlockSpec((None, None, 1, d), lambda j, i: (i // tiles_per_seq, 1, 0, 0)),
            pl.BlockSpec((d, tn), lambda j, i: (0, j)),
            *late_in_specs,
        ],
        out_specs=[pl.BlockSpec((None, tm, tn), lambda j, i: (*row(j, i), j)), *late_out_specs],
        out_shape=[jax.ShapeDtypeStruct((bsz, seq, n), BF16), *late_shapes],
        compiler_params=_params(2),
        name="inproj",
    )(x, norm_g, ada4, ada4, w_bf16, *[w for w, _ in late_weights])
    return outs[0], outs[1:]


HG_GROUP = 4
HG_ROWS = HG_GROUP * HG_CHUNK
GATE_CHUNK = 256
GATE_W_BLOCK = 512


def _mixer_kernel(n_wg, x_ref, g_ref, shift_ref, scale_ref, *refs):
    wg_refs = refs[:n_wg]
    (q_ref, f_ref, i_ref, hg_ref, lbl_ref, gn_ref, *attn_in,
     gates_ref, o_ref, oat_ref, h_ref, state_ref) = refs[n_wg:]
    c, n = HG_CHUNK, HG_ROWS
    heads = range(HG_HEADS)
    lanes = [slice(u * HG_DK, (u + 1) * HG_DK) for u in heads]

    @pl.when(pl.program_id(1) == 0)
    def _():
        state_ref[...] = jnp.zeros_like(state_ref)

    h_ref[...] = _rms_modulate(x_ref[...], g_ref[...], scale_ref[...], shift_ref[...]).astype(BF16)
    wg_cols = wg_refs[0].shape[1]
    pending = list(range(n_wg * wg_cols // GATE_CHUNK))

    def gate_chunks(count):
        for _ in range(min(count, len(pending))):
            k = pending.pop(0)
            w_ref, off = wg_refs[k * GATE_CHUNK // wg_cols], k * GATE_CHUNK % wg_cols
            gates_ref[:, k * GATE_CHUNK:(k + 1) * GATE_CHUNK] = jnp.dot(
                h_ref[...], w_ref[:, off:off + GATE_CHUNK], preferred_element_type=F32).astype(gates_ref.dtype)

    attention = _attention_steps(*attn_in, oat_ref, pl.program_id(1) == 0)
    attend = lambda: next(attention, None)

    def interleave(count):
        for _ in range(count):
            attend()
            gate_chunks(1)

    lbl = lbl_ref[...]
    e = jnp.exp(lbl - jnp.max(lbl, axis=0, keepdims=True))
    lb = e[0:1] / jnp.sum(e, axis=0, keepdims=True)
    gn = gn_ref[...]

    t = lax.broadcasted_iota(jnp.int32, (n, n), 0)
    s = lax.broadcasted_iota(jnp.int32, (n, n), 1)
    tri = ((t // c) == (s // c)) & (s <= t)
    r8 = lax.broadcasted_iota(jnp.int32, (8, n), 0)
    s8 = lax.broadcasted_iota(jnp.int32, (8, n), 1)
    in_chunk = (s8 // c) == (r8 % HG_GROUP)
    stat_rows = in_chunk & ((r8 >= HG_GROUP) | ((s8 % c) < c // 2))
    cum_ops = jnp.concatenate([tri.astype(F32), stat_rows.astype(F32)], axis=0).astype(BF16)
    per_chunk = lambda rows: jnp.concatenate(
        [jnp.broadcast_to(rows[cc:cc + 1], (c, HG_DK)) for cc in range(HG_GROUP)], axis=0)
    col_chunk = lax.broadcasted_iota(jnp.int32, (HG_DV, n), 1) // c
    col_masks = [(col_chunk == cc).astype(BF16) for cc in range(HG_GROUP)]
    zeros_rows = lambda rows: jnp.zeros((rows, HG_DK), BF16)

    gate_chunks(2)
    attend()
    gate_chunks(1)
    ks, qss, hilo = [], [], []
    for u in heads:
        hf = f_ref[:, lanes[u]].astype(F32)
        hq = q_ref[:, lanes[u]].astype(F32)
        f = lb[:, lanes[u]] + (1.0 - lb[:, lanes[u]]) * jax.nn.sigmoid(hf)
        log_f = jnp.log(f)
        ks.append(1.0 - f)
        qss.append(hq * jax.nn.sigmoid(hq))
        hi = log_f.astype(BF16)
        lo = (log_f - hi.astype(F32)).astype(BF16)
        hilo.append(jnp.concatenate([hi, lo], axis=1))
    rs = [jnp.dot(cum_ops, hilo[u], preferred_element_type=F32) for u in heads]
    interleave(2)
    a_s, bms, kks, qes, decs = [], [], [], [], []
    for u in heads:
        r = rs[u][:, :HG_DK] + rs[u][:, HG_DK:]
        b, b_last = r[0:n], r[n + HG_GROUP:n + 2 * HG_GROUP]
        b_rel = b - per_chunk(r[n:n + HG_GROUP])
        a_s.append((qss[u] * jnp.exp(b_rel)).astype(BF16))
        bms.append((ks[u] * jnp.exp(-b_rel)).astype(BF16))
        kks.append((ks[u] * jnp.exp(per_chunk(b_last) - b)).astype(BF16))
        qes.append((qss[u] * jnp.exp(b)).astype(BF16))
        decs.append(jnp.exp(b_last))
    scores = [lax.dot_general(a_s[u], bms[u], (((1,), (1,)), ((), ())), preferred_element_type=F32)
              for u in heads]
    interleave(2)
    vs = [i_ref[:, lanes[u]] for u in heads]
    os = [jnp.dot(jnp.where(tri, scores[u], 0.0).astype(BF16), vs[u], preferred_element_type=F32)
          for u in heads]
    interleave(2)
    upd_ts = []
    for u in heads:
        v_t = vs[u].astype(F32).T.astype(BF16)
        v_t_blocks = jnp.concatenate([v_t * col_masks[cc] for cc in range(HG_GROUP)], axis=0)
        upd_ts.append(jnp.dot(v_t_blocks, kks[u], preferred_element_type=F32))
    interleave(2)
    for u in heads:
        s_t = state_ref[u]
        states = []
        for cc in range(HG_GROUP):
            states.append(s_t.astype(BF16))
            s_t = s_t * decs[u][cc:cc + 1] + upd_ts[u][cc * HG_DV:(cc + 1) * HG_DV]
        state_ref[u] = s_t
        qe = qes[u]
        qe_blocks = jnp.concatenate(
            [jnp.concatenate([p for p in (zeros_rows(cc * c), qe[cc * c:(cc + 1) * c],
                                          zeros_rows(n - (cc + 1) * c)) if p.shape[0]], axis=0)
             for cc in range(HG_GROUP)], axis=1)
        os[u] = os[u] + lax.dot_general(qe_blocks, jnp.concatenate(states, axis=1),
                                        (((1,), (1,)), ((), ())), preferred_element_type=F32)
    interleave(len(pending) - 2)
    for _ in attention:
        pass
    gate_chunks(len(pending))
    for u in heads:
        o = os[u]
        hg = hg_ref[:, lanes[u]].astype(F32)
        y = o * lax.rsqrt(jnp.mean(o * o, axis=-1, keepdims=True) + EPS) * gn
        o_ref[:, lanes[u]] = (y * (hg * jax.nn.sigmoid(hg))).astype(o_ref.dtype)


def _mixer(x, norm_g, ada4, w_in_bf16, proj, lb_logits, out_norm_g, bias, sinks, q_norm_g, k_norm_g):
    bsz, seq, d = x.shape
    n_gate = w_in_bf16.shape[-1] - GATE_OFF
    tm = HG_ROWS
    blocks_per_step = tm // BLOCK
    rows = lambda w, colblk: pl.BlockSpec((None, tm, w), lambda b, i: (b, i, colblk))
    ada_row = lambda k: pl.BlockSpec((None, None, 1, d), lambda b, i: (b, k, 0, 0))
    const = lambda shape: pl.BlockSpec(shape, lambda b, i: (0,) * len(shape))
    n_wg = n_gate // GATE_W_BLOCK
    w_gate = [pl.BlockSpec((d, GATE_W_BLOCK), lambda b, i, k=k: (0, GATE_OFF // GATE_W_BLOCK + k),
                           pipeline_mode=pl.Buffered(1)) for k in range(n_wg)]
    hg_base = HG_OFF // HG_WIDTH
    k_col = (AT_OFF + AT_WIDTH) // KV_WIDTH
    halo = lambda colblk: pl.BlockSpec(
        (None, BLOCK, KV_WIDTH), lambda b, i: (b, jnp.maximum(i * blocks_per_step - 1, 0), colblk))
    qg = jnp.tile(q_norm_g, (1, AT_GROUP))
    kg = jnp.tile(k_norm_g, (1, AT_KV_HEADS))
    return pl.pallas_call(
        functools.partial(_mixer_kernel, n_wg),
        grid=(bsz, seq // tm),
        in_specs=[
            rows(d, 0), const((1, d)), ada_row(0), ada_row(1), *w_gate,
            rows(HG_WIDTH, hg_base), rows(HG_WIDTH, hg_base + 1), rows(HG_WIDTH, hg_base + 2),
            rows(HG_WIDTH, hg_base + 3), const(lb_logits.shape), const((1, HG_DV)),
            rows(AT_WIDTH, AT_OFF // AT_WIDTH), rows(KV_WIDTH, k_col), rows(KV_WIDTH, k_col + 1),
            halo(k_col), halo(k_col + 1),
            pl.BlockSpec(bias.shape, lambda b, i: (0, 0, 0), pipeline_mode=pl.Buffered(1)),
            pl.BlockSpec(memory_space=pltpu.SMEM), const((1, AT_GROUP * AT_HEAD_DIM)), const((1, KV_WIDTH)),
        ],
        out_specs=[rows(n_gate, 0), rows(HG_WIDTH, 0), rows(AT_WIDTH, 0)],
        out_shape=[jax.ShapeDtypeStruct((bsz, seq, n_gate), BF16),
                   jax.ShapeDtypeStruct((bsz, seq, HG_WIDTH), BF16),
                   jax.ShapeDtypeStruct((bsz, seq, AT_WIDTH), BF16)],
        scratch_shapes=[pltpu.VMEM((tm, d), BF16), pltpu.VMEM((HG_HEADS, HG_DV, HG_DK), F32)],
        compiler_params=_params(2),
        name="mixer",
    )(x, norm_g, ada4, ada4, *([w_in_bf16] * n_wg), proj, proj, proj, proj, lb_logits, out_norm_g,
      proj, proj, proj, proj, proj, bias, sinks, qg, kg)


def _t5_causal_bucket(n):
    nf = jnp.maximum(n, 1).astype(F32)
    large = MAX_EXACT + (jnp.log(nf / MAX_EXACT) / math.log(MAX_DISTANCE / MAX_EXACT)
                         * (N_BUCKETS - MAX_EXACT)).astype(jnp.int32)
    large = jnp.minimum(large, N_BUCKETS - 1)
    return jnp.where(n < MAX_EXACT, n, large)


def _bias_kernel(table_ref, bucket_t_ref, o_ref):
    bucket_t = bucket_t_ref[...]
    j = lax.broadcasted_iota(jnp.int32, bucket_t.shape, 0)
    i = lax.broadcasted_iota(jnp.int32, bucket_t.shape, 1)
    dist = i - j + BLOCK
    band = (dist >= 0) & (dist < WINDOW)
    for h in range(AT_HEADS):
        acc = jnp.zeros(bucket_t.shape, F32)
        for r in range(N_BUCKETS):
            acc = jnp.where(bucket_t == r, table_ref[r, h], acc)
        g = h % AT_GROUP
        o_ref[h // AT_GROUP, :, g * BLOCK:(g + 1) * BLOCK] = jnp.where(band, acc, NEG_INF)


def _attn_bias(rel_bias_table):
    i = jnp.arange(BLOCK, dtype=jnp.int32)[None, :]
    j = jnp.arange(2 * BLOCK, dtype=jnp.int32)[:, None]
    bucket_t = _t5_causal_bucket(jnp.maximum(i - j + BLOCK, 0))
    shape = (AT_KV_HEADS, 2 * BLOCK, AT_GROUP * BLOCK)
    return pl.pallas_call(
        _bias_kernel,
        in_specs=[
            pl.BlockSpec(memory_space=pltpu.SMEM),
            pl.BlockSpec((2 * BLOCK, BLOCK), lambda: (0, 0)),
        ],
        out_specs=pl.BlockSpec(shape, lambda: (0, 0, 0)),
        out_shape=jax.ShapeDtypeStruct(shape, F32),
        name="attn_bias",
    )(rel_bias_table, bucket_t)


def _head_mean_square(t, pool):
    return jnp.dot((t * t).astype(BF16), pool, preferred_element_type=F32)


def _attention_steps(q_ref, k_ref, v_ref, kh_ref, vh_ref, bias_ref, sink_ref, qg_ref, kg_ref, o_ref, first_block):
    n_blocks = q_ref.shape[0] // BLOCK
    d = AT_HEAD_DIM
    grp = AT_GROUP * d
    sub = lax.broadcasted_iota(jnp.int32, (grp, grp), 0)
    lane = lax.broadcasted_iota(jnp.int32, (grp, grp), 1)
    pool = jnp.where(sub // d == lane // d, 1.0 / d, 0.0).astype(BF16)

    k_all = jnp.concatenate([kh_ref[...], k_ref[...]], axis=0).astype(F32)
    k_n = (k_all * lax.rsqrt(_head_mean_square(k_all, pool) + EPS) * kg_ref[...]).astype(BF16)
    v_t = jnp.concatenate([vh_ref[...], v_ref[...]], axis=0).astype(F32).T.astype(BF16)

    lane_head = lax.broadcasted_iota(jnp.int32, (BLOCK, grp), 1) // d
    head_masks = [(lane_head == g).astype(BF16) for g in range(AT_GROUP)]
    pre_seq = jnp.where(first_block, NEG_INF, 0.0)

    scale = d ** -0.5
    k_reps, q_ns = [], []
    for kv in range(AT_KV_HEADS):
        spread = ((sub // d == kv) & (sub % d == lane % d)).astype(BF16)
        k_reps.append(jnp.dot(k_n, spread, preferred_element_type=F32).astype(BF16))
        q_g = q_ref[:, kv * grp:(kv + 1) * grp].astype(F32)
        q_ns.append((q_g * lax.rsqrt(_head_mean_square(q_g, pool) + EPS) * (qg_ref[...] * scale)).astype(BF16))
    yield

    def logits_t(kv, nb):
        rows = slice(nb * BLOCK, (nb + 1) * BLOCK)
        keys = slice(nb * BLOCK, (nb + 2) * BLOCK)
        q4 = jnp.concatenate([q_ns[kv][rows] * head_masks[g] for g in range(AT_GROUP)], axis=0)
        lg = lax.dot_general(k_reps[kv][keys], q4, (((1,), (1,)), ((), ())),
                             preferred_element_type=F32) + bias_ref[kv]
        if nb == 0:
            lg = jnp.concatenate([lg[:BLOCK] + pre_seq, lg[BLOCK:]], axis=0)
        return lg

    units = [(kv, nb) for kv in range(AT_KV_HEADS) for nb in range(n_blocks)]
    lg_next = logits_t(*units[0])
    yield
    for idx, (kv, nb) in enumerate(units):
        lg = lg_next
        if idx + 1 < len(units):
            lg_next = logits_t(*units[idx + 1])
            yield
        rows = slice(nb * BLOCK, (nb + 1) * BLOCK)
        keys = slice(nb * BLOCK, (nb + 2) * BLOCK)
        sink = jnp.concatenate(
            [jnp.full((1, BLOCK), sink_ref[kv * AT_GROUP + g], F32) for g in range(AT_GROUP)], axis=1)
        m = jnp.maximum(jnp.max(lg, axis=0, keepdims=True), sink)
        e = jnp.exp(lg - m)
        den = jnp.sum(e, axis=0, keepdims=True) + jnp.exp(sink - m)
        o_t = jnp.dot(v_t[kv * d:(kv + 1) * d, keys], e.astype(BF16),
                      preferred_element_type=F32) * (1.0 / den)
        o = jnp.concatenate([o_t[:, g * BLOCK:(g + 1) * BLOCK] for g in range(AT_GROUP)], axis=0).T
        o_ref[rows, kv * grp:(kv + 1) * grp] = o.astype(o_ref.dtype)
        yield


MERGE_SPLIT = 2


def _merge_kernel(x_ref, ohg_ref, oat_ref, ghg_ref, gat_ref, wbh_ref, wba_ref, wo_ref,
                  gate1_ref, n2g_ref, shift2_ref, scale2_ref, x1_ref, h2_ref):
    part = x_ref.shape[0] // MERGE_SPLIT
    parts = [slice(p * part, (p + 1) * part) for p in range(MERGE_SPLIT)]
    m_hg = [jnp.dot(ohg_ref[rows, :], wbh_ref[...], preferred_element_type=F32) for rows in parts]
    m_at = [jnp.dot(oat_ref[rows, :], wba_ref[...], preferred_element_type=F32) for rows in parts]
    ys = []
    for p, rows in enumerate(parts):
        merged = (jax.nn.sigmoid(ghg_ref[rows, :].astype(F32)) * m_hg[p]
                  + jax.nn.sigmoid(gat_ref[rows, :].astype(F32)) * m_at[p])
        ys.append(jnp.dot(merged.astype(BF16), wo_ref[...], preferred_element_type=F32))
    for p, rows in enumerate(parts):
        x1 = x_ref[rows, :] + gate1_ref[...] * ys[p]
        x1_ref[rows, :] = x1
        h2_ref[rows, :] = _rms_modulate(x1, n2g_ref[...], scale2_ref[...], shift2_ref[...]).astype(h2_ref.dtype)


def _merge(x, o_hg, o_at, gates, w_bh, w_ba, w_o, ada4, norm2_g):
    bsz, seq, d = x.shape
    tm = 512
    resident = lambda shape: pl.BlockSpec(shape, lambda b, i: (0, 0), pipeline_mode=pl.Buffered(1))
    ada_row = lambda k: pl.BlockSpec((None, None, 1, d), lambda b, i: (b, k, 0, 0))
    rows = lambda w, colblk: pl.BlockSpec((None, tm, w), lambda b, i: (b, i, colblk))
    return pl.pallas_call(
        _merge_kernel,
        grid=(bsz, seq // tm),
        in_specs=[
            rows(d, 0), rows(HG_WIDTH, 0), rows(AT_WIDTH, 0), rows(d, 0), rows(d, 1),
            resident(w_bh.shape), resident(w_ba.shape), resident(w_o.shape),
            ada_row(2), pl.BlockSpec((1, d), lambda b, i: (0, 0)), ada_row(3), ada_row(4),
        ],
        out_specs=[rows(d, 0), rows(d, 0)],
        out_shape=[jax.ShapeDtypeStruct((bsz, seq, d), F32), jax.ShapeDtypeStruct((bsz, seq, d), BF16)],
        compiler_params=_params(2),
        name="merge",
    )(x, o_hg, o_at, gates, gates, w_bh, w_ba, w_o, ada4, norm2_g, ada4, ada4)


FFN_TF = 1024


def _ffn_kernel(h2_ref, x1_ref, gate2_ref, w1_hbm, w2_hbm, o_ref, w1_buf, w2_buf, sem):
    n_tiles, _, tf = w1_hbm.shape
    step = pl.program_id(0) * pl.num_programs(1) + pl.program_id(1)
    n_steps = pl.num_programs(0) * pl.num_programs(1)

    def tile_copies(t, slot):
        return (pltpu.make_async_copy(w1_hbm.at[t], w1_buf.at[slot], sem.at[0, slot]),
                pltpu.make_async_copy(w2_hbm.at[pl.ds(t * tf, tf), :], w2_buf.at[slot], sem.at[1, slot]))

    def start(t, slot):
        for copy in tile_copies(t, slot):
            copy.start()

    @pl.when(step == 0)
    def _():
        start(0, 0)

    assert n_tiles % 2 == 0
    for t in range(n_tiles):
        slot = t % 2
        if t + 1 < n_tiles:
            start(t + 1, 1 - slot)
        else:
            @pl.when(step + 1 < n_steps)
            def _():
                start(0, 1 - slot)
        for copy in tile_copies(t, slot):
            copy.wait()
        hid = jnp.maximum(jnp.dot(h2_ref[...], w1_buf[slot], preferred_element_type=F32), 0.0)
        down = jnp.dot((hid * hid).astype(BF16), w2_buf[slot], preferred_element_type=F32)
        if t == 0:
            o_ref[...] = down
        else:
            o_ref[...] += down
    o_ref[...] = x1_ref[...] + gate2_ref[...] * o_ref[...]


def _ffn(h2, x1, w1_tiles, w2, ada4):
    bsz, seq, d = x1.shape
    tm, tf = 512, w1_tiles.shape[2]
    rows = lambda: pl.BlockSpec((None, tm, d), lambda b, i: (b, i, 0))
    return pl.pallas_call(
        _ffn_kernel,
        grid=(bsz, seq // tm),
        in_specs=[
            rows(), rows(),
            pl.BlockSpec((None, None, 1, d), lambda b, i: (b, 5, 0, 0)),
            pl.BlockSpec(memory_space=pl.ANY), pl.BlockSpec(memory_space=pl.ANY),
        ],
        out_specs=rows(),
        out_shape=jax.ShapeDtypeStruct((bsz, seq, d), F32),
        scratch_shapes=[pltpu.VMEM((2, d, tf), BF16), pltpu.VMEM((2, tf, d), BF16),
                        pltpu.SemaphoreType.DMA((2, 2))],
        compiler_params=_params(2),
        name="ffn",
    )(h2, x1, ada4, w1_tiles, w2)


def kernel(x, c, w_ada, b_ada, norm1_g, norm2_g, w_in, hg_lb_logits, hg_out_norm_g, q_norm_g, k_norm_g,
           attn_sinks, rel_bias_table, w_branch_hg, w_branch_attn, w_out, w_ff1, w_ff2):
    assert w_ada.shape[0] == 1, "single-layer block"
    bsz = x.shape[0]
    ada = _ada(c, w_ada, b_ada)
    ada4 = ada.reshape(bsz, 6, 1, D_MODEL)
    w_in_bf16 = w_in[0].astype(BF16)
    proj, (w_bh, w_ba, w_o, w1, w2) = _inproj(
        x, norm1_g, ada4, w_in_bf16, GATE_OFF,
        ((w_branch_hg, None), (w_branch_attn, None), (w_out, None), (w_ff1, FFN_TF), (w_ff2, None)))
    bias = _attn_bias(rel_bias_table)
    gates, o_hg, o_at = _mixer(x, norm1_g, ada4, w_in_bf16, proj, hg_lb_logits, hg_out_norm_g,
                               bias, attn_sinks[0], q_norm_g, k_norm_g)
    x1, h2 = _merge(x, o_hg, o_at, gates, w_bh, w_ba, w_o, ada4, norm2_g)
    return _ffn(h2, x1, w1, w2, ada4)
```

```python
import functools
import math

import jax
import jax.numpy as jnp
from jax import lax
from jax.experimental import pallas as pl
from jax.experimental.pallas import tpu as pltpu

F32 = jnp.float32
BF16 = jnp.bfloat16

D_MODEL = 2048
HG_HEADS = 8
HG_DK = 128
HG_DV = 128
HG_WIDTH = HG_HEADS * HG_DK
HG_CHUNK = 64
AT_HEADS = 16
AT_KV_HEADS = 4
AT_HEAD_DIM = 64
AT_GROUP = AT_HEADS // AT_KV_HEADS
AT_WIDTH = AT_HEADS * AT_HEAD_DIM
KV_WIDTH = AT_KV_HEADS * AT_HEAD_DIM
WINDOW = 128
BLOCK = 128
N_BUCKETS = 32
MAX_EXACT = N_BUCKETS // 2
MAX_DISTANCE = 128
D_FF = 4 * D_MODEL
EPS = 1e-6
NEG_INF = -1e30

HG_OFF = 0
AT_OFF = 4 * HG_WIDTH
GATE_OFF = AT_OFF + AT_WIDTH + 2 * KV_WIDTH
IN_WIDTH = GATE_OFF + 2 * D_MODEL
MXU_COLS = 256

V7X_VMEM_LIMIT_BYTES = 60 * 1024 * 1024
BF16_SUBLANES = 16


def _params(n_axes):
    return pltpu.CompilerParams(
        dimension_semantics=("arbitrary",) * n_axes,
        vmem_limit_bytes=V7X_VMEM_LIMIT_BYTES,
    )


def _rms_modulate(x, g, scale, shift):
    r = lax.rsqrt(jnp.mean(x * x, axis=-1, keepdims=True) + EPS)
    return (x * r * g) * (1.0 + scale) + shift


def _ada_kernel(c_ref, w_ref, b_ref, o_ref):
    c = c_ref[...]
    c_act = c * jax.nn.sigmoid(c)
    o_ref[...] = jnp.dot(c_act, w_ref[...], preferred_element_type=F32,
                         precision=lax.Precision.HIGHEST) + b_ref[...]


def _ada(c, w_ada, b_ada):
    bsz, d = c.shape
    n = w_ada.shape[-1]
    tn = 1024
    return pl.pallas_call(
        _ada_kernel,
        grid=(n // tn,),
        in_specs=[
            pl.BlockSpec((bsz, d), lambda j: (0, 0)),
            pl.BlockSpec((None, d, tn), lambda j: (0, 0, j)),
            pl.BlockSpec((1, tn), lambda j: (0, j)),
        ],
        out_specs=pl.BlockSpec((bsz, tn), lambda j: (0, j)),
        out_shape=jax.ShapeDtypeStruct((bsz, n), F32),
        compiler_params=_params(1),
        name="ada",
    )(c, w_ada, b_ada)


def _inproj_kernel(n_late, x_ref, g_ref, shift_ref, scale_ref, w_ref, *refs):
    late_in, (o_ref, *late_out) = refs[:n_late], refs[n_late:]
    h = _rms_modulate(x_ref[...], g_ref[...], scale_ref[...], shift_ref[...])
    o_ref[...] = jnp.dot(h.astype(BF16), w_ref[...], preferred_element_type=F32).astype(o_ref.dtype)
    for src, dst in zip(late_in, late_out):
        if len(dst.shape) == 2:
            dst[...] = src[...].astype(dst.dtype)
        else:
            ct = dst.shape[2]
            for t in range(dst.shape[0]):
                dst[t] = src[:, t * ct:(t + 1) * ct].astype(dst.dtype)


def _inproj(x, norm_g, ada4, w_bf16, n, late_weights):
    bsz, seq, d = x.shape
    tm, n_col_tiles = 512, 2
    tn = n // n_col_tiles
    assert tn % MXU_COLS == 0
    tiles_per_seq = seq // tm
    n_row_tiles = bsz * tiles_per_seq
    n_steps = n_col_tiles * n_row_tiles
    row = lambda j, i: (i // tiles_per_seq, i % tiles_per_seq)
    late_in_specs, late_out_specs, late_shapes = [], [], []
    for w, col_tile in late_weights:
        _, rows, cols = w.shape
        slab = max(rows // n_steps, BF16_SUBLANES)
        assert rows % slab == 0
        idx = lambda j, i, last=rows // slab - 1: (jnp.minimum(j * n_row_tiles + i, last), 0)
        late_in_specs.append(pl.BlockSpec((None, slab, cols), lambda j, i, idx=idx: (0, *idx(j, i))))
        if col_tile is None:
            late_out_specs.append(pl.BlockSpec((slab, cols), idx))
            late_shapes.append(jax.ShapeDtypeStruct((rows, cols), BF16))
        else:
            late_out_specs.append(pl.BlockSpec((cols // col_tile, slab, col_tile),
                                               lambda j, i, idx=idx: (0, *idx(j, i))))
            late_shapes.append(jax.ShapeDtypeStruct((cols // col_tile, rows, col_tile), BF16))
    outs = pl.pallas_call(
        functools.partial(_inproj_kernel, len(late_weights)),
        grid=(n_col_tiles, n_row_tiles),
        in_specs=[
            pl.BlockSpec((None, tm, d), lambda j, i: (*row(j, i), 0)),
            pl.BlockSpec((1, d), lambda j, i: (0, 0)),
            pl.BlockSpec((None, None, 1, d), lambda j, i: (i // tiles_per_seq, 0, 0, 0)),
            pl.BlockSpec((None, None, 1, d), lambda j, i: (i // tiles_per_seq, 1, 0, 0)),
            pl.BlockSpec((d, tn), lambda j, i: (0, j)),
            *late_in_specs,
        ],
        out_specs=[pl.BlockSpec((None, tm, tn), lambda j, i: (*row(j, i), j)), *late_out_specs],
        out_shape=[jax.ShapeDtypeStruct((bsz, seq, n), BF16), *late_shapes],
        compiler_params=_params(2),
        name="inproj",
    )(x, norm_g, ada4, ada4, w_bf16, *[w for w, _ in late_weights])
    return outs[0], outs[1:]


HG_GROUP = 4
HG_ROWS = HG_GROUP * HG_CHUNK
GATE_CHUNK = 256
GATE_W_BLOCK = 512


def _mixer_kernel(n_wg, x_ref, g_ref, shift_ref, scale_ref, *refs):
    wg_refs = refs[:n_wg]
    (q_ref, f_ref, i_ref, hg_ref, lbl_ref, gn_ref, *attn_in,
     gates_ref, o_ref, oat_ref, h_ref, state_ref) = refs[n_wg:]
    c, n = HG_CHUNK, HG_ROWS
    heads = range(HG_HEADS)
    lanes = [slice(u * HG_DK, (u + 1) * HG_DK) for u in heads]

    @pl.when(pl.program_id(1) == 0)
    def _():
        state_ref[...] = jnp.zeros_like(state_ref)

    h_ref[...] = _rms_modulate(x_ref[...], g_ref[...], scale_ref[...], shift_ref[...]).astype(BF16)
    wg_cols = wg_refs[0].shape[1]
    pending = list(range(n_wg * wg_cols // GATE_CHUNK))

    def gate_chunks(count):
        for _ in range(min(count, len(pending))):
            k = pending.pop(0)
            w_ref, off = wg_refs[k * GATE_CHUNK // wg_cols], k * GATE_CHUNK % wg_cols
            gates_ref[:, k * GATE_CHUNK:(k + 1) * GATE_CHUNK] = jnp.dot(
                h_ref[...], w_ref[:, off:off + GATE_CHUNK], preferred_element_type=F32).astype(gates_ref.dtype)

    attention = _attention_steps(*attn_in, oat_ref, pl.program_id(1) == 0)
    attend = lambda: next(attention, None)

    def interleave(count):
        for _ in range(count):
            attend()
            gate_chunks(1)

    lbl = lbl_ref[...]
    e = jnp.exp(lbl - jnp.max(lbl, axis=0, keepdims=True))
    lb = e[0:1] / jnp.sum(e, axis=0, keepdims=True)
    gn = gn_ref[...]

    t = lax.broadcasted_iota(jnp.int32, (n, n), 0)
    s = lax.broadcasted_iota(jnp.int32, (n, n), 1)
    tri = ((t // c) == (s // c)) & (s <= t)
    r8 = lax.broadcasted_iota(jnp.int32, (8, n), 0)
    s8 = lax.broadcasted_iota(jnp.int32, (8, n), 1)
    in_chunk = (s8 // c) == (r8 % HG_GROUP)
    stat_rows = in_chunk & ((r8 >= HG_GROUP) | ((s8 % c) < c // 2))
    cum_ops = jnp.concatenate([tri.astype(F32), stat_rows.astype(F32)], axis=0).astype(BF16)
    per_chunk = lambda rows: jnp.concatenate(
        [jnp.broadcast_to(rows[cc:cc + 1], (c, HG_DK)) for cc in range(HG_GROUP)], axis=0)
    col_chunk = lax.broadcasted_iota(jnp.int32, (HG_DV, n), 1) // c
    col_masks = [(col_chunk == cc).astype(BF16) for cc in range(HG_GROUP)]
    zeros_rows = lambda rows: jnp.zeros((rows, HG_DK), BF16)

    gate_chunks(2)
    attend()
    gate_chunks(1)
    ks, qss, hilo = [], [], []
    for u in heads:
        hf = f_ref[:, lanes[u]].astype(F32)
        hq = q_ref[:, lanes[u]].astype(F32)
        f = lb[:, lanes[u]] + (1.0 - lb[:, lanes[u]]) * jax.nn.sigmoid(hf)
        log_f = jnp.log(f)
        ks.append(1.0 - f)
        qss.append(hq * jax.nn.sigmoid(hq))
        hi = log_f.astype(BF16)
        lo = (log_f - hi.astype(F32)).astype(BF16)
        hilo.append(jnp.concatenate([hi, lo], axis=1))
    rs = [jnp.dot(cum_ops, hilo[u], preferred_element_type=F32) for u in heads]
    interleave(2)
    a_s, bms, kks, qes, decs = [], [], [], [], []
    for u in heads:
        r = rs[u][:, :HG_DK] + rs[u][:, HG_DK:]
        b, b_last = r[0:n], r[n + HG_GROUP:n + 2 * HG_GROUP]
        b_rel = b - per_chunk(r[n:n + HG_GROUP])
        a_s.append((qss[u] * jnp.exp(b_rel)).astype(BF16))
        bms.append((ks[u] * jnp.exp(-b_rel)).astype(BF16))
        kks.append((ks[u] * jnp.exp(per_chunk(b_last) - b)).astype(BF16))
        qes.append((qss[u] * jnp.exp(b)).astype(BF16))
        decs.append(jnp.exp(b_last))
    scores = [lax.dot_general(a_s[u], bms[u], (((1,), (1,)), ((), ())), preferred_element_type=F32)
              for u in heads]
    interleave(2)
    vs = [i_ref[:, lanes[u]] for u in heads]
    os = [jnp.dot(jnp.where(tri, scores[u], 0.0).astype(BF16), vs[u], preferred_element_type=F32)
          for u in heads]
    interleave(2)
    upd_ts = []
    for u in heads:
        v_t = vs[u].astype(F32).T.astype(BF16)
        v_t_blocks = jnp.concatenate([v_t * col_masks[cc] for cc in range(HG_GROUP)], axis=0)
        upd_ts.append(jnp.dot(v_t_blocks, kks[u], preferred_element_type=F32))
    interleave(2)
    for u in heads:
        s_t = state_ref[u]
        states = []
        for cc in range(HG_GROUP):
            states.append(s_t.astype(BF16))
            s_t = s_t * decs[u][cc:cc + 1] + upd_ts[u][cc * HG_DV:(cc + 1) * HG_DV]
        state_ref[u] = s_t
        qe = qes[u]
        qe_blocks = jnp.concatenate(
            [jnp.concatenate([p for p in (zeros_rows(cc * c), qe[cc * c:(cc + 1) * c],
                                          zeros_rows(n - (cc + 1) * c)) if p.shape[0]], axis=0)
             for cc in range(HG_GROUP)], axis=1)
        os[u] = os[u] + lax.dot_general(qe_blocks, jnp.concatenate(states, axis=1),
                                        (((1,), (1,)), ((), ())), preferred_element_type=F32)
    interleave(len(pending) - 2)
    for _ in attention:
        pass
    gate_chunks(len(pending))
    for u in heads:
        o = os[u]
        hg = hg_ref[:, lanes[u]].astype(F32)
        y = o * lax.rsqrt(jnp.mean(o * o, axis=-1, keepdims=True) + EPS) * gn
        o_ref[:, lanes[u]] = (y * (hg * jax.nn.sigmoid(hg))).astype(o_ref.dtype)


def _mixer(x, norm_g, ada4, w_in_bf16, proj, lb_logits, out_norm_g, bias, sinks, q_norm_g, k_norm_g):
    bsz, seq, d = x.shape
    n_gate = w_in_bf16.shape[-1] - GATE_OFF
    tm = HG_ROWS
    blocks_per_step = tm // BLOCK
    rows = lambda w, colblk: pl.BlockSpec((None, tm, w), lambda b, i: (b, i, colblk))
    ada_row = lambda k: pl.BlockSpec((None, None, 1, d), lambda b, i: (b, k, 0, 0))
    const = lambda shape: pl.BlockSpec(shape, lambda b, i: (0,) * len(shape))
    n_wg = n_gate // GATE_W_BLOCK
    w_gate = [pl.BlockSpec((d, GATE_W_BLOCK), lambda b, i, k=k: (0, GATE_OFF // GATE_W_BLOCK + k),
                           pipeline_mode=pl.Buffered(1)) for k in range(n_wg)]
    hg_base = HG_OFF // HG_WIDTH
    k_col = (AT_OFF + AT_WIDTH) // KV_WIDTH
    halo = lambda colblk: pl.BlockSpec(
        (None, BLOCK, KV_WIDTH), lambda b, i: (b, jnp.maximum(i * blocks_per_step - 1, 0), colblk))
    qg = jnp.tile(q_norm_g, (1, AT_GROUP))
    kg = jnp.tile(k_norm_g, (1, AT_KV_HEADS))
    return pl.pallas_call(
        functools.partial(_mixer_kernel, n_wg),
        grid=(bsz, seq // tm),
        in_specs=[
            rows(d, 0), const((1, d)), ada_row(0), ada_row(1), *w_gate,
            rows(HG_WIDTH, hg_base), rows(HG_WIDTH, hg_base + 1), rows(HG_WIDTH, hg_base + 2),
            rows(HG_WIDTH, hg_base + 3), const(lb_logits.shape), const((1, HG_DV)),
            rows(AT_WIDTH, AT_OFF // AT_WIDTH), rows(KV_WIDTH, k_col), rows(KV_WIDTH, k_col + 1),
            halo(k_col), halo(k_col + 1),
            pl.BlockSpec(bias.shape, lambda b, i: (0, 0, 0), pipeline_mode=pl.Buffered(1)),
            pl.BlockSpec(memory_space=pltpu.SMEM), const((1, AT_GROUP * AT_HEAD_DIM)), const((1, KV_WIDTH)),
        ],
        out_specs=[rows(n_gate, 0), rows(HG_WIDTH, 0), rows(AT_WIDTH, 0)],
        out_shape=[jax.ShapeDtypeStruct((bsz, seq, n_gate), BF16),
                   jax.ShapeDtypeStruct((bsz, seq, HG_WIDTH), BF16),
                   jax.ShapeDtypeStruct((bsz, seq, AT_WIDTH), BF16)],
        scratch_shapes=[pltpu.VMEM((tm, d), BF16), pltpu.VMEM((HG_HEADS, HG_DV, HG_DK), F32)],
        compiler_params=_params(2),
        name="mixer",
    )(x, norm_g, ada4, ada4, *([w_in_bf16] * n_wg), proj, proj, proj, proj, lb_logits, out_norm_g,
      proj, proj, proj, proj, proj, bias, sinks, qg, kg)


def _t5_causal_bucket(n):
    nf = jnp.maximum(n, 1).astype(F32)
    large = MAX_EXACT + (jnp.log(nf / MAX_EXACT) / math.log(MAX_DISTANCE / MAX_EXACT)
                         * (N_BUCKETS - MAX_EXACT)).astype(jnp.int32)
    large = jnp.minimum(large, N_BUCKETS - 1)
    return jnp.where(n < MAX_EXACT, n, large)


def _bias_kernel(table_ref, bucket_t_ref, o_ref):
    bucket_t = bucket_t_ref[...]
    j = lax.broadcasted_iota(jnp.int32, bucket_t.shape, 0)
    i = lax.broadcasted_iota(jnp.int32, bucket_t.shape, 1)
    dist = i - j + BLOCK
    band = (dist >= 0) & (dist < WINDOW)
    for h in range(AT_HEADS):
        acc = jnp.zeros(bucket_t.shape, F32)
        for r in range(N_BUCKETS):
            acc = jnp.where(bucket_t == r, table_ref[r, h], acc)
        g = h % AT_GROUP
        o_ref[h // AT_GROUP, :, g * BLOCK:(g + 1) * BLOCK] = jnp.where(band, acc, NEG_INF)


def _attn_bias(rel_bias_table):
    i = jnp.arange(BLOCK, dtype=jnp.int32)[None, :]
    j = jnp.arange(2 * BLOCK, dtype=jnp.int32)[:, None]
    bucket_t = _t5_causal_bucket(jnp.maximum(i - j + BLOCK, 0))
    shape = (AT_KV_HEADS, 2 * BLOCK, AT_GROUP * BLOCK)
    return pl.pallas_call(
        _bias_kernel,
        in_specs=[
            pl.BlockSpec(memory_space=pltpu.SMEM),
            pl.BlockSpec((2 * BLOCK, BLOCK), lambda: (0, 0)),
        ],
        out_specs=pl.BlockSpec(shape, lambda: (0, 0, 0)),
        out_shape=jax.ShapeDtypeStruct(shape, F32),
        name="attn_bias",
    )(rel_bias_table, bucket_t)


def _head_mean_square(t, pool):
    return jnp.dot((t * t).astype(BF16), pool, preferred_element_type=F32)


def _attention_steps(q_ref, k_ref, v_ref, kh_ref, vh_ref, bias_ref, sink_ref, qg_ref, kg_ref, o_ref, first_block):
    n_blocks = q_ref.shape[0] // BLOCK
    d = AT_HEAD_DIM
    grp = AT_GROUP * d
    sub = lax.broadcasted_iota(jnp.int32, (grp, grp), 0)
    lane = lax.broadcasted_iota(jnp.int32, (grp, grp), 1)
    pool = jnp.where(sub // d == lane // d, 1.0 / d, 0.0).astype(BF16)

    k_all = jnp.concatenate([kh_ref[...], k_ref[...]], axis=0).astype(F32)
    k_n = (k_all * lax.rsqrt(_head_mean_square(k_all, pool) + EPS) * kg_ref[...]).astype(BF16)
    v_t = jnp.concatenate([vh_ref[...], v_ref[...]], axis=0).astype(F32).T.astype(BF16)

    lane_head = lax.broadcasted_iota(jnp.int32, (BLOCK, grp), 1) // d
    head_masks = [(lane_head == g).astype(BF16) for g in range(AT_GROUP)]
    pre_seq = jnp.where(first_block, NEG_INF, 0.0)

    scale = d ** -0.5
    k_reps, q_ns = [], []
    for kv in range(AT_KV_HEADS):
        spread = ((sub // d == kv) & (sub % d == lane % d)).astype(BF16)
        k_reps.append(jnp.dot(k_n, spread, preferred_element_type=F32).astype(BF16))
        q_g = q_ref[:, kv * grp:(kv + 1) * grp].astype(F32)
        q_ns.append((q_g * lax.rsqrt(_head_mean_square(q_g, pool) + EPS) * (qg_ref[...] * scale)).astype(BF16))
    yield

    def logits_t(kv, nb):
        rows = slice(nb * BLOCK, (nb + 1) * BLOCK)
        keys = slice(nb * BLOCK, (nb + 2) * BLOCK)
        q4 = jnp.concatenate([q_ns[kv][rows] * head_masks[g] for g in range(AT_GROUP)], axis=0)
        lg = lax.dot_general(k_reps[kv][keys], q4, (((1,), (1,)), ((), ())),
                             preferred_element_type=F32) + bias_ref[kv]
        if nb == 0:
            lg = jnp.concatenate([lg[:BLOCK] + pre_seq, lg[BLOCK:]], axis=0)
        return lg

    units = [(kv, nb) for kv in range(AT_KV_HEADS) for nb in range(n_blocks)]
    lg_next = logits_t(*units[0])
    yield
    for idx, (kv, nb) in enumerate(units):
        lg = lg_next
        if idx + 1 < len(units):
            lg_next = logits_t(*units[idx + 1])
            yield
        rows = slice(nb * BLOCK, (nb + 1) * BLOCK)
        keys = slice(nb * BLOCK, (nb + 2) * BLOCK)
        sink = jnp.concatenate(
            [jnp.full((1, BLOCK), sink_ref[kv * AT_GROUP + g], F32) for g in range(AT_GROUP)], axis=1)
        m = jnp.maximum(jnp.max(lg, axis=0, keepdims=True), sink)
        e = jnp.exp(lg - m)
        den = jnp.sum(e, axis=0, keepdims=True) + jnp.exp(sink - m)
        o_t = jnp.dot(v_t[kv * d:(kv + 1) * d, keys], e.astype(BF16),
                      preferred_element_type=F32) * (1.0 / den)
        o = jnp.concatenate([o_t[:, g * BLOCK:(g + 1) * BLOCK] for g in range(AT_GROUP)], axis=0).T
        o_ref[rows, kv * grp:(kv + 1) * grp] = o.astype(o_ref.dtype)
        yield


MERGE_SPLIT = 2


def _merge_kernel(x_ref, ohg_ref, oat_ref, ghg_ref, gat_ref, wbh_ref, wba_ref, wo_ref,
                  gate1_ref, n2g_ref, shift2_ref, scale2_ref, x1_ref, h2_ref):
    part = x_ref.shape[0] // MERGE_SPLIT
    parts = [slice(p * part, (p + 1) * part) for p in range(MERGE_SPLIT)]
    m_hg = [jnp.dot(ohg_ref[rows, :], wbh_ref[...], preferred_element_type=F32) for rows in parts]
    m_at = [jnp.dot(oat_ref[rows, :], wba_ref[...], preferred_element_type=F32) for rows in parts]
    ys = []
    for p, rows in enumerate(parts):
        merged = (jax.nn.sigmoid(ghg_ref[rows, :].astype(F32)) * m_hg[p]
                  + jax.nn.sigmoid(gat_ref[rows, :].astype(F32)) * m_at[p])
        ys.append(jnp.dot(merged.astype(BF16), wo_ref[...], preferred_element_type=F32))
    for p, rows in enumerate(parts):
        x1 = x_ref[rows, :] + gate1_ref[...] * ys[p]
        x1_ref[rows, :] = x1
        h2_ref[rows, :] = _rms_modulate(x1, n2g_ref[...], scale2_ref[...], shift2_ref[...]).astype(h2_ref.dtype)


def _merge(x, o_hg, o_at, gates, w_bh, w_ba, w_o, ada4, norm2_g):
    bsz, seq, d = x.shape
    tm = 512
    resident = lambda shape: pl.BlockSpec(shape, lambda b, i: (0, 0), pipeline_mode=pl.Buffered(1))
    ada_row = lambda k: pl.BlockSpec((None, None, 1, d), lambda b, i: (b, k, 0, 0))
    rows = lambda w, colblk: pl.BlockSpec((None, tm, w), lambda b, i: (b, i, colblk))
    return pl.pallas_call(
        _merge_kernel,
        grid=(bsz, seq // tm),
        in_specs=[
            rows(d, 0), rows(HG_WIDTH, 0), rows(AT_WIDTH, 0), rows(d, 0), rows(d, 1),
            resident(w_bh.shape), resident(w_ba.shape), resident(w_o.shape),
            ada_row(2), pl.BlockSpec((1, d), lambda b, i: (0, 0)), ada_row(3), ada_row(4),
        ],
        out_specs=[rows(d, 0), rows(d, 0)],
        out_shape=[jax.ShapeDtypeStruct((bsz, seq, d), F32), jax.ShapeDtypeStruct((bsz, seq, d), BF16)],
        compiler_params=_params(2),
        name="merge",
    )(x, o_hg, o_at, gates, gates, w_bh, w_ba, w_o, ada4, norm2_g, ada4, ada4)


FFN_TF = 512


def _ffn_kernel(h2_ref, x1_ref, gate2_ref, w1_hbm, w2_hbm, o_ref, w1_buf, w2_buf, sem):
    n_tiles, _, tf = w1_hbm.shape
    step = pl.program_id(0) * pl.num_programs(1) + pl.program_id(1)
    n_steps = pl.num_programs(0) * pl.num_programs(1)

    def tile_copies(t, slot):
        return (pltpu.make_async_copy(w1_hbm.at[t], w1_buf.at[slot], sem.at[0, slot]),
                pltpu.make_async_copy(w2_hbm.at[pl.ds(t * tf, tf), :], w2_buf.at[slot], sem.at[1, slot]))

    def start(t, slot):
        for copy in tile_copies(t, slot):
            copy.start()

    @pl.when(step == 0)
    def _():
        start(0, 0)

    assert n_tiles % 2 == 0
    for t in range(n_tiles):
        slot = t % 2
        if t + 1 < n_tiles:
            start(t + 1, 1 - slot)
        else:
            @pl.when(step + 1 < n_steps)
            def _():
                start(0, 1 - slot)
        for copy in tile_copies(t, slot):
            copy.wait()
        hid = jnp.maximum(jnp.dot(h2_ref[...], w1_buf[slot], preferred_element_type=F32), 0.0)
        down = jnp.dot((hid * hid).astype(BF16), w2_buf[slot], preferred_element_type=F32)
        if t == 0:
            o_ref[...] = down
        else:
            o_ref[...] += down
    o_ref[...] = x1_ref[...] + gate2_ref[...] * o_ref[...]


def _ffn(h2, x1, w1_tiles, w2, ada4):
    bsz, seq, d = x1.shape
    tm, tf = 1024, w1_tiles.shape[2]
    rows = lambda: pl.BlockSpec((None, tm, d), lambda b, i: (b, i, 0))
    return pl.pallas_call(
        _ffn_kernel,
        grid=(bsz, seq // tm),
        in_specs=[
            rows(), rows(),
            pl.BlockSpec((None, None, 1, d), lambda b, i: (b, 5, 0, 0)),
            pl.BlockSpec(memory_space=pl.ANY), pl.BlockSpec(memory_space=pl.ANY),
        ],
        out_specs=rows(),
        out_shape=jax.ShapeDtypeStruct((bsz, seq, d), F32),
        scratch_shapes=[pltpu.VMEM((2, d, tf), BF16), pltpu.VMEM((2, tf, d), BF16),
                        pltpu.SemaphoreType.DMA((2, 2))],
        compiler_params=_params(2),
        name="ffn",
    )(h2, x1, ada4, w1_tiles, w2)


def kernel(x, c, w_ada, b_ada, norm1_g, norm2_g, w_in, hg_lb_logits, hg_out_norm_g, q_norm_g, k_norm_g,
           attn_sinks, rel_bias_table, w_branch_hg, w_branch_attn, w_out, w_ff1, w_ff2):
    assert w_ada.shape[0] == 1, "single-layer block"
    bsz = x.shape[0]
    ada = _ada(c, w_ada, b_ada)
    ada4 = ada.reshape(bsz, 6, 1, D_MODEL)
    w_in_bf16 = w_in[0].astype(BF16)
    proj, (w_bh, w_ba, w_o, w1, w2) = _inproj(
        x, norm1_g, ada4, w_in_bf16, GATE_OFF,
        ((w_branch_hg, None), (w_branch_attn, None), (w_out, None), (w_ff1, FFN_TF), (w_ff2, None)))
    bias = _attn_bias(rel_bias_table)
    gates, o_hg, o_at = _mixer(x, norm1_g, ada4, w_in_bf16, proj, hg_lb_logits, hg_out_norm_g,
                               bias, attn_sinks[0], q_norm_g, k_norm_g)
    x1, h2 = _merge(x, o_hg, o_at, gates, w_bh, w_ba, w_o, ada4, norm2_g)
    return _ffn(h2, x1, w1, w2, ada4)
```

```python
import functools
import math

import jax
import jax.numpy as jnp
from jax import lax
from jax.experimental import pallas as pl
from jax.experimental.pallas import tpu as pltpu

F32 = jnp.float32
BF16 = jnp.bfloat16

D_MODEL = 2048
HG_HEADS = 8
HG_DK = 128
HG_DV = 128
HG_WIDTH = HG_HEADS * HG_DK
HG_CHUNK = 64
AT_HEADS = 16
AT_KV_HEADS = 4
AT_HEAD_DIM = 64
AT_GROUP = AT_HEADS // AT_KV_HEADS
AT_WIDTH = AT_HEADS * AT_HEAD_DIM
KV_WIDTH = AT_KV_HEADS * AT_HEAD_DIM
WINDOW = 128
BLOCK = 128
N_BUCKETS = 32
MAX_EXACT = N_BUCKETS // 2
MAX_DISTANCE = 128
D_FF = 4 * D_MODEL
EPS = 1e-6
NEG_INF = -1e30

HG_OFF = 0
AT_OFF = 4 * HG_WIDTH
GATE_OFF = AT_OFF + AT_WIDTH + 2 * KV_WIDTH
IN_WIDTH = GATE_OFF + 2 * D_MODEL
MXU_COLS = 256

V7X_VMEM_LIMIT_BYTES = 60 * 1024 * 1024
BF16_SUBLANES = 16


def _params(n_axes):
    return pltpu.CompilerParams(
        dimension_semantics=("arbitrary",) * n_axes,
        vmem_limit_bytes=V7X_VMEM_LIMIT_BYTES,
    )


def _rms_modulate(x, g, scale, shift):
    r = lax.rsqrt(jnp.mean(x * x, axis=-1, keepdims=True) + EPS)
    return (x * r * g) * (1.0 + scale) + shift


def _ada_kernel(c_ref, w_ref, b_ref, o_ref):
    c = c_ref[...]
    c_act = c * jax.nn.sigmoid(c)
    o_ref[...] = jnp.dot(c_act, w_ref[...], preferred_element_type=F32,
                         precision=lax.Precision.HIGHEST) + b_ref[...]


def _ada(c, w_ada, b_ada):
    bsz, d = c.shape
    n = w_ada.shape[-1]
    tn = 1024
    return pl.pallas_call(
        _ada_kernel,
        grid=(n // tn,),
        in_specs=[
            pl.BlockSpec((bsz, d), lambda j: (0, 0)),
            pl.BlockSpec((None, d, tn), lambda j: (0, 0, j)),
            pl.BlockSpec((1, tn), lambda j: (0, j)),
        ],
        out_specs=pl.BlockSpec((bsz, tn), lambda j: (0, j)),
        out_shape=jax.ShapeDtypeStruct((bsz, n), F32),
        compiler_params=_params(1),
        name="ada",
    )(c, w_ada, b_ada)


def _inproj_kernel(n_late, x_ref, g_ref, shift_ref, scale_ref, w_ref, *refs):
    late_in, (o_ref, *late_out) = refs[:n_late], refs[n_late:]
    h = _rms_modulate(x_ref[...], g_ref[...], scale_ref[...], shift_ref[...])
    o_ref[...] = jnp.dot(h.astype(BF16), w_ref[...], preferred_element_type=F32).astype(o_ref.dtype)
    for src, dst in zip(late_in, late_out):
        dst[...] = src[...].astype(dst.dtype)


def _inproj(x, norm_g, ada4, w_bf16, n, late_weights):
    bsz, seq, d = x.shape
    tm, n_col_tiles = 512, 2
    tn = n // n_col_tiles
    assert tn % MXU_COLS == 0
    tiles_per_seq = seq // tm
    n_row_tiles = bsz * tiles_per_seq
    n_steps = n_col_tiles * n_row_tiles
    row = lambda j, i: (i // tiles_per_seq, i % tiles_per_seq)
    late_in_specs, late_out_specs, late_shapes = [], [], []
    for w in late_weights:
        _, rows, cols = w.shape
        slab = max(rows // n_steps, BF16_SUBLANES)
        assert rows % slab == 0
        idx = lambda j, i, last=rows // slab - 1: (jnp.minimum(j * n_row_tiles + i, last), 0)
        late_in_specs.append(pl.BlockSpec((None, slab, cols), lambda j, i, idx=idx: (0, *idx(j, i))))
        late_out_specs.append(pl.BlockSpec((slab, cols), idx))
        late_shapes.append(jax.ShapeDtypeStruct((rows, cols), BF16))
    outs = pl.pallas_call(
        functools.partial(_inproj_kernel, len(late_weights)),
        grid=(n_col_tiles, n_row_tiles),
        in_specs=[
            pl.BlockSpec((None, tm, d), lambda j, i: (*row(j, i), 0)),
            pl.BlockSpec((1, d), lambda j, i: (0, 0)),
            pl.BlockSpec((None, None, 1, d), lambda j, i: (i // tiles_per_seq, 0, 0, 0)),
            pl.BlockSpec((None, None, 1, d), lambda j, i: (i // tiles_per_seq, 1, 0, 0)),
            pl.BlockSpec((d, tn), lambda j, i: (0, j)),
            *late_in_specs,
        ],
        out_specs=[pl.BlockSpec((None, tm, tn), lambda j, i: (*row(j, i), j)), *late_out_specs],
        out_shape=[jax.ShapeDtypeStruct((bsz, seq, n), BF16), *late_shapes],
        compiler_params=_params(2),
        name="inproj",
    )(x, norm_g, ada4, ada4, w_bf16, *late_weights)
    return outs[0], outs[1:]


HG_GROUP = 4
HG_ROWS = HG_GROUP * HG_CHUNK
GATE_CHUNK = 256
GATE_W_BLOCK = 512


def _mixer_kernel(n_wg, x_ref, g_ref, shift_ref, scale_ref, *refs):
    wg_refs = refs[:n_wg]
    (q_ref, f_ref, i_ref, hg_ref, lbl_ref, gn_ref, *attn_in,
     gates_ref, o_ref, oat_ref, h_ref, state_ref) = refs[n_wg:]
    c, n = HG_CHUNK, HG_ROWS
    heads = range(HG_HEADS)
    lanes = [slice(u * HG_DK, (u + 1) * HG_DK) for u in heads]

    @pl.when(pl.program_id(1) == 0)
    def _():
        state_ref[...] = jnp.zeros_like(state_ref)

    h_ref[...] = _rms_modulate(x_ref[...], g_ref[...], scale_ref[...], shift_ref[...]).astype(BF16)
    wg_cols = wg_refs[0].shape[1]
    pending = list(range(n_wg * wg_cols // GATE_CHUNK))

    def gate_chunks(count):
        for _ in range(min(count, len(pending))):
            k = pending.pop(0)
            w_ref, off = wg_refs[k * GATE_CHUNK // wg_cols], k * GATE_CHUNK % wg_cols
            gates_ref[:, k * GATE_CHUNK:(k + 1) * GATE_CHUNK] = jnp.dot(
                h_ref[...], w_ref[:, off:off + GATE_CHUNK], preferred_element_type=F32).astype(gates_ref.dtype)

    attention = _attention_steps(*attn_in, oat_ref, pl.program_id(1) == 0)
    attend = lambda: next(attention, None)

    def interleave(count):
        for _ in range(count):
            attend()
            gate_chunks(1)

    lbl = lbl_ref[...]
    e = jnp.exp(lbl - jnp.max(lbl, axis=0, keepdims=True))
    lb = e[0:1] / jnp.sum(e, axis=0, keepdims=True)
    gn = gn_ref[...]

    t = lax.broadcasted_iota(jnp.int32, (n, n), 0)
    s = lax.broadcasted_iota(jnp.int32, (n, n), 1)
    tri = ((t // c) == (s // c)) & (s <= t)
    r8 = lax.broadcasted_iota(jnp.int32, (8, n), 0)
    s8 = lax.broadcasted_iota(jnp.int32, (8, n), 1)
    in_chunk = (s8 // c) == (r8 % HG_GROUP)
    stat_rows = in_chunk & ((r8 >= HG_GROUP) | ((s8 % c) < c // 2))
    cum_ops = jnp.concatenate([tri.astype(F32), stat_rows.astype(F32)], axis=0).astype(BF16)
    per_chunk = lambda rows: jnp.concatenate(
        [jnp.broadcast_to(rows[cc:cc + 1], (c, HG_DK)) for cc in range(HG_GROUP)], axis=0)
    col_chunk = lax.broadcasted_iota(jnp.int32, (HG_DV, n), 1) // c
    col_masks = [(col_chunk == cc).astype(BF16) for cc in range(HG_GROUP)]
    zeros_rows = lambda rows: jnp.zeros((rows, HG_DK), BF16)

    gate_chunks(2)
    attend()
    gate_chunks(1)
    ks, qss, hilo = [], [], []
    for u in heads:
        hf = f_ref[:, lanes[u]].astype(F32)
        hq = q_ref[:, lanes[u]].astype(F32)
        f = lb[:, lanes[u]] + (1.0 - lb[:, lanes[u]]) * jax.nn.sigmoid(hf)
        log_f = jnp.log(f)
        ks.append(1.0 - f)
        qss.append(hq * jax.nn.sigmoid(hq))
        hi = log_f.astype(BF16)
        lo = (log_f - hi.astype(F32)).astype(BF16)
        hilo.append(jnp.concatenate([hi, lo], axis=1))
    rs = [jnp.dot(cum_ops, hilo[u], preferred_element_type=F32) for u in heads]
    interleave(2)
    a_s, bms, kks, qes, decs = [], [], [], [], []
    for u in heads:
        r = rs[u][:, :HG_DK] + rs[u][:, HG_DK:]
        b, b_last = r[0:n], r[n + HG_GROUP:n + 2 * HG_GROUP]
        b_rel = b - per_chunk(r[n:n + HG_GROUP])
        a_s.append((qss[u] * jnp.exp(b_rel)).astype(BF16))
        bms.append((ks[u] * jnp.exp(-b_rel)).astype(BF16))
        kks.append((ks[u] * jnp.exp(per_chunk(b_last) - b)).astype(BF16))
        qes.append((qss[u] * jnp.exp(b)).astype(BF16))
        decs.append(jnp.exp(b_last))
    scores = [lax.dot_general(a_s[u], bms[u], (((1,), (1,)), ((), ())), preferred_element_type=F32)
              for u in heads]
    interleave(2)
    vs = [i_ref[:, lanes[u]] for u in heads]
    os = [jnp.dot(jnp.where(tri, scores[u], 0.0).astype(BF16), vs[u], preferred_element_type=F32)
          for u in heads]
    interleave(2)
    upd_ts = []
    for u in heads:
        v_t = vs[u].astype(F32).T.astype(BF16)
        v_t_blocks = jnp.concatenate([v_t * col_masks[cc] for cc in range(HG_GROUP)], axis=0)
        upd_ts.append(jnp.dot(v_t_blocks, kks[u], preferred_element_type=F32))
    interleave(2)
    for u in heads:
        s_t = state_ref[u]
        states = []
        for cc in range(HG_GROUP):
            states.append(s_t.astype(BF16))
            s_t = s_t * decs[u][cc:cc + 1] + upd_ts[u][cc * HG_DV:(cc + 1) * HG_DV]
        state_ref[u] = s_t
        qe = qes[u]
        qe_blocks = jnp.concatenate(
            [jnp.concatenate([p for p in (zeros_rows(cc * c), qe[cc * c:(cc + 1) * c],
                                          zeros_rows(n - (cc + 1) * c)) if p.shape[0]], axis=0)
             for cc in range(HG_GROUP)], axis=1)
        os[u] = os[u] + lax.dot_general(qe_blocks, jnp.concatenate(states, axis=1),
                                        (((1,), (1,)), ((), ())), preferred_element_type=F32)
    interleave(len(pending) - 2)
    for _ in attention:
        pass
    gate_chunks(len(pending))
    for u in heads:
        o = os[u]
        hg = hg_ref[:, lanes[u]].astype(F32)
        y = o * lax.rsqrt(jnp.mean(o * o, axis=-1, keepdims=True) + EPS) * gn
        o_ref[:, lanes[u]] = (y * (hg * jax.nn.sigmoid(hg))).astype(o_ref.dtype)


def _mixer(x, norm_g, ada4, w_in_bf16, proj, lb_logits, out_norm_g, bias, sinks, q_norm_g, k_norm_g):
    bsz, seq, d = x.shape
    n_gate = w_in_bf16.shape[-1] - GATE_OFF
    tm = HG_ROWS
    blocks_per_step = tm // BLOCK
    rows = lambda w, colblk: pl.BlockSpec((None, tm, w), lambda b, i: (b, i, colblk))
    ada_row = lambda k: pl.BlockSpec((None, None, 1, d), lambda b, i: (b, k, 0, 0))
    const = lambda shape: pl.BlockSpec(shape, lambda b, i: (0,) * len(shape))
    n_wg = n_gate // GATE_W_BLOCK
    w_gate = [pl.BlockSpec((d, GATE_W_BLOCK), lambda b, i, k=k: (0, GATE_OFF // GATE_W_BLOCK + k),
                           pipeline_mode=pl.Buffered(1)) for k in range(n_wg)]
    hg_base = HG_OFF // HG_WIDTH
    k_col = (AT_OFF + AT_WIDTH) // KV_WIDTH
    halo = lambda colblk: pl.BlockSpec(
        (None, BLOCK, KV_WIDTH), lambda b, i: (b, jnp.maximum(i * blocks_per_step - 1, 0), colblk))
    qg = jnp.tile(q_norm_g, (1, AT_GROUP))
    kg = jnp.tile(k_norm_g, (1, AT_KV_HEADS))
    return pl.pallas_call(
        functools.partial(_mixer_kernel, n_wg),
        grid=(bsz, seq // tm),
        in_specs=[
            rows(d, 0), const((1, d)), ada_row(0), ada_row(1), *w_gate,
            rows(HG_WIDTH, hg_base), rows(HG_WIDTH, hg_base + 1), rows(HG_WIDTH, hg_base + 2),
            rows(HG_WIDTH, hg_base + 3), const(lb_logits.shape), const((1, HG_DV)),
            rows(AT_WIDTH, AT_OFF // AT_WIDTH), rows(KV_WIDTH, k_col), rows(KV_WIDTH, k_col + 1),
            halo(k_col), halo(k_col + 1),
            pl.BlockSpec(bias.shape, lambda b, i: (0, 0, 0), pipeline_mode=pl.Buffered(1)),
            pl.BlockSpec(memory_space=pltpu.SMEM), const((1, AT_GROUP * AT_HEAD_DIM)), const((1, KV_WIDTH)),
        ],
        out_specs=[rows(n_gate, 0), rows(HG_WIDTH, 0), rows(AT_WIDTH, 0)],
        out_shape=[jax.ShapeDtypeStruct((bsz, seq, n_gate), BF16),
                   jax.ShapeDtypeStruct((bsz, seq, HG_WIDTH), BF16),
                   jax.ShapeDtypeStruct((bsz, seq, AT_WIDTH), BF16)],
        scratch_shapes=[pltpu.VMEM((tm, d), BF16), pltpu.VMEM((HG_HEADS, HG_DV, HG_DK), F32)],
        compiler_params=_params(2),
        name="mixer",
    )(x, norm_g, ada4, ada4, *([w_in_bf16] * n_wg), proj, proj, proj, proj, lb_logits, out_norm_g,
      proj, proj, proj, proj, proj, bias, sinks, qg, kg)


def _t5_causal_bucket(n):
    nf = jnp.maximum(n, 1).astype(F32)
    large = MAX_EXACT + (jnp.log(nf / MAX_EXACT) / math.log(MAX_DISTANCE / MAX_EXACT)
                         * (N_BUCKETS - MAX_EXACT)).astype(jnp.int32)
    large = jnp.minimum(large, N_BUCKETS - 1)
    return jnp.where(n < MAX_EXACT, n, large)


def _bias_kernel(table_ref, bucket_t_ref, o_ref):
    bucket_t = bucket_t_ref[...]
    j = lax.broadcasted_iota(jnp.int32, bucket_t.shape, 0)
    i = lax.broadcasted_iota(jnp.int32, bucket_t.shape, 1)
    dist = i - j + BLOCK
    band = (dist >= 0) & (dist < WINDOW)
    for h in range(AT_HEADS):
        acc = jnp.zeros(bucket_t.shape, F32)
        for r in range(N_BUCKETS):
            acc = jnp.where(bucket_t == r, table_ref[r, h], acc)
        g = h % AT_GROUP
        o_ref[h // AT_GROUP, :, g * BLOCK:(g + 1) * BLOCK] = jnp.where(band, acc, NEG_INF)


def _attn_bias(rel_bias_table):
    i = jnp.arange(BLOCK, dtype=jnp.int32)[None, :]
    j = jnp.arange(2 * BLOCK, dtype=jnp.int32)[:, None]
    bucket_t = _t5_causal_bucket(jnp.maximum(i - j + BLOCK, 0))
    shape = (AT_KV_HEADS, 2 * BLOCK, AT_GROUP * BLOCK)
    return pl.pallas_call(
        _bias_kernel,
        in_specs=[
            pl.BlockSpec(memory_space=pltpu.SMEM),
            pl.BlockSpec((2 * BLOCK, BLOCK), lambda: (0, 0)),
        ],
        out_specs=pl.BlockSpec(shape, lambda: (0, 0, 0)),
        out_shape=jax.ShapeDtypeStruct(shape, F32),
        name="attn_bias",
    )(rel_bias_table, bucket_t)


def _head_mean_square(t, pool):
    return jnp.dot((t * t).astype(BF16), pool, preferred_element_type=F32)


def _attention_steps(q_ref, k_ref, v_ref, kh_ref, vh_ref, bias_ref, sink_ref, qg_ref, kg_ref, o_ref, first_block):
    n_blocks = q_ref.shape[0] // BLOCK
    d = AT_HEAD_DIM
    grp = AT_GROUP * d
    sub = lax.broadcasted_iota(jnp.int32, (grp, grp), 0)
    lane = lax.broadcasted_iota(jnp.int32, (grp, grp), 1)
    pool = jnp.where(sub // d == lane // d, 1.0 / d, 0.0).astype(BF16)

    k_all = jnp.concatenate([kh_ref[...], k_ref[...]], axis=0).astype(F32)
    k_n = (k_all * lax.rsqrt(_head_mean_square(k_all, pool) + EPS) * kg_ref[...]).astype(BF16)
    v_t = jnp.concatenate([vh_ref[...], v_ref[...]], axis=0).astype(F32).T.astype(BF16)

    lane_head = lax.broadcasted_iota(jnp.int32, (BLOCK, grp), 1) // d
    head_masks = [(lane_head == g).astype(BF16) for g in range(AT_GROUP)]
    pre_seq = jnp.where(first_block, NEG_INF, 0.0)

    scale = d ** -0.5
    k_reps, q_ns = [], []
    for kv in range(AT_KV_HEADS):
        spread = ((sub // d == kv) & (sub % d == lane % d)).astype(BF16)
        k_reps.append(jnp.dot(k_n, spread, preferred_element_type=F32).astype(BF16))
        q_g = q_ref[:, kv * grp:(kv + 1) * grp].astype(F32)
        q_ns.append((q_g * lax.rsqrt(_head_mean_square(q_g, pool) + EPS) * (qg_ref[...] * scale)).astype(BF16))
    yield

    def logits_t(kv, nb):
        rows = slice(nb * BLOCK, (nb + 1) * BLOCK)
        keys = slice(nb * BLOCK, (nb + 2) * BLOCK)
        q4 = jnp.concatenate([q_ns[kv][rows] * head_masks[g] for g in range(AT_GROUP)], axis=0)
        lg = lax.dot_general(k_reps[kv][keys], q4, (((1,), (1,)), ((), ())),
                             preferred_element_type=F32) + bias_ref[kv]
        if nb == 0:
            lg = jnp.concatenate([lg[:BLOCK] + pre_seq, lg[BLOCK:]], axis=0)
        return lg

    units = [(kv, nb) for kv in range(AT_KV_HEADS) for nb in range(n_blocks)]
    lg_next = logits_t(*units[0])
    yield
    for idx, (kv, nb) in enumerate(units):
        lg = lg_next
        if idx + 1 < len(units):
            lg_next = logits_t(*units[idx + 1])
            yield
        rows = slice(nb * BLOCK, (nb + 1) * BLOCK)
        keys = slice(nb * BLOCK, (nb + 2) * BLOCK)
        sink = jnp.concatenate(
            [jnp.full((1, BLOCK), sink_ref[kv * AT_GROUP + g], F32) for g in range(AT_GROUP)], axis=1)
        m = jnp.maximum(jnp.max(lg, axis=0, keepdims=True), sink)
        e = jnp.exp(lg - m)
        den = jnp.sum(e, axis=0, keepdims=True) + jnp.exp(sink - m)
        o_t = jnp.dot(v_t[kv * d:(kv + 1) * d, keys], e.astype(BF16),
                      preferred_element_type=F32) * (1.0 / den)
        o = jnp.concatenate([o_t[:, g * BLOCK:(g + 1) * BLOCK] for g in range(AT_GROUP)], axis=0).T
        o_ref[rows, kv * grp:(kv + 1) * grp] = o.astype(o_ref.dtype)
        yield


MERGE_SPLIT = 2


def _merge_kernel(x_ref, ohg_ref, oat_ref, ghg_ref, gat_ref, wbh_ref, wba_ref, wo_ref,
                  gate1_ref, n2g_ref, shift2_ref, scale2_ref, x1_ref, h2_ref):
    part = x_ref.shape[0] // MERGE_SPLIT
    parts = [slice(p * part, (p + 1) * part) for p in range(MERGE_SPLIT)]
    m_hg = [jnp.dot(ohg_ref[rows, :], wbh_ref[...], preferred_element_type=F32) for rows in parts]
    m_at = [jnp.dot(oat_ref[rows, :], wba_ref[...], preferred_element_type=F32) for rows in parts]
    ys = []
    for p, rows in enumerate(parts):
        merged = (jax.nn.sigmoid(ghg_ref[rows, :].astype(F32)) * m_hg[p]
                  + jax.nn.sigmoid(gat_ref[rows, :].astype(F32)) * m_at[p])
        ys.append(jnp.dot(merged.astype(BF16), wo_ref[...], preferred_element_type=F32))
    for p, rows in enumerate(parts):
        x1 = x_ref[rows, :] + gate1_ref[...] * ys[p]
        x1_ref[rows, :] = x1
        h2_ref[rows, :] = _rms_modulate(x1, n2g_ref[...], scale2_ref[...], shift2_ref[...]).astype(h2_ref.dtype)


def _merge(x, o_hg, o_at, gates, w_bh, w_ba, w_o, ada4, norm2_g):
    bsz, seq, d = x.shape
    tm = 512
    resident = lambda shape: pl.BlockSpec(shape, lambda b, i: (0, 0), pipeline_mode=pl.Buffered(1))
    ada_row = lambda k: pl.BlockSpec((None, None, 1, d), lambda b, i: (b, k, 0, 0))
    rows = lambda w, colblk: pl.BlockSpec((None, tm, w), lambda b, i: (b, i, colblk))
    return pl.pallas_call(
        _merge_kernel,
        grid=(bsz, seq // tm),
        in_specs=[
            rows(d, 0), rows(HG_WIDTH, 0), rows(AT_WIDTH, 0), rows(d, 0), rows(d, 1),
            resident(w_bh.shape), resident(w_ba.shape), resident(w_o.shape),
            ada_row(2), pl.BlockSpec((1, d), lambda b, i: (0, 0)), ada_row(3), ada_row(4),
        ],
        out_specs=[rows(d, 0), rows(d, 0)],
        out_shape=[jax.ShapeDtypeStruct((bsz, seq, d), F32), jax.ShapeDtypeStruct((bsz, seq, d), BF16)],
        compiler_params=_params(2),
        name="merge",
    )(x, o_hg, o_at, gates, gates, w_bh, w_ba, w_o, ada4, norm2_g, ada4, ada4)


def _ffn_kernel(h2_ref, w1_ref, w2_ref, x1_ref, gate2_ref, o_ref):
    j = pl.program_id(2)

    @pl.when(j == 0)
    def _():
        o_ref[...] = jnp.zeros_like(o_ref)

    hid = jnp.maximum(jnp.dot(h2_ref[...], w1_ref[...], preferred_element_type=F32), 0.0)
    o_ref[...] += jnp.dot((hid * hid).astype(BF16), w2_ref[...], preferred_element_type=F32)

    @pl.when(j == pl.num_programs(2) - 1)
    def _():
        o_ref[...] = x1_ref[...] + gate2_ref[...] * o_ref[...]


def _ffn(h2, x1, w1, w2, ada4):
    bsz, seq, d = x1.shape
    dff = w1.shape[-1]
    tm, tf = 512, 2048
    rows = lambda: pl.BlockSpec((None, tm, d), lambda b, i, j: (b, i, 0))
    return pl.pallas_call(
        _ffn_kernel,
        grid=(bsz, seq // tm, dff // tf),
        in_specs=[
            rows(),
            pl.BlockSpec((d, tf), lambda b, i, j: (0, j)),
            pl.BlockSpec((tf, d), lambda b, i, j: (j, 0)),
            rows(),
            pl.BlockSpec((None, None, 1, d), lambda b, i, j: (b, 5, 0, 0)),
        ],
        out_specs=rows(),
        out_shape=jax.ShapeDtypeStruct((bsz, seq, d), F32),
        compiler_params=_params(3),
        name="ffn",
    )(h2, w1, w2, x1, ada4)


def kernel(x, c, w_ada, b_ada, norm1_g, norm2_g, w_in, hg_lb_logits, hg_out_norm_g, q_norm_g, k_norm_g,
           attn_sinks, rel_bias_table, w_branch_hg, w_branch_attn, w_out, w_ff1, w_ff2):
    assert w_ada.shape[0] == 1, "single-layer block"
    bsz = x.shape[0]
    ada = _ada(c, w_ada, b_ada)
    ada4 = ada.reshape(bsz, 6, 1, D_MODEL)
    w_in_bf16 = w_in[0].astype(BF16)
    proj, (w_bh, w_ba, w_o, w1, w2) = _inproj(
        x, norm1_g, ada4, w_in_bf16, GATE_OFF, (w_branch_hg, w_branch_attn, w_out, w_ff1, w_ff2))
    bias = _attn_bias(rel_bias_table)
    gates, o_hg, o_at = _mixer(x, norm1_g, ada4, w_in_bf16, proj, hg_lb_logits, hg_out_norm_g,
                               bias, attn_sinks[0], q_norm_g, k_norm_g)
    x1, h2 = _merge(x, o_hg, o_at, gates, w_bh, w_ba, w_o, ada4, norm2_g)
    return _ffn(h2, x1, w1, w2, ada4)
```

```python
import functools
import math

import jax
import jax.numpy as jnp
from jax import lax
from jax.experimental import pallas as pl
from jax.experimental.pallas import tpu as pltpu

F32 = jnp.float32
BF16 = jnp.bfloat16

D_MODEL = 2048
HG_HEADS = 8
HG_DK = 128
HG_DV = 128
HG_WIDTH = HG_HEADS * HG_DK
HG_CHUNK = 64
AT_HEADS = 16
AT_KV_HEADS = 4
AT_HEAD_DIM = 64
AT_GROUP = AT_HEADS // AT_KV_HEADS
AT_WIDTH = AT_HEADS * AT_HEAD_DIM
KV_WIDTH = AT_KV_HEADS * AT_HEAD_DIM
WINDOW = 128
BLOCK = 128
N_BUCKETS = 32
MAX_EXACT = N_BUCKETS // 2
MAX_DISTANCE = 128
D_FF = 4 * D_MODEL
EPS = 1e-6
NEG_INF = -1e30

HG_OFF = 0
AT_OFF = 4 * HG_WIDTH
GATE_OFF = AT_OFF + AT_WIDTH + 2 * KV_WIDTH
IN_WIDTH = GATE_OFF + 2 * D_MODEL
MXU_COLS = 256

V7X_VMEM_LIMIT_BYTES = 60 * 1024 * 1024
BF16_SUBLANES = 16


def _params(n_axes):
    return pltpu.CompilerParams(
        dimension_semantics=("arbitrary",) * n_axes,
        vmem_limit_bytes=V7X_VMEM_LIMIT_BYTES,
    )


def _rms_modulate(x, g, scale, shift):
    r = lax.rsqrt(jnp.mean(x * x, axis=-1, keepdims=True) + EPS)
    return (x * r * g) * (1.0 + scale) + shift


def _ada_columns(c_ref, w_ref, b_ref):
    c = c_ref[...]
    c_act = c * jax.nn.sigmoid(c)
    return jnp.dot(c_act, w_ref[...], preferred_element_type=F32, precision=lax.Precision.HIGHEST) + b_ref[...]


def _ada_kernel(c_ref, w_ref, b_ref, o_ref):
    o_ref[...] = _ada_columns(c_ref, w_ref, b_ref)


def _ada(c, w_ada, b_ada, n):
    bsz, d = c.shape
    tn = 1024
    return pl.pallas_call(
        _ada_kernel,
        grid=(n // tn,),
        in_specs=[
            pl.BlockSpec((bsz, d), lambda j: (0, 0)),
            pl.BlockSpec((None, d, tn), lambda j: (0, 0, j)),
            pl.BlockSpec((1, tn), lambda j: (0, j)),
        ],
        out_specs=pl.BlockSpec((bsz, tn), lambda j: (0, j)),
        out_shape=jax.ShapeDtypeStruct((bsz, n), F32),
        compiler_params=_params(1),
        name="ada",
    )(c, w_ada, b_ada)


def _inproj_kernel(late_parts, x_ref, g_ref, shift_ref, scale_ref, w_ref, c_ref, wada_ref, bada_ref, *refs):
    n_in = sum(late_parts)
    late_in, (o_ref, ada_ref, *late_out) = refs[:n_in], refs[n_in:]
    h = _rms_modulate(x_ref[...], g_ref[...], scale_ref[...], shift_ref[...])
    o_ref[...] = jnp.dot(h.astype(BF16), w_ref[...], preferred_element_type=F32).astype(o_ref.dtype)
    ada_ref[...] = _ada_columns(c_ref, wada_ref, bada_ref)
    first = 0
    for n_parts, dst in zip(late_parts, late_out):
        parts = [ref[...] for ref in late_in[first:first + n_parts]]
        first += n_parts
        dst[...] = (parts[0] if n_parts == 1 else jnp.concatenate(parts, axis=1)).astype(dst.dtype)


def _inproj(x, norm_g, ada4, w_bf16, late_weights, c, w_ada, b_ada, ada_done):
    bsz, seq, d = x.shape
    n = w_bf16.shape[-1]
    tm, n_col_tiles = 512, 2
    tn = n // n_col_tiles
    assert tn % MXU_COLS == 0
    tiles_per_seq = seq // tm
    n_row_tiles = bsz * tiles_per_seq
    n_steps = n_col_tiles * n_row_tiles
    row = lambda j, i: (i // tiles_per_seq, i % tiles_per_seq)
    step = lambda j, i: j * n_row_tiles + i
    late_in_specs, late_out_specs, late_shapes, late_parts, late_args = [], [], [], [], []
    for w, window in late_weights:
        _, rows, cols = w.shape
        first_col, width = window or (0, cols)
        part = math.gcd(first_col, width)
        slab = max(rows // n_steps, BF16_SUBLANES)
        assert rows % slab == 0
        idx = lambda j, i, last=rows // slab - 1: jnp.minimum(step(j, i), last)
        for k in range(width // part):
            late_in_specs.append(pl.BlockSpec(
                (None, slab, part), lambda j, i, idx=idx, blk=first_col // part + k: (0, idx(j, i), blk)))
        late_parts.append(width // part)
        late_args += [w] * (width // part)
        late_out_specs.append(pl.BlockSpec((slab, width), lambda j, i, idx=idx: (idx(j, i), 0)))
        late_shapes.append(jax.ShapeDtypeStruct((rows, width), BF16))
    n_ada = w_ada.shape[-1] - ada_done
    ada_cols = n_ada // n_steps
    assert ada_cols % 128 == 0 and ada_done % ada_cols == 0
    ada_blk = lambda j, i: ada_done // ada_cols + step(j, i)
    outs = pl.pallas_call(
        functools.partial(_inproj_kernel, tuple(late_parts)),
        grid=(n_col_tiles, n_row_tiles),
        in_specs=[
            pl.BlockSpec((None, tm, d), lambda j, i: (*row(j, i), 0)),
            pl.BlockSpec((1, d), lambda j, i: (0, 0)),
            pl.BlockSpec((None, None, 1, d), lambda j, i: (i // tiles_per_seq, 0, 0, 0)),
            pl.BlockSpec((None, None, 1, d), lambda j, i: (i // tiles_per_seq, 1, 0, 0)),
            pl.BlockSpec((d, tn), lambda j, i: (0, j)),
            pl.BlockSpec(c.shape, lambda j, i: (0, 0)),
            pl.BlockSpec((None, d, ada_cols), lambda j, i: (0, 0, ada_blk(j, i))),
            pl.BlockSpec((1, ada_cols), lambda j, i: (0, ada_blk(j, i))),
            *late_in_specs,
        ],
        out_specs=[pl.BlockSpec((None, tm, tn), lambda j, i: (*row(j, i), j)),
                   pl.BlockSpec((bsz, ada_cols), lambda j, i: (0, step(j, i))), *late_out_specs],
        out_shape=[jax.ShapeDtypeStruct((bsz, seq, n), BF16), jax.ShapeDtypeStruct((bsz, n_ada), F32),
                   *late_shapes],
        compiler_params=_params(2),
        name="inproj",
    )(x, norm_g, ada4, ada4, w_bf16, c, w_ada, b_ada, *late_args)
    return outs[0], outs[1], outs[2:]


HG_GROUP = 4
HG_ROWS = HG_GROUP * HG_CHUNK
GATE_CHUNK = 256


def _mixer_kernel(n_wg, x_ref, g_ref, shift_ref, scale_ref, *refs):
    wg_refs = refs[:n_wg]
    (q_ref, f_ref, i_ref, hg_ref, lbl_ref, gn_ref, *attn_in,
     gates_ref, o_ref, oat_ref, h_ref, state_ref) = refs[n_wg:]
    c, n = HG_CHUNK, HG_ROWS
    heads = range(HG_HEADS)
    lanes = [slice(u * HG_DK, (u + 1) * HG_DK) for u in heads]

    @pl.when(pl.program_id(1) == 0)
    def _():
        state_ref[...] = jnp.zeros_like(state_ref)

    h_ref[...] = _rms_modulate(x_ref[...], g_ref[...], scale_ref[...], shift_ref[...]).astype(BF16)
    wg_cols = wg_refs[0].shape[1]
    pending = list(range(n_wg * wg_cols // GATE_CHUNK))

    def gate_chunks(count):
        for _ in range(min(count, len(pending))):
            k = pending.pop(0)
            w_ref, off = wg_refs[k * GATE_CHUNK // wg_cols], k * GATE_CHUNK % wg_cols
            gates_ref[:, k * GATE_CHUNK:(k + 1) * GATE_CHUNK] = jnp.dot(
                h_ref[...], w_ref[:, off:off + GATE_CHUNK], preferred_element_type=F32).astype(gates_ref.dtype)

    attention = _attention_steps(*attn_in, oat_ref, pl.program_id(1) == 0)
    attend = lambda: next(attention, None)

    def interleave(count):
        for _ in range(count):
            attend()
            gate_chunks(1)

    lbl = lbl_ref[...]
    e = jnp.exp(lbl - jnp.max(lbl, axis=0, keepdims=True))
    lb = e[0:1] / jnp.sum(e, axis=0, keepdims=True)
    gn = gn_ref[...]

    t = lax.broadcasted_iota(jnp.int32, (n, n), 0)
    s = lax.broadcasted_iota(jnp.int32, (n, n), 1)
    tri = ((t // c) == (s // c)) & (s <= t)
    r8 = lax.broadcasted_iota(jnp.int32, (8, n), 0)
    s8 = lax.broadcasted_iota(jnp.int32, (8, n), 1)
    in_chunk = (s8 // c) == (r8 % HG_GROUP)
    stat_rows = in_chunk & ((r8 >= HG_GROUP) | ((s8 % c) < c // 2))
    cum_ops = jnp.concatenate([tri.astype(F32), stat_rows.astype(F32)], axis=0).astype(BF16)
    per_chunk = lambda rows: jnp.concatenate(
        [jnp.broadcast_to(rows[cc:cc + 1], (c, HG_DK)) for cc in range(HG_GROUP)], axis=0)
    col_chunk = lax.broadcasted_iota(jnp.int32, (HG_DV, n), 1) // c
    col_masks = [(col_chunk == cc).astype(BF16) for cc in range(HG_GROUP)]
    zeros_rows = lambda rows: jnp.zeros((rows, HG_DK), BF16)

    attend()
    gate_chunks(3)
    ks, qss, hilo = [], [], []
    for u in heads:
        hf = f_ref[:, lanes[u]].astype(F32)
        hq = q_ref[:, lanes[u]].astype(F32)
        f = lb[:, lanes[u]] + (1.0 - lb[:, lanes[u]]) * jax.nn.sigmoid(hf)
        log_f = jnp.log(f)
        ks.append(1.0 - f)
        qss.append(hq * jax.nn.sigmoid(hq))
        hi = log_f.astype(BF16)
        lo = (log_f - hi.astype(F32)).astype(BF16)
        hilo.append(jnp.concatenate([hi, lo], axis=1))
    rs = [jnp.dot(cum_ops, hilo[u], preferred_element_type=F32) for u in heads]
    interleave(2)
    a_s, bms, kks, qes, decs = [], [], [], [], []
    for u in heads:
        r = rs[u][:, :HG_DK] + rs[u][:, HG_DK:]
        b, b_last = r[0:n], r[n + HG_GROUP:n + 2 * HG_GROUP]
        b_rel = b - per_chunk(r[n:n + HG_GROUP])
        a_s.append((qss[u] * jnp.exp(b_rel)).astype(BF16))
        bms.append((ks[u] * jnp.exp(-b_rel)).astype(BF16))
        kks.append((ks[u] * jnp.exp(per_chunk(b_last) - b)).astype(BF16))
        qes.append((qss[u] * jnp.exp(b)).astype(BF16))
        decs.append(jnp.exp(b_last))
    scores = [lax.dot_general(a_s[u], bms[u], (((1,), (1,)), ((), ())), preferred_element_type=F32)
              for u in heads]
    interleave(2)
    vs = [i_ref[:, lanes[u]] for u in heads]
    os = [jnp.dot(jnp.where(tri, scores[u], 0.0).astype(BF16), vs[u], preferred_element_type=F32)
          for u in heads]
    interleave(2)
    upd_ts = []
    for u in heads:
        v_t = vs[u].astype(F32).T.astype(BF16)
        v_t_blocks = jnp.concatenate([v_t * col_masks[cc] for cc in range(HG_GROUP)], axis=0)
        upd_ts.append(jnp.dot(v_t_blocks, kks[u], preferred_element_type=F32))
    interleave(2)
    for u in heads:
        s_t = state_ref[u]
        states = []
        for cc in range(HG_GROUP):
            states.append(s_t.astype(BF16))
            s_t = s_t * decs[u][cc:cc + 1] + upd_ts[u][cc * HG_DV:(cc + 1) * HG_DV]
        state_ref[u] = s_t
        qe = qes[u]
        qe_blocks = jnp.concatenate(
            [jnp.concatenate([p for p in (zeros_rows(cc * c), qe[cc * c:(cc + 1) * c],
                                          zeros_rows(n - (cc + 1) * c)) if p.shape[0]], axis=0)
             for cc in range(HG_GROUP)], axis=1)
        os[u] = os[u] + lax.dot_general(qe_blocks, jnp.concatenate(states, axis=1),
                                        (((1,), (1,)), ((), ())), preferred_element_type=F32)
    interleave(len(pending) - 2)
    for _ in attention:
        pass
    gate_chunks(len(pending))
    for u in heads:
        o = os[u]
        hg = hg_ref[:, lanes[u]].astype(F32)
        y = o * lax.rsqrt(jnp.mean(o * o, axis=-1, keepdims=True) + EPS) * gn
        o_ref[:, lanes[u]] = (y * (hg * jax.nn.sigmoid(hg))).astype(o_ref.dtype)


def _mixer(x, norm_g, ada4, w_gate_bf16, proj, lb_logits, out_norm_g, bias, sinks, q_norm_g, k_norm_g):
    bsz, seq, d = x.shape
    n_gate = w_gate_bf16.shape[-1]
    tm = HG_ROWS
    blocks_per_step = tm // BLOCK
    rows = lambda w, colblk: pl.BlockSpec((None, tm, w), lambda b, i: (b, i, colblk))
    ada_row = lambda k: pl.BlockSpec((None, None, 1, d), lambda b, i: (b, k, 0, 0))
    const = lambda shape: pl.BlockSpec(shape, lambda b, i: (0,) * len(shape))
    n_wg = 1
    w_gate = [pl.BlockSpec((d, n_gate), lambda b, i: (0, 0), pipeline_mode=pl.Buffered(1))]
    hg_base = HG_OFF // HG_WIDTH
    k_col = (AT_OFF + AT_WIDTH) // KV_WIDTH
    halo = lambda colblk: pl.BlockSpec(
        (None, BLOCK, KV_WIDTH), lambda b, i: (b, jnp.maximum(i * blocks_per_step - 1, 0), colblk))
    qg = jnp.tile(q_norm_g, (1, AT_GROUP))
    kg = jnp.tile(k_norm_g, (1, AT_KV_HEADS))
    return pl.pallas_call(
        functools.partial(_mixer_kernel, n_wg),
        grid=(bsz, seq // tm),
        in_specs=[
            rows(d, 0), const((1, d)), ada_row(0), ada_row(1), *w_gate,
            rows(HG_WIDTH, hg_base), rows(HG_WIDTH, hg_base + 1), rows(HG_WIDTH, hg_base + 2),
            rows(HG_WIDTH, hg_base + 3), const(lb_logits.shape), const((1, HG_DV)),
            rows(AT_WIDTH, AT_OFF // AT_WIDTH), rows(KV_WIDTH, k_col), rows(KV_WIDTH, k_col + 1),
            halo(k_col), halo(k_col + 1),
            pl.BlockSpec(bias.shape, lambda b, i: (0, 0, 0), pipeline_mode=pl.Buffered(1)),
            pl.BlockSpec(memory_space=pltpu.SMEM), const((1, AT_GROUP * AT_HEAD_DIM)), const((1, KV_WIDTH)),
        ],
        out_specs=[rows(n_gate, 0), rows(HG_WIDTH, 0), rows(AT_WIDTH, 0)],
        out_shape=[jax.ShapeDtypeStruct((bsz, seq, n_gate), BF16),
                   jax.ShapeDtypeStruct((bsz, seq, HG_WIDTH), BF16),
                   jax.ShapeDtypeStruct((bsz, seq, AT_WIDTH), BF16)],
        scratch_shapes=[pltpu.VMEM((tm, d), BF16), pltpu.VMEM((HG_HEADS, HG_DV, HG_DK), F32)],
        compiler_params=_params(2),
        name="mixer",
    )(x, norm_g, ada4, ada4, w_gate_bf16, proj, proj, proj, proj, lb_logits, out_norm_g,
      proj, proj, proj, proj, proj, bias, sinks, qg, kg)


def _t5_causal_bucket(n):
    nf = jnp.maximum(n, 1).astype(F32)
    large = MAX_EXACT + (jnp.log(nf / MAX_EXACT) / math.log(MAX_DISTANCE / MAX_EXACT)
                         * (N_BUCKETS - MAX_EXACT)).astype(jnp.int32)
    large = jnp.minimum(large, N_BUCKETS - 1)
    return jnp.where(n < MAX_EXACT, n, large)


def _bias_kernel(table_ref, bucket_t_ref, o_ref):
    bucket_t = bucket_t_ref[...]
    j = lax.broadcasted_iota(jnp.int32, bucket_t.shape, 0)
    i = lax.broadcasted_iota(jnp.int32, bucket_t.shape, 1)
    dist = i - j + BLOCK
    band = (dist >= 0) & (dist < WINDOW)
    for h in range(AT_HEADS):
        acc = jnp.zeros(bucket_t.shape, F32)
        for r in range(N_BUCKETS):
            acc = jnp.where(bucket_t == r, table_ref[r, h], acc)
        g = h % AT_GROUP
        o_ref[h // AT_GROUP, :, g * BLOCK:(g + 1) * BLOCK] = jnp.where(band, acc, NEG_INF)


def _attn_bias(rel_bias_table):
    i = jnp.arange(BLOCK, dtype=jnp.int32)[None, :]
    j = jnp.arange(2 * BLOCK, dtype=jnp.int32)[:, None]
    bucket_t = _t5_causal_bucket(jnp.maximum(i - j + BLOCK, 0))
    shape = (AT_KV_HEADS, 2 * BLOCK, AT_GROUP * BLOCK)
    return pl.pallas_call(
        _bias_kernel,
        in_specs=[
            pl.BlockSpec(memory_space=pltpu.SMEM),
            pl.BlockSpec((2 * BLOCK, BLOCK), lambda: (0, 0)),
        ],
        out_specs=pl.BlockSpec(shape, lambda: (0, 0, 0)),
        out_shape=jax.ShapeDtypeStruct(shape, F32),
        name="attn_bias",
    )(rel_bias_table, bucket_t)


def _head_mean_square(t, pool):
    return jnp.dot((t * t).astype(BF16), pool, preferred_element_type=F32)


def _attention_steps(q_ref, k_ref, v_ref, kh_ref, vh_ref, bias_ref, sink_ref, qg_ref, kg_ref, o_ref, first_block):
    n_blocks = q_ref.shape[0] // BLOCK
    d = AT_HEAD_DIM
    grp = AT_GROUP * d
    sub = lax.broadcasted_iota(jnp.int32, (grp, grp), 0)
    lane = lax.broadcasted_iota(jnp.int32, (grp, grp), 1)
    pool = jnp.where(sub // d == lane // d, 1.0 / d, 0.0).astype(BF16)

    k_all = jnp.concatenate([kh_ref[...], k_ref[...]], axis=0).astype(F32)
    k_n = (k_all * lax.rsqrt(_head_mean_square(k_all, pool) + EPS) * kg_ref[...]).astype(BF16)
    v_t = jnp.concatenate([vh_ref[...], v_ref[...]], axis=0).astype(F32).T.astype(BF16)

    lane_head = lax.broadcasted_iota(jnp.int32, (BLOCK, grp), 1) // d
    head_masks = [(lane_head == g).astype(BF16) for g in range(AT_GROUP)]
    pre_seq = jnp.where(first_block, NEG_INF, 0.0)

    scale = d ** -0.5
    k_reps, q_ns = [], []
    for kv in range(AT_KV_HEADS):
        spread = ((sub // d == kv) & (sub % d == lane % d)).astype(BF16)
        k_reps.append(jnp.dot(k_n, spread, preferred_element_type=F32).astype(BF16))
        q_g = q_ref[:, kv * grp:(kv + 1) * grp].astype(F32)
        q_ns.append((q_g * lax.rsqrt(_head_mean_square(q_g, pool) + EPS) * (qg_ref[...] * scale)).astype(BF16))
    yield

    def logits_t(kv, nb):
        rows = slice(nb * BLOCK, (nb + 1) * BLOCK)
        keys = slice(nb * BLOCK, (nb + 2) * BLOCK)
        q4 = jnp.concatenate([q_ns[kv][rows] * head_masks[g] for g in range(AT_GROUP)], axis=0)
        lg = lax.dot_general(k_reps[kv][keys], q4, (((1,), (1,)), ((), ())),
                             preferred_element_type=F32) + bias_ref[kv]
        if nb == 0:
            lg = jnp.concatenate([lg[:BLOCK] + pre_seq, lg[BLOCK:]], axis=0)
        return lg

    units = [(kv, nb) for kv in range(AT_KV_HEADS) for nb in range(n_blocks)]
    lg_next = logits_t(*units[0])
    yield
    for idx, (kv, nb) in enumerate(units):
        lg = lg_next
        if idx + 1 < len(units):
            lg_next = logits_t(*units[idx + 1])
            yield
        rows = slice(nb * BLOCK, (nb + 1) * BLOCK)
        keys = slice(nb * BLOCK, (nb + 2) * BLOCK)
        sink = jnp.concatenate(
            [jnp.full((1, BLOCK), sink_ref[kv * AT_GROUP + g], F32) for g in range(AT_GROUP)], axis=1)
        m = jnp.maximum(jnp.max(lg, axis=0, keepdims=True), sink)
        e = jnp.exp(lg - m)
        den = jnp.sum(e, axis=0, keepdims=True) + jnp.exp(sink - m)
        o_t = jnp.dot(v_t[kv * d:(kv + 1) * d, keys], e.astype(BF16),
                      preferred_element_type=F32) * (1.0 / den)
        o = jnp.concatenate([o_t[:, g * BLOCK:(g + 1) * BLOCK] for g in range(AT_GROUP)], axis=0).T
        o_ref[rows, kv * grp:(kv + 1) * grp] = o.astype(o_ref.dtype)
        yield


MERGE_SPLIT = 2


def _merge_kernel(x_ref, ohg_ref, oat_ref, ghg_ref, gat_ref, wbh_ref, wba_ref, wo_ref,
                  gate1_ref, n2g_ref, shift2_ref, scale2_ref, x1_ref, h2_ref):
    part = x_ref.shape[0] // MERGE_SPLIT
    parts = [slice(p * part, (p + 1) * part) for p in range(MERGE_SPLIT)]
    m_hg = [jnp.dot(ohg_ref[rows, :], wbh_ref[...], preferred_element_type=F32) for rows in parts]
    m_at = [jnp.dot(oat_ref[rows, :], wba_ref[...], preferred_element_type=F32) for rows in parts]
    ys = []
    for p, rows in enumerate(parts):
        merged = (jax.nn.sigmoid(ghg_ref[rows, :].astype(F32)) * m_hg[p]
                  + jax.nn.sigmoid(gat_ref[rows, :].astype(F32)) * m_at[p])
        ys.append(jnp.dot(merged.astype(BF16), wo_ref[...], preferred_element_type=F32))
    for p, rows in enumerate(parts):
        x1 = x_ref[rows, :] + gate1_ref[...] * ys[p]
        x1_ref[rows, :] = x1
        h2_ref[rows, :] = _rms_modulate(x1, n2g_ref[...], scale2_ref[...], shift2_ref[...]).astype(h2_ref.dtype)


def _merge(x, o_hg, o_at, gates, w_bh, w_ba, w_o, ada4, norm2_g):
    bsz, seq, d = x.shape
    tm = 512
    resident = lambda shape: pl.BlockSpec(shape, lambda b, i: (0, 0), pipeline_mode=pl.Buffered(1))
    ada_row = lambda k: pl.BlockSpec((None, None, 1, d), lambda b, i: (b, k, 0, 0))
    rows = lambda w, colblk: pl.BlockSpec((None, tm, w), lambda b, i: (b, i, colblk))
    return pl.pallas_call(
        _merge_kernel,
        grid=(bsz, seq // tm),
        in_specs=[
            rows(d, 0), rows(HG_WIDTH, 0), rows(AT_WIDTH, 0), rows(d, 0), rows(d, 1),
            resident(w_bh.shape), resident(w_ba.shape), resident(w_o.shape),
            ada_row(0), pl.BlockSpec((1, d), lambda b, i: (0, 0)), ada_row(1), ada_row(2),
        ],
        out_specs=[rows(d, 0), rows(d, 0)],
        out_shape=[jax.ShapeDtypeStruct((bsz, seq, d), F32), jax.ShapeDtypeStruct((bsz, seq, d), BF16)],
        compiler_params=_params(2),
        name="merge",
    )(x, o_hg, o_at, gates, gates, w_bh, w_ba, w_o, ada4, norm2_g, ada4, ada4)


def _ffn_kernel(h2_ref, w1_ref, w2_ref, x1_ref, gate2_ref, o_ref):
    j = pl.program_id(2)

    @pl.when(j == 0)
    def _():
        o_ref[...] = jnp.zeros_like(o_ref)

    hid = jnp.maximum(jnp.dot(h2_ref[...], w1_ref[...], preferred_element_type=F32), 0.0)
    o_ref[...] += jnp.dot((hid * hid).astype(BF16), w2_ref[...], preferred_element_type=F32)

    @pl.when(j == pl.num_programs(2) - 1)
    def _():
        o_ref[...] = x1_ref[...] + gate2_ref[...] * o_ref[...]


def _ffn(h2, x1, w1, w2, ada4):
    bsz, seq, d = x1.shape
    dff = w1.shape[-1]
    tm, tf = 512, 2048
    rows = lambda: pl.BlockSpec((None, tm, d), lambda b, i, j: (b, i, 0))
    return pl.pallas_call(
        _ffn_kernel,
        grid=(bsz, seq // tm, dff // tf),
        in_specs=[
            rows(),
            pl.BlockSpec((d, tf), lambda b, i, j: (0, j)),
            pl.BlockSpec((tf, d), lambda b, i, j: (j, 0)),
            rows(),
            pl.BlockSpec((None, None, 1, d), lambda b, i, j: (b, 3, 0, 0)),
        ],
        out_specs=rows(),
        out_shape=jax.ShapeDtypeStruct((bsz, seq, d), F32),
        compiler_params=_params(3),
        name="ffn",
    )(h2, w1, w2, x1, ada4)


def kernel(x, c, w_ada, b_ada, norm1_g, norm2_g, w_in, hg_lb_logits, hg_out_norm_g, q_norm_g, k_norm_g,
           attn_sinks, rel_bias_table, w_branch_hg, w_branch_attn, w_out, w_ff1, w_ff2):
    assert w_ada.shape[0] == 1, "single-layer block"
    bsz = x.shape[0]
    d = D_MODEL
    ada_in = _ada(c, w_ada, b_ada, 2 * d).reshape(bsz, 2, 1, d)
    proj, ada_rest, (w_gate, w_bh, w_ba, w_o, w1, w2) = _inproj(
        x, norm1_g, ada_in, w_in[0, :, :GATE_OFF].astype(BF16),
        ((w_in, (GATE_OFF, IN_WIDTH - GATE_OFF)), (w_branch_hg, None), (w_branch_attn, None), (w_out, None),
         (w_ff1, None), (w_ff2, None)),
        c, w_ada, b_ada, 2 * d)
    ada_out = ada_rest.reshape(bsz, 4, 1, d)
    bias = _attn_bias(rel_bias_table)
    gates, o_hg, o_at = _mixer(x, norm1_g, ada_in, w_gate, proj, hg_lb_logits, hg_out_norm_g,
                               bias, attn_sinks[0], q_norm_g, k_norm_g)
    x1, h2 = _merge(x, o_hg, o_at, gates, w_bh, w_ba, w_o, ada_out, norm2_g)
    return _ffn(h2, x1, w1, w2, ada_out)
```

```python
import functools
import math

import jax
import jax.numpy as jnp
from jax import lax
from jax.experimental import pallas as pl
from jax.experimental.pallas import tpu as pltpu

F32 = jnp.float32
BF16 = jnp.bfloat16

D_MODEL = 2048
HG_HEADS = 8
HG_DK = 128
HG_DV = 128
HG_WIDTH = HG_HEADS * HG_DK
HG_CHUNK = 64
AT_HEADS = 16
AT_KV_HEADS = 4
AT_HEAD_DIM = 64
AT_GROUP = AT_HEADS // AT_KV_HEADS
AT_WIDTH = AT_HEADS * AT_HEAD_DIM
KV_WIDTH = AT_KV_HEADS * AT_HEAD_DIM
WINDOW = 128
BLOCK = 128
N_BUCKETS = 32
MAX_EXACT = N_BUCKETS // 2
MAX_DISTANCE = 128
D_FF = 4 * D_MODEL
EPS = 1e-6
NEG_INF = -1e30

HG_OFF = 0
AT_OFF = 4 * HG_WIDTH
GATE_OFF = AT_OFF + AT_WIDTH + 2 * KV_WIDTH
IN_WIDTH = GATE_OFF + 2 * D_MODEL
MXU_COLS = 256

V7X_VMEM_LIMIT_BYTES = 60 * 1024 * 1024
BF16_SUBLANES = 16


def _params(n_axes):
    return pltpu.CompilerParams(
        dimension_semantics=("arbitrary",) * n_axes,
        vmem_limit_bytes=V7X_VMEM_LIMIT_BYTES,
    )


def _rms_modulate(x, g, scale, shift):
    r = lax.rsqrt(jnp.mean(x * x, axis=-1, keepdims=True) + EPS)
    return (x * r * g) * (1.0 + scale) + shift


def _sigmoid(x):
    return 0.5 * jnp.tanh(0.5 * x) + 0.5


def _silu(x):
    h = 0.5 * x
    return h + h * jnp.tanh(h)


def _ada_columns(c_ref, w_ref, b_ref):
    c = c_ref[...]
    c_act = c * jax.nn.sigmoid(c)
    return jnp.dot(c_act, w_ref[...], preferred_element_type=F32, precision=lax.Precision.HIGHEST) + b_ref[...]


def _ada_kernel(c_ref, w_ref, b_ref, o_ref):
    o_ref[...] = _ada_columns(c_ref, w_ref, b_ref)


def _ada(c, w_ada, b_ada, n):
    bsz, d = c.shape
    tn = 1024
    return pl.pallas_call(
        _ada_kernel,
        grid=(n // tn,),
        in_specs=[
            pl.BlockSpec((bsz, d), lambda j: (0, 0)),
            pl.BlockSpec((None, d, tn), lambda j: (0, 0, j)),
            pl.BlockSpec((1, tn), lambda j: (0, j)),
        ],
        out_specs=pl.BlockSpec((bsz, tn), lambda j: (0, j)),
        out_shape=jax.ShapeDtypeStruct((bsz, n), F32),
        compiler_params=_params(1),
        name="ada",
    )(c, w_ada, b_ada)


def _inproj_kernel(late_parts, x_ref, g_ref, shift_ref, scale_ref, w_ref, c_ref, wada_ref, bada_ref, *refs):
    n_in = sum(late_parts)
    late_in, (o_ref, ada_ref, *late_out) = refs[:n_in], refs[n_in:]
    h = _rms_modulate(x_ref[...], g_ref[...], scale_ref[...], shift_ref[...])
    o_ref[...] = jnp.dot(h.astype(BF16), w_ref[...], preferred_element_type=F32).astype(o_ref.dtype)
    ada_ref[...] = _ada_columns(c_ref, wada_ref, bada_ref)
    first = 0
    for n_parts, dst in zip(late_parts, late_out):
        parts = [ref[...] for ref in late_in[first:first + n_parts]]
        first += n_parts
        dst[...] = (parts[0] if n_parts == 1 else jnp.concatenate(parts, axis=1)).astype(dst.dtype)


def _inproj(x, norm_g, ada4, w_bf16, late_weights, c, w_ada, b_ada, ada_done):
    bsz, seq, d = x.shape
    n = w_bf16.shape[-1]
    tm, n_col_tiles = 512, 2
    tn = n // n_col_tiles
    assert tn % MXU_COLS == 0
    tiles_per_seq = seq // tm
    n_row_tiles = bsz * tiles_per_seq
    n_steps = n_col_tiles * n_row_tiles
    row = lambda j, i: (i // tiles_per_seq, i % tiles_per_seq)
    step = lambda j, i: j * n_row_tiles + i
    late_in_specs, late_out_specs, late_shapes, late_parts, late_args = [], [], [], [], []
    for w, window in late_weights:
        _, rows, cols = w.shape
        first_col, width = window or (0, cols)
        part = math.gcd(first_col, width)
        slab = max(rows // n_steps, BF16_SUBLANES)
        assert rows % slab == 0
        idx = lambda j, i, last=rows // slab - 1: jnp.minimum(step(j, i), last)
        for k in range(width // part):
            late_in_specs.append(pl.BlockSpec(
                (None, slab, part), lambda j, i, idx=idx, blk=first_col // part + k: (0, idx(j, i), blk)))
        late_parts.append(width // part)
        late_args += [w] * (width // part)
        late_out_specs.append(pl.BlockSpec((slab, width), lambda j, i, idx=idx: (idx(j, i), 0)))
        late_shapes.append(jax.ShapeDtypeStruct((rows, width), BF16))
    n_ada = w_ada.shape[-1] - ada_done
    ada_cols = n_ada // n_steps
    assert ada_cols % 128 == 0 and ada_done % ada_cols == 0
    ada_blk = lambda j, i: ada_done // ada_cols + step(j, i)
    outs = pl.pallas_call(
        functools.partial(_inproj_kernel, tuple(late_parts)),
        grid=(n_col_tiles, n_row_tiles),
        in_specs=[
            pl.BlockSpec((None, tm, d), lambda j, i: (*row(j, i), 0)),
            pl.BlockSpec((1, d), lambda j, i: (0, 0)),
            pl.BlockSpec((None, None, 1, d), lambda j, i: (i // tiles_per_seq, 0, 0, 0)),
            pl.BlockSpec((None, None, 1, d), lambda j, i: (i // tiles_per_seq, 1, 0, 0)),
            pl.BlockSpec((d, tn), lambda j, i: (0, j)),
            pl.BlockSpec(c.shape, lambda j, i: (0, 0)),
            pl.BlockSpec((None, d, ada_cols), lambda j, i: (0, 0, ada_blk(j, i))),
            pl.BlockSpec((1, ada_cols), lambda j, i: (0, ada_blk(j, i))),
            *late_in_specs,
        ],
        out_specs=[pl.BlockSpec((None, tm, tn), lambda j, i: (*row(j, i), j)),
                   pl.BlockSpec((bsz, ada_cols), lambda j, i: (0, step(j, i))), *late_out_specs],
        out_shape=[jax.ShapeDtypeStruct((bsz, seq, n), BF16), jax.ShapeDtypeStruct((bsz, n_ada), F32),
                   *late_shapes],
        compiler_params=_params(2),
        name="inproj",
    )(x, norm_g, ada4, ada4, w_bf16, c, w_ada, b_ada, *late_args)
    return outs[0], outs[1], outs[2:]


HG_GROUP = 4
HG_ROWS = HG_GROUP * HG_CHUNK
GATE_CHUNK = 256


def _mixer_kernel(n_wg, x_ref, g_ref, shift_ref, scale_ref, *refs):
    wg_refs = refs[:n_wg]
    (q_ref, f_ref, i_ref, hg_ref, lbl_ref, gn_ref, *attn_in,
     gates_ref, o_ref, oat_ref, h_ref, state_ref) = refs[n_wg:]
    c, n = HG_CHUNK, HG_ROWS
    heads = range(HG_HEADS)
    lanes = [slice(u * HG_DK, (u + 1) * HG_DK) for u in heads]

    @pl.when(pl.program_id(1) == 0)
    def _():
        state_ref[...] = jnp.zeros_like(state_ref)

    h_ref[...] = _rms_modulate(x_ref[...], g_ref[...], scale_ref[...], shift_ref[...]).astype(BF16)
    wg_cols = wg_refs[0].shape[1]
    pending = list(range(n_wg * wg_cols // GATE_CHUNK))

    def gate_chunks(count):
        for _ in range(min(count, len(pending))):
            k = pending.pop(0)
            w_ref, off = wg_refs[k * GATE_CHUNK // wg_cols], k * GATE_CHUNK % wg_cols
            gates_ref[:, k * GATE_CHUNK:(k + 1) * GATE_CHUNK] = jnp.dot(
                h_ref[...], w_ref[:, off:off + GATE_CHUNK], preferred_element_type=F32).astype(gates_ref.dtype)

    attention = _attention_steps(*attn_in, oat_ref, pl.program_id(1) == 0)
    attend = lambda: next(attention, None)

    def interleave(count):
        for _ in range(count):
            attend()
            gate_chunks(1)

    lbl = lbl_ref[...]
    e = jnp.exp(lbl - jnp.max(lbl, axis=0, keepdims=True))
    lb = e[0:1] / jnp.sum(e, axis=0, keepdims=True)
    gn = gn_ref[...]

    t = lax.broadcasted_iota(jnp.int32, (n, n), 0)
    s = lax.broadcasted_iota(jnp.int32, (n, n), 1)
    tri = ((t // c) == (s // c)) & (s <= t)
    r8 = lax.broadcasted_iota(jnp.int32, (8, n), 0)
    s8 = lax.broadcasted_iota(jnp.int32, (8, n), 1)
    in_chunk = (s8 // c) == (r8 % HG_GROUP)
    stat_rows = in_chunk & ((r8 >= HG_GROUP) | ((s8 % c) < c // 2))
    cum_ops = jnp.concatenate([tri.astype(F32), stat_rows.astype(F32)], axis=0).astype(BF16)
    per_chunk = lambda rows: jnp.concatenate(
        [jnp.broadcast_to(rows[cc:cc + 1], (c, HG_DK)) for cc in range(HG_GROUP)], axis=0)
    zeros_rows = lambda rows: jnp.zeros((rows, HG_DK), BF16)

    def chunk_blocks(arr):
        return jnp.concatenate(
            [jnp.concatenate([p for p in (zeros_rows(cc * c), arr[cc * c:(cc + 1) * c],
                                          zeros_rows(n - (cc + 1) * c)) if p.shape[0]], axis=0)
             for cc in range(HG_GROUP)], axis=1)

    attend()
    gate_chunks(3)
    ks, qss, hilo = [], [], []
    for u in heads:
        hf = f_ref[:, lanes[u]].astype(F32)
        hq = q_ref[:, lanes[u]].astype(F32)
        f = lb[:, lanes[u]] + (1.0 - lb[:, lanes[u]]) * _sigmoid(hf)
        log_f = jnp.log2(f)
        ks.append(1.0 - f)
        qss.append(_silu(hq))
        hi = log_f.astype(BF16)
        lo = (log_f - hi.astype(F32)).astype(BF16)
        hilo.append(jnp.concatenate([hi, lo], axis=1))
    rs = [jnp.dot(cum_ops, hilo[u], preferred_element_type=F32) for u in heads]
    interleave(2)
    a_s, bms, kks, qes, decs = [], [], [], [], []
    for u in heads:
        r = rs[u][:, :HG_DK] + rs[u][:, HG_DK:]
        b, b_last = r[0:n], r[n + HG_GROUP:n + 2 * HG_GROUP]
        b_rel = b - per_chunk(r[n:n + HG_GROUP])
        a_s.append((qss[u] * jnp.exp2(b_rel)).astype(BF16))
        bms.append((ks[u] * jnp.exp2(-b_rel)).astype(BF16))
        kks.append((ks[u] * jnp.exp2(per_chunk(b_last) - b)).astype(BF16))
        qes.append((qss[u] * jnp.exp2(b)).astype(BF16))
        decs.append(jnp.exp2(b_last))
    scores = [lax.dot_general(a_s[u], bms[u], (((1,), (1,)), ((), ())), preferred_element_type=F32)
              for u in heads]
    interleave(2)
    vs = [i_ref[:, lanes[u]] for u in heads]
    os = [jnp.dot(jnp.where(tri, scores[u], 0.0).astype(BF16), vs[u], preferred_element_type=F32)
          for u in heads]
    interleave(2)
    upd_ts = [jnp.dot(vs[u].astype(F32).T.astype(BF16), chunk_blocks(kks[u]), preferred_element_type=F32)
              for u in heads]
    interleave(2)
    for u in heads:
        s_t = state_ref[u]
        states = []
        for cc in range(HG_GROUP):
            states.append(s_t.astype(BF16))
            s_t = s_t * decs[u][cc:cc + 1] + upd_ts[u][:, cc * HG_DK:(cc + 1) * HG_DK]
        state_ref[u] = s_t
        os[u] = os[u] + lax.dot_general(chunk_blocks(qes[u]), jnp.concatenate(states, axis=1),
                                        (((1,), (1,)), ((), ())), preferred_element_type=F32)
    interleave(len(pending) - 2)
    for _ in attention:
        pass
    gate_chunks(len(pending))
    for u in heads:
        o = os[u]
        hg = hg_ref[:, lanes[u]].astype(F32)
        y = o * lax.rsqrt(jnp.mean(o * o, axis=-1, keepdims=True) + EPS) * gn
        o_ref[:, lanes[u]] = (y * _silu(hg)).astype(o_ref.dtype)


def _mixer(x, norm_g, ada4, w_gate_bf16, proj, lb_logits, out_norm_g, bias, sinks, q_norm_g, k_norm_g):
    bsz, seq, d = x.shape
    n_gate = w_gate_bf16.shape[-1]
    tm = HG_ROWS
    blocks_per_step = tm // BLOCK
    rows = lambda w, colblk: pl.BlockSpec((None, tm, w), lambda b, i: (b, i, colblk))
    ada_row = lambda k: pl.BlockSpec((None, None, 1, d), lambda b, i: (b, k, 0, 0))
    const = lambda shape: pl.BlockSpec(shape, lambda b, i: (0,) * len(shape))
    n_wg = 1
    w_gate = [pl.BlockSpec((d, n_gate), lambda b, i: (0, 0), pipeline_mode=pl.Buffered(1))]
    hg_base = HG_OFF // HG_WIDTH
    k_col = (AT_OFF + AT_WIDTH) // KV_WIDTH
    halo = lambda colblk: pl.BlockSpec(
        (None, BLOCK, KV_WIDTH), lambda b, i: (b, jnp.maximum(i * blocks_per_step - 1, 0), colblk))
    qg = jnp.tile(q_norm_g, (1, AT_GROUP))
    kg = jnp.tile(k_norm_g, (1, AT_KV_HEADS))
    return pl.pallas_call(
        functools.partial(_mixer_kernel, n_wg),
        grid=(bsz, seq // tm),
        in_specs=[
            rows(d, 0), const((1, d)), ada_row(0), ada_row(1), *w_gate,
            rows(HG_WIDTH, hg_base), rows(HG_WIDTH, hg_base + 1), rows(HG_WIDTH, hg_base + 2),
            rows(HG_WIDTH, hg_base + 3), const(lb_logits.shape), const((1, HG_DV)),
            rows(AT_WIDTH, AT_OFF // AT_WIDTH), rows(KV_WIDTH, k_col), rows(KV_WIDTH, k_col + 1),
            halo(k_col), halo(k_col + 1),
            pl.BlockSpec(bias.shape, lambda b, i: (0, 0, 0), pipeline_mode=pl.Buffered(1)),
            pl.BlockSpec(memory_space=pltpu.SMEM), const((1, AT_GROUP * AT_HEAD_DIM)), const((1, KV_WIDTH)),
        ],
        out_specs=[rows(n_gate, 0), rows(HG_WIDTH, 0), rows(AT_WIDTH, 0)],
        out_shape=[jax.ShapeDtypeStruct((bsz, seq, n_gate), BF16),
                   jax.ShapeDtypeStruct((bsz, seq, HG_WIDTH), BF16),
                   jax.ShapeDtypeStruct((bsz, seq, AT_WIDTH), BF16)],
        scratch_shapes=[pltpu.VMEM((tm, d), BF16), pltpu.VMEM((HG_HEADS, HG_DV, HG_DK), F32)],
        compiler_params=_params(2),
        name="mixer",
    )(x, norm_g, ada4, ada4, w_gate_bf16, proj, proj, proj, proj, lb_logits, out_norm_g,
      proj, proj, proj, proj, proj, bias, sinks, qg, kg)


def _t5_causal_bucket(n):
    nf = jnp.maximum(n, 1).astype(F32)
    large = MAX_EXACT + (jnp.log(nf / MAX_EXACT) / math.log(MAX_DISTANCE / MAX_EXACT)
                         * (N_BUCKETS - MAX_EXACT)).astype(jnp.int32)
    large = jnp.minimum(large, N_BUCKETS - 1)
    return jnp.where(n < MAX_EXACT, n, large)


def _bias_kernel(table_ref, bucket_t_ref, o_ref):
    bucket_t = bucket_t_ref[...]
    j = lax.broadcasted_iota(jnp.int32, bucket_t.shape, 0)
    i = lax.broadcasted_iota(jnp.int32, bucket_t.shape, 1)
    dist = i - j + BLOCK
    band = (dist >= 0) & (dist < WINDOW)
    for h in range(AT_HEADS):
        acc = jnp.zeros(bucket_t.shape, F32)
        for r in range(N_BUCKETS):
            acc = jnp.where(bucket_t == r, table_ref[r, h], acc)
        g = h % AT_GROUP
        o_ref[h // AT_GROUP, :, g * BLOCK:(g + 1) * BLOCK] = jnp.where(band, acc, NEG_INF)


def _attn_bias(rel_bias_table):
    i = jnp.arange(BLOCK, dtype=jnp.int32)[None, :]
    j = jnp.arange(2 * BLOCK, dtype=jnp.int32)[:, None]
    bucket_t = _t5_causal_bucket(jnp.maximum(i - j + BLOCK, 0))
    shape = (AT_KV_HEADS, 2 * BLOCK, AT_GROUP * BLOCK)
    return pl.pallas_call(
        _bias_kernel,
        in_specs=[
            pl.BlockSpec(memory_space=pltpu.SMEM),
            pl.BlockSpec((2 * BLOCK, BLOCK), lambda: (0, 0)),
        ],
        out_specs=pl.BlockSpec(shape, lambda: (0, 0, 0)),
        out_shape=jax.ShapeDtypeStruct(shape, F32),
        name="attn_bias",
    )(rel_bias_table, bucket_t)


def _head_mean_square(t, pool):
    return jnp.dot((t * t).astype(BF16), pool, preferred_element_type=F32)


def _attention_steps(q_ref, k_ref, v_ref, kh_ref, vh_ref, bias_ref, sink_ref, qg_ref, kg_ref, o_ref, first_block):
    n_blocks = q_ref.shape[0] // BLOCK
    d = AT_HEAD_DIM
    grp = AT_GROUP * d
    sub = lax.broadcasted_iota(jnp.int32, (grp, grp), 0)
    lane = lax.broadcasted_iota(jnp.int32, (grp, grp), 1)
    pool = jnp.where(sub // d == lane // d, 1.0 / d, 0.0).astype(BF16)

    k_all = jnp.concatenate([kh_ref[...], k_ref[...]], axis=0).astype(F32)
    k_n = (k_all * lax.rsqrt(_head_mean_square(k_all, pool) + EPS) * kg_ref[...]).astype(BF16)
    v_t = jnp.concatenate([vh_ref[...], v_ref[...]], axis=0).astype(F32).T.astype(BF16)

    lane_head = lax.broadcasted_iota(jnp.int32, (BLOCK, grp), 1) // d
    head_masks = [(lane_head == g).astype(BF16) for g in range(AT_GROUP)]
    pre_seq = jnp.where(first_block, NEG_INF, 0.0)

    scale = d ** -0.5
    k_reps, q_ns = [], []
    for kv in range(AT_KV_HEADS):
        spread = ((sub // d == kv) & (sub % d == lane % d)).astype(BF16)
        k_reps.append(jnp.dot(k_n, spread, preferred_element_type=F32).astype(BF16))
        q_g = q_ref[:, kv * grp:(kv + 1) * grp].astype(F32)
        q_ns.append((q_g * lax.rsqrt(_head_mean_square(q_g, pool) + EPS) * (qg_ref[...] * scale)).astype(BF16))
    yield

    def logits_t(kv, nb):
        rows = slice(nb * BLOCK, (nb + 1) * BLOCK)
        keys = slice(nb * BLOCK, (nb + 2) * BLOCK)
        q4 = jnp.concatenate([q_ns[kv][rows] * head_masks[g] for g in range(AT_GROUP)], axis=0)
        lg = lax.dot_general(k_reps[kv][keys], q4, (((1,), (1,)), ((), ())),
                             preferred_element_type=F32) + bias_ref[kv]
        if nb == 0:
            lg = jnp.concatenate([lg[:BLOCK] + pre_seq, lg[BLOCK:]], axis=0)
        return lg

    units = [(kv, nb) for kv in range(AT_KV_HEADS) for nb in range(n_blocks)]
    lg_next = logits_t(*units[0])
    yield
    for idx, (kv, nb) in enumerate(units):
        lg = lg_next
        if idx + 1 < len(units):
            lg_next = logits_t(*units[idx + 1])
            yield
        rows = slice(nb * BLOCK, (nb + 1) * BLOCK)
        keys = slice(nb * BLOCK, (nb + 2) * BLOCK)
        sink = jnp.concatenate(
            [jnp.full((1, BLOCK), sink_ref[kv * AT_GROUP + g], F32) for g in range(AT_GROUP)], axis=1)
        m = jnp.maximum(jnp.max(lg, axis=0, keepdims=True), sink)
        e = jnp.exp(lg - m)
        den = jnp.sum(e, axis=0, keepdims=True) + jnp.exp(sink - m)
        o_t = jnp.dot(v_t[kv * d:(kv + 1) * d, keys], e.astype(BF16),
                      preferred_element_type=F32) * (1.0 / den)
        o = jnp.concatenate([o_t[:, g * BLOCK:(g + 1) * BLOCK] for g in range(AT_GROUP)], axis=0).T
        o_ref[rows, kv * grp:(kv + 1) * grp] = o.astype(o_ref.dtype)
        yield


MERGE_SPLIT = 2


def _merge_kernel(x_ref, ohg_ref, oat_ref, ghg_ref, gat_ref, wbh_ref, wba_ref, wo_ref,
                  gate1_ref, n2g_ref, shift2_ref, scale2_ref, x1_ref, h2_ref):
    part = x_ref.shape[0] // MERGE_SPLIT
    parts = [slice(p * part, (p + 1) * part) for p in range(MERGE_SPLIT)]
    m_hg = [jnp.dot(ohg_ref[rows, :], wbh_ref[...], preferred_element_type=F32) for rows in parts]
    m_at = [jnp.dot(oat_ref[rows, :], wba_ref[...], preferred_element_type=F32) for rows in parts]
    ys = []
    for p, rows in enumerate(parts):
        merged = (_sigmoid(ghg_ref[rows, :].astype(F32)) * m_hg[p]
                  + _sigmoid(gat_ref[rows, :].astype(F32)) * m_at[p])
        ys.append(jnp.dot(merged.astype(BF16), wo_ref[...], preferred_element_type=F32))
    for p, rows in enumerate(parts):
        x1 = x_ref[rows, :] + gate1_ref[...] * ys[p]
        x1_ref[rows, :] = x1
        h2_ref[rows, :] = _rms_modulate(x1, n2g_ref[...], scale2_ref[...], shift2_ref[...]).astype(h2_ref.dtype)


def _merge(x, o_hg, o_at, gates, w_bh, w_ba, w_o, ada4, norm2_g):
    bsz, seq, d = x.shape
    tm = 512
    resident = lambda shape: pl.BlockSpec(shape, lambda b, i: (0, 0), pipeline_mode=pl.Buffered(1))
    ada_row = lambda k: pl.BlockSpec((None, None, 1, d), lambda b, i: (b, k, 0, 0))
    rows = lambda w, colblk: pl.BlockSpec((None, tm, w), lambda b, i: (b, i, colblk))
    return pl.pallas_call(
        _merge_kernel,
        grid=(bsz, seq // tm),
        in_specs=[
            rows(d, 0), rows(HG_WIDTH, 0), rows(AT_WIDTH, 0), rows(d, 0), rows(d, 1),
            resident(w_bh.shape), resident(w_ba.shape), resident(w_o.shape),
            ada_row(0), pl.BlockSpec((1, d), lambda b, i: (0, 0)), ada_row(1), ada_row(2),
        ],
        out_specs=[rows(d, 0), rows(d, 0)],
        out_shape=[jax.ShapeDtypeStruct((bsz, seq, d), F32), jax.ShapeDtypeStruct((bsz, seq, d), BF16)],
        compiler_params=_params(2),
        name="merge",
    )(x, o_hg, o_at, gates, gates, w_bh, w_ba, w_o, ada4, norm2_g, ada4, ada4)


def _ffn_kernel(h2_ref, w1_ref, w2_ref, x1_ref, gate2_ref, o_ref):
    j = pl.program_id(2)

    @pl.when(j == 0)
    def _():
        o_ref[...] = jnp.zeros_like(o_ref)

    hid = jnp.maximum(jnp.dot(h2_ref[...], w1_ref[...], preferred_element_type=F32), 0.0)
    o_ref[...] += jnp.dot((hid * hid).astype(BF16), w2_ref[...], preferred_element_type=F32)

    @pl.when(j == pl.num_programs(2) - 1)
    def _():
        o_ref[...] = x1_ref[...] + gate2_ref[...] * o_ref[...]


def _ffn(h2, x1, w1, w2, ada4):
    bsz, seq, d = x1.shape
    dff = w1.shape[-1]
    tm, tf = 512, 2048
    rows = lambda: pl.BlockSpec((None, tm, d), lambda b, i, j: (b, i, 0))
    return pl.pallas_call(
        _ffn_kernel,
        grid=(bsz, seq // tm, dff // tf),
        in_specs=[
            rows(),
            pl.BlockSpec((d, tf), lambda b, i, j: (0, j)),
            pl.BlockSpec((tf, d), lambda b, i, j: (j, 0)),
            rows(),
            pl.BlockSpec((None, None, 1, d), lambda b, i, j: (b, 3, 0, 0)),
        ],
        out_specs=rows(),
        out_shape=jax.ShapeDtypeStruct((bsz, seq, d), F32),
        compiler_params=_params(3),
        name="ffn",
    )(h2, w1, w2, x1, ada4)


def kernel(x, c, w_ada, b_ada, norm1_g, norm2_g, w_in, hg_lb_logits, hg_out_norm_g, q_norm_g, k_norm_g,
           attn_sinks, rel_bias_table, w_branch_hg, w_branch_attn, w_out, w_ff1, w_ff2):
    assert w_ada.shape[0] == 1, "single-layer block"
    bsz = x.shape[0]
    d = D_MODEL
    ada_in = _ada(c, w_ada, b_ada, 2 * d).reshape(bsz, 2, 1, d)
    proj, ada_rest, (w_gate, w_bh, w_ba, w_o, w1, w2) = _inproj(
        x, norm1_g, ada_in, w_in[0, :, :GATE_OFF].astype(BF16),
        ((w_in, (GATE_OFF, IN_WIDTH - GATE_OFF)), (w_branch_hg, None), (w_branch_attn, None), (w_out, None),
         (w_ff1, None), (w_ff2, None)),
        c, w_ada, b_ada, 2 * d)
    ada_out = ada_rest.reshape(bsz, 4, 1, d)
    bias = _attn_bias(rel_bias_table)
    gates, o_hg, o_at = _mixer(x, norm1_g, ada_in, w_gate, proj, hg_lb_logits, hg_out_norm_g,
                               bias, attn_sinks[0], q_norm_g, k_norm_g)
    x1, h2 = _merge(x, o_hg, o_at, gates, w_bh, w_ba, w_o, ada_out, norm2_g)
    return _ffn(h2, x1, w1, w2, ada_out)
```

```python
import functools
import math

import jax
import jax.numpy as jnp
from jax import lax
from jax.experimental import pallas as pl
from jax.experimental.pallas import tpu as pltpu

F32 = jnp.float32
BF16 = jnp.bfloat16

D_MODEL = 2048
HG_HEADS = 8
HG_DK = 128
HG_DV = 128
HG_WIDTH = HG_HEADS * HG_DK
HG_CHUNK = 64
AT_HEADS = 16
AT_KV_HEADS = 4
AT_HEAD_DIM = 64
AT_GROUP = AT_HEADS // AT_KV_HEADS
AT_WIDTH = AT_HEADS * AT_HEAD_DIM
KV_WIDTH = AT_KV_HEADS * AT_HEAD_DIM
WINDOW = 128
BLOCK = 128
N_BUCKETS = 32
MAX_EXACT = N_BUCKETS // 2
MAX_DISTANCE = 128
D_FF = 4 * D_MODEL
EPS = 1e-6
NEG_INF = -1e30

HG_OFF = 0
AT_OFF = 4 * HG_WIDTH
GATE_OFF = AT_OFF + AT_WIDTH + 2 * KV_WIDTH
IN_WIDTH = GATE_OFF + 2 * D_MODEL
MXU_COLS = 256

V7X_VMEM_LIMIT_BYTES = 60 * 1024 * 1024
BF16_SUBLANES = 16


def _params(n_axes):
    return pltpu.CompilerParams(
        dimension_semantics=("arbitrary",) * n_axes,
        vmem_limit_bytes=V7X_VMEM_LIMIT_BYTES,
    )


def _rms_modulate(x, g, scale, shift):
    r = lax.rsqrt(jnp.mean(x * x, axis=-1, keepdims=True) + EPS)
    return (x * r * g) * (1.0 + scale) + shift


def _sigmoid(x):
    return 0.5 * jnp.tanh(0.5 * x) + 0.5


def _silu(x):
    h = 0.5 * x
    return h + h * jnp.tanh(h)


def _split_bf16(x):
    hi = x.astype(BF16)
    return hi, (x - hi.astype(F32)).astype(BF16)


def _ada_columns(c_ref, w_ref, b_ref):
    c = c_ref[...]
    c_hi, c_lo = _split_bf16(c * jax.nn.sigmoid(c))
    w_hi, w_lo = _split_bf16(w_ref[...])
    dot = functools.partial(jnp.dot, preferred_element_type=F32)
    return dot(c_hi, w_hi) + dot(c_lo, w_hi) + dot(c_hi, w_lo) + b_ref[...]


def _ada_kernel(c_ref, w_ref, b_ref, o_ref):
    o_ref[...] = _ada_columns(c_ref, w_ref, b_ref)


def _ada(c, w_ada, b_ada, n):
    bsz, d = c.shape
    tn = 1024
    return pl.pallas_call(
        _ada_kernel,
        grid=(n // tn,),
        in_specs=[
            pl.BlockSpec((bsz, d), lambda j: (0, 0)),
            pl.BlockSpec((None, d, tn), lambda j: (0, 0, j)),
            pl.BlockSpec((1, tn), lambda j: (0, j)),
        ],
        out_specs=pl.BlockSpec((bsz, tn), lambda j: (0, j)),
        out_shape=jax.ShapeDtypeStruct((bsz, n), F32),
        compiler_params=_params(1),
        name="ada",
    )(c, w_ada, b_ada)


def _inproj_kernel(late_parts, x_ref, g_ref, shift_ref, scale_ref, w_ref, c_ref, wada_ref, bada_ref, *refs):
    n_in = sum(late_parts)
    late_in, (o_ref, ada_ref, *late_out) = refs[:n_in], refs[n_in:]
    h = _rms_modulate(x_ref[...], g_ref[...], scale_ref[...], shift_ref[...])
    o_ref[...] = jnp.dot(h.astype(BF16), w_ref[...], preferred_element_type=F32).astype(o_ref.dtype)
    ada_ref[...] = _ada_columns(c_ref, wada_ref, bada_ref)
    first = 0
    for n_parts, dst in zip(late_parts, late_out):
        parts = [ref[...] for ref in late_in[first:first + n_parts]]
        first += n_parts
        rows = parts[0] if n_parts == 1 else jnp.concatenate(parts, axis=1)
        if len(dst.shape) == 2:
            dst[...] = rows.astype(dst.dtype)
        else:
            ct = dst.shape[2]
            for t in range(dst.shape[0]):
                dst[t] = rows[:, t * ct:(t + 1) * ct].astype(dst.dtype)


def _inproj(x, norm_g, ada4, w_bf16, late_weights, c, w_ada, b_ada, ada_done):
    bsz, seq, d = x.shape
    n = w_bf16.shape[-1]
    tm, n_col_tiles = 512, 2
    tn = n // n_col_tiles
    assert tn % MXU_COLS == 0
    tiles_per_seq = seq // tm
    n_row_tiles = bsz * tiles_per_seq
    n_steps = n_col_tiles * n_row_tiles
    row = lambda j, i: (i // tiles_per_seq, i % tiles_per_seq)
    step = lambda j, i: j * n_row_tiles + i
    late_in_specs, late_out_specs, late_shapes, late_parts, late_args = [], [], [], [], []
    for w, window, col_tile in late_weights:
        _, rows, cols = w.shape
        first_col, width = window or (0, cols)
        part = math.gcd(first_col, width)
        slab = max(rows // n_steps, BF16_SUBLANES)
        assert rows % slab == 0
        idx = lambda j, i, last=rows // slab - 1: jnp.minimum(step(j, i), last)
        for k in range(width // part):
            late_in_specs.append(pl.BlockSpec(
                (None, slab, part), lambda j, i, idx=idx, blk=first_col // part + k: (0, idx(j, i), blk)))
        late_parts.append(width // part)
        late_args += [w] * (width // part)
        if col_tile is None:
            late_out_specs.append(pl.BlockSpec((slab, width), lambda j, i, idx=idx: (idx(j, i), 0)))
            late_shapes.append(jax.ShapeDtypeStruct((rows, width), BF16))
        else:
            late_out_specs.append(pl.BlockSpec((width // col_tile, slab, col_tile),
                                               lambda j, i, idx=idx: (0, idx(j, i), 0)))
            late_shapes.append(jax.ShapeDtypeStruct((width // col_tile, rows, col_tile), BF16))
    n_ada = w_ada.shape[-1] - ada_done
    ada_cols = n_ada // n_steps
    assert ada_cols % 128 == 0 and ada_done % ada_cols == 0
    ada_blk = lambda j, i: ada_done // ada_cols + step(j, i)
    outs = pl.pallas_call(
        functools.partial(_inproj_kernel, tuple(late_parts)),
        grid=(n_col_tiles, n_row_tiles),
        in_specs=[
            pl.BlockSpec((None, tm, d), lambda j, i: (*row(j, i), 0)),
            pl.BlockSpec((1, d), lambda j, i: (0, 0)),
            pl.BlockSpec((None, None, 1, d), lambda j, i: (i // tiles_per_seq, 0, 0, 0)),
            pl.BlockSpec((None, None, 1, d), lambda j, i: (i // tiles_per_seq, 1, 0, 0)),
            pl.BlockSpec((d, tn), lambda j, i: (0, j)),
            pl.BlockSpec(c.shape, lambda j, i: (0, 0)),
            pl.BlockSpec((None, d, ada_cols), lambda j, i: (0, 0, ada_blk(j, i))),
            pl.BlockSpec((1, ada_cols), lambda j, i: (0, ada_blk(j, i))),
            *late_in_specs,
        ],
        out_specs=[pl.BlockSpec((None, tm, tn), lambda j, i: (*row(j, i), j)),
                   pl.BlockSpec((bsz, ada_cols), lambda j, i: (0, step(j, i))), *late_out_specs],
        out_shape=[jax.ShapeDtypeStruct((bsz, seq, n), BF16), jax.ShapeDtypeStruct((bsz, n_ada), F32),
                   *late_shapes],
        compiler_params=_params(2),
        name="inproj",
    )(x, norm_g, ada4, ada4, w_bf16, c, w_ada, b_ada, *late_args)
    return outs[0], outs[1], outs[2:]


HG_GROUP = 4
HG_ROWS = HG_GROUP * HG_CHUNK
GATE_CHUNK = 256


def _mixer_kernel(x_ref, g_ref, shift_ref, scale_ref, *refs):
    (wg_ref, q_ref, f_ref, i_ref, hg_ref, lbl_ref, gn_ref, *attn_in,
     gates_ref, o_ref, oat_ref, h_ref, state_ref) = refs
    c, n = HG_CHUNK, HG_ROWS
    heads = range(HG_HEADS)
    lanes = [slice(u * HG_DK, (u + 1) * HG_DK) for u in heads]

    @pl.when(pl.program_id(1) == 0)
    def _():
        state_ref[...] = jnp.zeros_like(state_ref)

    h_ref[...] = _rms_modulate(x_ref[...], g_ref[...], scale_ref[...], shift_ref[...]).astype(BF16)
    pending = list(range(wg_ref.shape[0]))

    def gate_chunks(count):
        for _ in range(min(count, len(pending))):
            k = pending.pop(0)
            gates_ref[:, k * GATE_CHUNK:(k + 1) * GATE_CHUNK] = jnp.dot(
                h_ref[...], wg_ref[k], preferred_element_type=F32).astype(gates_ref.dtype)

    attention = _attention_steps(*attn_in, oat_ref, pl.program_id(1) == 0)
    attend = lambda: next(attention, None)

    def interleave(count):
        for _ in range(count):
            attend()
            gate_chunks(1)

    lbl = lbl_ref[...]
    e = jnp.exp(lbl - jnp.max(lbl, axis=0, keepdims=True))
    lb = e[0:1] / jnp.sum(e, axis=0, keepdims=True)
    gn = gn_ref[...]

    t = lax.broadcasted_iota(jnp.int32, (n, n), 0)
    s = lax.broadcasted_iota(jnp.int32, (n, n), 1)
    tri = ((t // c) == (s // c)) & (s <= t)
    r8 = lax.broadcasted_iota(jnp.int32, (8, n), 0)
    s8 = lax.broadcasted_iota(jnp.int32, (8, n), 1)
    in_chunk = (s8 // c) == (r8 % HG_GROUP)
    stat_rows = in_chunk & ((r8 >= HG_GROUP) | ((s8 % c) < c // 2))
    cum_ops = jnp.concatenate([tri.astype(F32), stat_rows.astype(F32)], axis=0).astype(BF16)
    per_chunk = lambda rows: jnp.concatenate(
        [jnp.broadcast_to(rows[cc:cc + 1], (c, HG_DK)) for cc in range(HG_GROUP)], axis=0)
    zeros_rows = lambda rows: jnp.zeros((rows, HG_DK), BF16)

    def chunk_blocks(arr):
        return jnp.concatenate(
            [jnp.concatenate([p for p in (zeros_rows(cc * c), arr[cc * c:(cc + 1) * c],
                                          zeros_rows(n - (cc + 1) * c)) if p.shape[0]], axis=0)
             for cc in range(HG_GROUP)], axis=1)

    attend()
    gate_chunks(3)
    ks, qss, hilo = [], [], []
    for u in heads:
        hf = f_ref[:, lanes[u]].astype(F32)
        hq = q_ref[:, lanes[u]].astype(F32)
        f = lb[:, lanes[u]] + (1.0 - lb[:, lanes[u]]) * _sigmoid(hf)
        log_f = jnp.log2(f)
        ks.append(1.0 - f)
        qss.append(_silu(hq))
        hi = log_f.astype(BF16)
        lo = (log_f - hi.astype(F32)).astype(BF16)
        hilo.append(jnp.concatenate([hi, lo], axis=1))
    rs = [jnp.dot(cum_ops, hilo[u], preferred_element_type=F32) for u in heads]
    interleave(2)
    a_s, bms, kks, qes, decs = [], [], [], [], []
    for u in heads:
        r = rs[u][:, :HG_DK] + rs[u][:, HG_DK:]
        b, b_last = r[0:n], r[n + HG_GROUP:n + 2 * HG_GROUP]
        b_rel = b - per_chunk(r[n:n + HG_GROUP])
        a_s.append((qss[u] * jnp.exp2(b_rel)).astype(BF16))
        bms.append((ks[u] * jnp.exp2(-b_rel)).astype(BF16))
        kks.append((ks[u] * jnp.exp2(per_chunk(b_last) - b)).astype(BF16))
        qes.append((qss[u] * jnp.exp2(b)).astype(BF16))
        decs.append(jnp.exp2(b_last))
    scores = [lax.dot_general(a_s[u], bms[u], (((1,), (1,)), ((), ())), preferred_element_type=F32)
              for u in heads]
    interleave(2)
    vs = [i_ref[:, lanes[u]] for u in heads]
    os = [jnp.dot(jnp.where(tri, scores[u], 0.0).astype(BF16), vs[u], preferred_element_type=F32)
          for u in heads]
    interleave(2)
    upd_ts = [jnp.dot(vs[u].astype(F32).T.astype(BF16), chunk_blocks(kks[u]), preferred_element_type=F32)
              for u in heads]
    interleave(2)
    for u in heads:
        s_t = state_ref[u]
        states = []
        for cc in range(HG_GROUP):
            states.append(s_t.astype(BF16))
            s_t = s_t * decs[u][cc:cc + 1] + upd_ts[u][:, cc * HG_DK:(cc + 1) * HG_DK]
        state_ref[u] = s_t
        os[u] = os[u] + lax.dot_general(chunk_blocks(qes[u]), jnp.concatenate(states, axis=1),
                                        (((1,), (1,)), ((), ())), preferred_element_type=F32)
    interleave(len(pending) - 2)
    for _ in attention:
        pass
    gate_chunks(len(pending))
    for u in heads:
        o = os[u]
        hg = hg_ref[:, lanes[u]].astype(F32)
        y = o * lax.rsqrt(jnp.mean(o * o, axis=-1, keepdims=True) + EPS) * gn
        o_ref[:, lanes[u]] = (y * _silu(hg)).astype(o_ref.dtype)


def _mixer(x, norm_g, ada4, w_gate_bf16, proj, lb_logits, out_norm_g, bias, sinks, q_norm_g, k_norm_g):
    bsz, seq, d = x.shape
    n_gate = w_gate_bf16.shape[0] * w_gate_bf16.shape[2]
    tm = HG_ROWS
    blocks_per_step = tm // BLOCK
    rows = lambda w, colblk: pl.BlockSpec((None, tm, w), lambda b, i: (b, i, colblk))
    ada_row = lambda k: pl.BlockSpec((None, None, 1, d), lambda b, i: (b, k, 0, 0))
    const = lambda shape: pl.BlockSpec(shape, lambda b, i: (0,) * len(shape))
    w_gate = pl.BlockSpec(w_gate_bf16.shape, lambda b, i: (0, 0, 0), pipeline_mode=pl.Buffered(1))
    hg_base = HG_OFF // HG_WIDTH
    k_col = (AT_OFF + AT_WIDTH) // KV_WIDTH
    halo = lambda colblk: pl.BlockSpec(
        (None, BLOCK, KV_WIDTH), lambda b, i: (b, jnp.maximum(i * blocks_per_step - 1, 0), colblk))
    qg = jnp.tile(q_norm_g, (1, AT_GROUP))
    kg = jnp.tile(k_norm_g, (1, AT_KV_HEADS))
    return pl.pallas_call(
        _mixer_kernel,
        grid=(bsz, seq // tm),
        in_specs=[
            rows(d, 0), const((1, d)), ada_row(0), ada_row(1), w_gate,
            rows(HG_WIDTH, hg_base), rows(HG_WIDTH, hg_base + 1), rows(HG_WIDTH, hg_base + 2),
            rows(HG_WIDTH, hg_base + 3), const(lb_logits.shape), const((1, HG_DV)),
            rows(AT_WIDTH, AT_OFF // AT_WIDTH), rows(KV_WIDTH, k_col), rows(KV_WIDTH, k_col + 1),
            halo(k_col), halo(k_col + 1),
            pl.BlockSpec(bias.shape, lambda b, i: (0, 0, 0), pipeline_mode=pl.Buffered(1)),
            pl.BlockSpec(memory_space=pltpu.SMEM), const((1, AT_GROUP * AT_HEAD_DIM)), const((1, KV_WIDTH)),
        ],
        out_specs=[rows(n_gate, 0), rows(HG_WIDTH, 0), rows(AT_WIDTH, 0)],
        out_shape=[jax.ShapeDtypeStruct((bsz, seq, n_gate), BF16),
                   jax.ShapeDtypeStruct((bsz, seq, HG_WIDTH), BF16),
                   jax.ShapeDtypeStruct((bsz, seq, AT_WIDTH), BF16)],
        scratch_shapes=[pltpu.VMEM((tm, d), BF16), pltpu.VMEM((HG_HEADS, HG_DV, HG_DK), F32)],
        compiler_params=_params(2),
        name="mixer",
    )(x, norm_g, ada4, ada4, w_gate_bf16, proj, proj, proj, proj, lb_logits, out_norm_g,
      proj, proj, proj, proj, proj, bias, sinks, qg, kg)


def _t5_causal_bucket(n):
    nf = jnp.maximum(n, 1).astype(F32)
    large = MAX_EXACT + (jnp.log(nf / MAX_EXACT) / math.log(MAX_DISTANCE / MAX_EXACT)
                         * (N_BUCKETS - MAX_EXACT)).astype(jnp.int32)
    large = jnp.minimum(large, N_BUCKETS - 1)
    return jnp.where(n < MAX_EXACT, n, large)


def _bias_kernel(table_ref, bucket_t_ref, o_ref):
    bucket_t = bucket_t_ref[...]
    j = lax.broadcasted_iota(jnp.int32, bucket_t.shape, 0)
    i = lax.broadcasted_iota(jnp.int32, bucket_t.shape, 1)
    dist = i - j + BLOCK
    band = (dist >= 0) & (dist < WINDOW)
    for h in range(AT_HEADS):
        acc = jnp.zeros(bucket_t.shape, F32)
        for r in range(N_BUCKETS):
            acc = jnp.where(bucket_t == r, table_ref[r, h], acc)
        g = h % AT_GROUP
        o_ref[h // AT_GROUP, :, g * BLOCK:(g + 1) * BLOCK] = jnp.where(band, acc, NEG_INF)


def _attn_bias(rel_bias_table):
    i = jnp.arange(BLOCK, dtype=jnp.int32)[None, :]
    j = jnp.arange(2 * BLOCK, dtype=jnp.int32)[:, None]
    bucket_t = _t5_causal_bucket(jnp.maximum(i - j + BLOCK, 0))
    shape = (AT_KV_HEADS, 2 * BLOCK, AT_GROUP * BLOCK)
    return pl.pallas_call(
        _bias_kernel,
        in_specs=[
            pl.BlockSpec(memory_space=pltpu.SMEM),
            pl.BlockSpec((2 * BLOCK, BLOCK), lambda: (0, 0)),
        ],
        out_specs=pl.BlockSpec(shape, lambda: (0, 0, 0)),
        out_shape=jax.ShapeDtypeStruct(shape, F32),
        name="attn_bias",
    )(rel_bias_table, bucket_t)


def _head_mean_square(t, pool):
    return jnp.dot((t * t).astype(BF16), pool, preferred_element_type=F32)


def _attention_steps(q_ref, k_ref, v_ref, kh_ref, vh_ref, bias_ref, sink_ref, qg_ref, kg_ref, o_ref, first_block):
    n_blocks = q_ref.shape[0] // BLOCK
    d = AT_HEAD_DIM
    grp = AT_GROUP * d
    sub = lax.broadcasted_iota(jnp.int32, (grp, grp), 0)
    lane = lax.broadcasted_iota(jnp.int32, (grp, grp), 1)
    pool = jnp.where(sub // d == lane // d, 1.0 / d, 0.0).astype(BF16)

    k_all = jnp.concatenate([kh_ref[...], k_ref[...]], axis=0).astype(F32)
    k_n = (k_all * lax.rsqrt(_head_mean_square(k_all, pool) + EPS) * kg_ref[...]).astype(BF16)
    v_t = jnp.concatenate([vh_ref[...], v_ref[...]], axis=0).astype(F32).T.astype(BF16)

    lane_head = lax.broadcasted_iota(jnp.int32, (BLOCK, grp), 1) // d
    head_masks = [(lane_head == g).astype(BF16) for g in range(AT_GROUP)]
    pre_seq = jnp.where(first_block, NEG_INF, 0.0)

    scale = d ** -0.5
    k_reps, q_ns = [], []
    for kv in range(AT_KV_HEADS):
        spread = ((sub // d == kv) & (sub % d == lane % d)).astype(BF16)
        k_reps.append(jnp.dot(k_n, spread, preferred_element_type=F32).astype(BF16))
        q_g = q_ref[:, kv * grp:(kv + 1) * grp].astype(F32)
        q_ns.append((q_g * lax.rsqrt(_head_mean_square(q_g, pool) + EPS) * (qg_ref[...] * scale)).astype(BF16))
    yield

    def logits_t(kv, nb):
        rows = slice(nb * BLOCK, (nb + 1) * BLOCK)
        keys = slice(nb * BLOCK, (nb + 2) * BLOCK)
        q4 = jnp.concatenate([q_ns[kv][rows] * head_masks[g] for g in range(AT_GROUP)], axis=0)
        lg = lax.dot_general(k_reps[kv][keys], q4, (((1,), (1,)), ((), ())),
                             preferred_element_type=F32) + bias_ref[kv]
        if nb == 0:
            lg = jnp.concatenate([lg[:BLOCK] + pre_seq, lg[BLOCK:]], axis=0)
        return lg

    units = [(kv, nb) for kv in range(AT_KV_HEADS) for nb in range(n_blocks)]
    lg_next = logits_t(*units[0])
    yield
    for idx, (kv, nb) in enumerate(units):
        lg = lg_next
        if idx + 1 < len(units):
            lg_next = logits_t(*units[idx + 1])
            yield
        rows = slice(nb * BLOCK, (nb + 1) * BLOCK)
        keys = slice(nb * BLOCK, (nb + 2) * BLOCK)
        sink = jnp.concatenate(
            [jnp.full((1, BLOCK), sink_ref[kv * AT_GROUP + g], F32) for g in range(AT_GROUP)], axis=1)
        m = jnp.maximum(jnp.max(lg, axis=0, keepdims=True), sink)
        e = jnp.exp(lg - m)
        den = jnp.sum(e, axis=0, keepdims=True) + jnp.exp(sink - m)
        o_t = jnp.dot(v_t[kv * d:(kv + 1) * d, keys], e.astype(BF16),
                      preferred_element_type=F32) * (1.0 / den)
        o = jnp.concatenate([o_t[:, g * BLOCK:(g + 1) * BLOCK] for g in range(AT_GROUP)], axis=0).T
        o_ref[rows, kv * grp:(kv + 1) * grp] = o.astype(o_ref.dtype)
        yield


MERGE_SPLIT = 2


def _merge_kernel(x_ref, ohg_ref, oat_ref, ghg_ref, gat_ref, wbh_ref, wba_ref, wo_ref,
                  gate1_ref, n2g_ref, shift2_ref, scale2_ref, x1_ref, h2_ref):
    part = x_ref.shape[0] // MERGE_SPLIT
    parts = [slice(p * part, (p + 1) * part) for p in range(MERGE_SPLIT)]
    m_hg = [jnp.dot(ohg_ref[rows, :], wbh_ref[...], preferred_element_type=F32) for rows in parts]
    m_at = [jnp.dot(oat_ref[rows, :], wba_ref[...], preferred_element_type=F32) for rows in parts]
    ys = []
    for p, rows in enumerate(parts):
        merged = (jax.nn.sigmoid(ghg_ref[rows, :].astype(F32)) * m_hg[p]
                  + jax.nn.sigmoid(gat_ref[rows, :].astype(F32)) * m_at[p])
        ys.append(jnp.dot(merged.astype(BF16), wo_ref[...], preferred_element_type=F32))
    for p, rows in enumerate(parts):
        x1 = x_ref[rows, :] + gate1_ref[...] * ys[p]
        x1_ref[rows, :] = x1
        h2_ref[rows, :] = _rms_modulate(x1, n2g_ref[...], scale2_ref[...], shift2_ref[...]).astype(h2_ref.dtype)


def _merge(x, o_hg, o_at, gates, w_bh, w_ba, w_o, ada4, norm2_g):
    bsz, seq, d = x.shape
    tm = 512
    resident = lambda shape: pl.BlockSpec(shape, lambda b, i: (0, 0), pipeline_mode=pl.Buffered(1))
    ada_row = lambda k: pl.BlockSpec((None, None, 1, d), lambda b, i: (b, k, 0, 0))
    rows = lambda w, colblk: pl.BlockSpec((None, tm, w), lambda b, i: (b, i, colblk))
    return pl.pallas_call(
        _merge_kernel,
        grid=(bsz, seq // tm),
        in_specs=[
            rows(d, 0), rows(HG_WIDTH, 0), rows(AT_WIDTH, 0), rows(d, 0), rows(d, 1),
            resident(w_bh.shape), resident(w_ba.shape), resident(w_o.shape),
            ada_row(0), pl.BlockSpec((1, d), lambda b, i: (0, 0)), ada_row(1), ada_row(2),
        ],
        out_specs=[rows(d, 0), rows(d, 0)],
        out_shape=[jax.ShapeDtypeStruct((bsz, seq, d), F32), jax.ShapeDtypeStruct((bsz, seq, d), BF16)],
        compiler_params=_params(2),
        name="merge",
    )(x, o_hg, o_at, gates, gates, w_bh, w_ba, w_o, ada4, norm2_g, ada4, ada4)


def _ffn_kernel(h2_ref, w1_ref, w2_ref, x1_ref, gate2_ref, o_ref):
    j = pl.program_id(2)

    @pl.when(j == 0)
    def _():
        o_ref[...] = jnp.zeros_like(o_ref)

    hid = jnp.maximum(jnp.dot(h2_ref[...], w1_ref[...], preferred_element_type=F32), 0.0)
    o_ref[...] += jnp.dot((hid * hid).astype(BF16), w2_ref[...], preferred_element_type=F32)

    @pl.when(j == pl.num_programs(2) - 1)
    def _():
        o_ref[...] = x1_ref[...] + gate2_ref[...] * o_ref[...]


def _ffn(h2, x1, w1, w2, ada4):
    bsz, seq, d = x1.shape
    dff = w1.shape[-1]
    tm, tf = 512, 2048
    rows = lambda: pl.BlockSpec((None, tm, d), lambda b, i, j: (b, i, 0))
    return pl.pallas_call(
        _ffn_kernel,
        grid=(bsz, seq // tm, dff // tf),
        in_specs=[
            rows(),
            pl.BlockSpec((d, tf), lambda b, i, j: (0, j)),
            pl.BlockSpec((tf, d), lambda b, i, j: (j, 0)),
            rows(),
            pl.BlockSpec((None, None, 1, d), lambda b, i, j: (b, 3, 0, 0)),
        ],
        out_specs=rows(),
        out_shape=jax.ShapeDtypeStruct((bsz, seq, d), F32),
        compiler_params=_params(3),
        name="ffn",
    )(h2, w1, w2, x1, ada4)


def kernel(x, c, w_ada, b_ada, norm1_g, norm2_g, w_in, hg_lb_logits, hg_out_norm_g, q_norm_g, k_norm_g,
           attn_sinks, rel_bias_table, w_branch_hg, w_branch_attn, w_out, w_ff1, w_ff2):
    assert w_ada.shape[0] == 1, "single-layer block"
    bsz = x.shape[0]
    d = D_MODEL
    ada_in = _ada(c, w_ada, b_ada, 2 * d).reshape(bsz, 2, 1, d)
    proj, ada_rest, (w_gate, w_bh, w_ba, w_o, w1, w2) = _inproj(
        x, norm1_g, ada_in, w_in[0, :, :GATE_OFF].astype(BF16),
        ((w_in, (GATE_OFF, IN_WIDTH - GATE_OFF), GATE_CHUNK), (w_branch_hg, None, None),
         (w_branch_attn, None, None), (w_out, None, None), (w_ff1, None, None), (w_ff2, None, None)),
        c, w_ada, b_ada, 2 * d)
    ada_out = ada_rest.reshape(bsz, 4, 1, d)
    bias = _attn_bias(rel_bias_table)
    gates, o_hg, o_at = _mixer(x, norm1_g, ada_in, w_gate, proj, hg_lb_logits, hg_out_norm_g,
                               bias, attn_sinks[0], q_norm_g, k_norm_g)
    x1, h2 = _merge(x, o_hg, o_at, gates, w_bh, w_ba, w_o, ada_out, norm2_g)
    return _ffn(h2, x1, w1, w2, ada_out)
```

```python
import functools
import math

import jax
import jax.numpy as jnp
from jax import lax
from jax.experimental import pallas as pl
from jax.experimental.pallas import tpu as pltpu

F32 = jnp.float32
BF16 = jnp.bfloat16

D_MODEL = 2048
HG_HEADS = 8
HG_DK = 128
HG_DV = 128
HG_WIDTH = HG_HEADS * HG_DK
HG_CHUNK = 64
AT_HEADS = 16
AT_KV_HEADS = 4
AT_HEAD_DIM = 64
AT_GROUP = AT_HEADS // AT_KV_HEADS
AT_WIDTH = AT_HEADS * AT_HEAD_DIM
KV_WIDTH = AT_KV_HEADS * AT_HEAD_DIM
WINDOW = 128
BLOCK = 128
N_BUCKETS = 32
MAX_EXACT = N_BUCKETS // 2
MAX_DISTANCE = 128
D_FF = 4 * D_MODEL
EPS = 1e-6
NEG_INF = -1e30

HG_OFF = 0
AT_OFF = 4 * HG_WIDTH
GATE_OFF = AT_OFF + AT_WIDTH + 2 * KV_WIDTH
IN_WIDTH = GATE_OFF + 2 * D_MODEL
MXU_COLS = 256

V7X_VMEM_LIMIT_BYTES = 60 * 1024 * 1024
BF16_SUBLANES = 16


def _params(n_axes):
    return pltpu.CompilerParams(
        dimension_semantics=("arbitrary",) * n_axes,
        vmem_limit_bytes=V7X_VMEM_LIMIT_BYTES,
    )


def _rms_modulate(x, g, scale, shift):
    r = lax.rsqrt(jnp.mean(x * x, axis=-1, keepdims=True) + EPS)
    return (x * r * g) * (1.0 + scale) + shift


def _sigmoid(x):
    return 0.5 * jnp.tanh(0.5 * x) + 0.5


def _silu(x):
    h = 0.5 * x
    return h + h * jnp.tanh(h)


def _split_bf16(x):
    hi = x.astype(BF16)
    return hi, (x - hi.astype(F32)).astype(BF16)


def _ada_columns(c_ref, w_ref, b_ref):
    c = c_ref[...]
    c_hi, c_lo = _split_bf16(c * jax.nn.sigmoid(c))
    w_hi, w_lo = _split_bf16(w_ref[...])
    dot = functools.partial(jnp.dot, preferred_element_type=F32)
    return dot(c_hi, w_hi) + dot(c_lo, w_hi) + dot(c_hi, w_lo) + b_ref[...]


def _ada_kernel(c_ref, w_ref, b_ref, o_ref):
    o_ref[...] = _ada_columns(c_ref, w_ref, b_ref)


def _ada(c, w_ada, b_ada, n):
    bsz, d = c.shape
    tn = 1024
    return pl.pallas_call(
        _ada_kernel,
        grid=(n // tn,),
        in_specs=[
            pl.BlockSpec((bsz, d), lambda j: (0, 0)),
            pl.BlockSpec((None, d, tn), lambda j: (0, 0, j)),
            pl.BlockSpec((1, tn), lambda j: (0, j)),
        ],
        out_specs=pl.BlockSpec((bsz, tn), lambda j: (0, j)),
        out_shape=jax.ShapeDtypeStruct((bsz, n), F32),
        compiler_params=_params(1),
        name="ada",
    )(c, w_ada, b_ada)


INPROJ_SPLIT = 2


def _inproj_kernel(late_parts, x_ref, g_ref, shift_ref, scale_ref, w_ref, c_ref, wada_ref, bada_ref, *refs):
    n_in = sum(late_parts)
    late_in, (o_ref, ada_ref, *late_out) = refs[:n_in], refs[n_in:]
    part = x_ref.shape[0] // INPROJ_SPLIT
    for p in range(INPROJ_SPLIT):
        rows = slice(p * part, (p + 1) * part)
        h = _rms_modulate(x_ref[rows, :], g_ref[...], scale_ref[...], shift_ref[...])
        o_ref[rows, :] = jnp.dot(h.astype(BF16), w_ref[...], preferred_element_type=F32).astype(o_ref.dtype)
    ada_ref[...] = _ada_columns(c_ref, wada_ref, bada_ref)
    first = 0
    for n_parts, dst in zip(late_parts, late_out):
        parts = [ref[...] for ref in late_in[first:first + n_parts]]
        first += n_parts
        rows = parts[0] if n_parts == 1 else jnp.concatenate(parts, axis=1)
        if len(dst.shape) == 2:
            dst[...] = rows.astype(dst.dtype)
        else:
            ct = dst.shape[2]
            for t in range(dst.shape[0]):
                dst[t] = rows[:, t * ct:(t + 1) * ct].astype(dst.dtype)


def _inproj(x, norm_g, ada4, w_bf16, late_weights, c, w_ada, b_ada, ada_done):
    bsz, seq, d = x.shape
    n = w_bf16.shape[-1]
    tm, n_col_tiles = 512, 2
    tn = n // n_col_tiles
    assert tn % MXU_COLS == 0
    tiles_per_seq = seq // tm
    n_row_tiles = bsz * tiles_per_seq
    n_steps = n_col_tiles * n_row_tiles
    row = lambda j, i: (i // tiles_per_seq, i % tiles_per_seq)
    step = lambda j, i: j * n_row_tiles + i
    late_in_specs, late_out_specs, late_shapes, late_parts, late_args = [], [], [], [], []
    for w, window, col_tile in late_weights:
        _, rows, cols = w.shape
        first_col, width = window or (0, cols)
        part = math.gcd(first_col, width)
        slab = max(rows // n_steps, BF16_SUBLANES)
        assert rows % slab == 0
        idx = lambda j, i, last=rows // slab - 1: jnp.minimum(step(j, i), last)
        for k in range(width // part):
            late_in_specs.append(pl.BlockSpec(
                (None, slab, part), lambda j, i, idx=idx, blk=first_col // part + k: (0, idx(j, i), blk)))
        late_parts.append(width // part)
        late_args += [w] * (width // part)
        if col_tile is None:
            late_out_specs.append(pl.BlockSpec((slab, width), lambda j, i, idx=idx: (idx(j, i), 0)))
            late_shapes.append(jax.ShapeDtypeStruct((rows, width), BF16))
        else:
            late_out_specs.append(pl.BlockSpec((width // col_tile, slab, col_tile),
                                               lambda j, i, idx=idx: (0, idx(j, i), 0)))
            late_shapes.append(jax.ShapeDtypeStruct((width // col_tile, rows, col_tile), BF16))
    n_ada = w_ada.shape[-1] - ada_done
    ada_cols = n_ada // n_steps
    assert ada_cols % 128 == 0 and ada_done % ada_cols == 0
    ada_blk = lambda j, i: ada_done // ada_cols + step(j, i)
    outs = pl.pallas_call(
        functools.partial(_inproj_kernel, tuple(late_parts)),
        grid=(n_col_tiles, n_row_tiles),
        in_specs=[
            pl.BlockSpec((None, tm, d), lambda j, i: (*row(j, i), 0)),
            pl.BlockSpec((1, d), lambda j, i: (0, 0)),
            pl.BlockSpec((None, None, 1, d), lambda j, i: (i // tiles_per_seq, 0, 0, 0)),
            pl.BlockSpec((None, None, 1, d), lambda j, i: (i // tiles_per_seq, 1, 0, 0)),
            pl.BlockSpec((d, tn), lambda j, i: (0, j)),
            pl.BlockSpec(c.shape, lambda j, i: (0, 0)),
            pl.BlockSpec((None, d, ada_cols), lambda j, i: (0, 0, ada_blk(j, i))),
            pl.BlockSpec((1, ada_cols), lambda j, i: (0, ada_blk(j, i))),
            *late_in_specs,
        ],
        out_specs=[pl.BlockSpec((None, tm, tn), lambda j, i: (*row(j, i), j)),
                   pl.BlockSpec((bsz, ada_cols), lambda j, i: (0, step(j, i))), *late_out_specs],
        out_shape=[jax.ShapeDtypeStruct((bsz, seq, n), BF16), jax.ShapeDtypeStruct((bsz, n_ada), F32),
                   *late_shapes],
        compiler_params=_params(2),
        name="inproj",
    )(x, norm_g, ada4, ada4, w_bf16, c, w_ada, b_ada, *late_args)
    return outs[0], outs[1], outs[2:]


HG_GROUP = 4
HG_ROWS = HG_GROUP * HG_CHUNK
GATE_CHUNK = 256


def _mixer_kernel(x_ref, g_ref, shift_ref, scale_ref, *refs):
    (wg_ref, q_ref, f_ref, i_ref, hg_ref, lbl_ref, gn_ref, *attn_in,
     gates_ref, o_ref, oat_ref, h_ref, state_ref) = refs
    c, n = HG_CHUNK, HG_ROWS
    heads = range(HG_HEADS)
    lanes = [slice(u * HG_DK, (u + 1) * HG_DK) for u in heads]

    @pl.when(pl.program_id(1) == 0)
    def _():
        state_ref[...] = jnp.zeros_like(state_ref)

    h_ref[...] = _rms_modulate(x_ref[...], g_ref[...], scale_ref[...], shift_ref[...]).astype(BF16)
    pending = list(range(wg_ref.shape[0]))

    def gate_chunks(count):
        for _ in range(min(count, len(pending))):
            k = pending.pop(0)
            gates_ref[:, k * GATE_CHUNK:(k + 1) * GATE_CHUNK] = jnp.dot(
                h_ref[...], wg_ref[k], preferred_element_type=F32).astype(gates_ref.dtype)

    attention = _attention_steps(*attn_in, oat_ref, pl.program_id(1) == 0)
    attend = lambda: next(attention, None)

    def interleave(count):
        for _ in range(count):
            attend()
            gate_chunks(1)

    lbl = lbl_ref[...]
    e = jnp.exp(lbl - jnp.max(lbl, axis=0, keepdims=True))
    lb = e[0:1] / jnp.sum(e, axis=0, keepdims=True)
    gn = gn_ref[...]

    t = lax.broadcasted_iota(jnp.int32, (n, n), 0)
    s = lax.broadcasted_iota(jnp.int32, (n, n), 1)
    tri = ((t // c) == (s // c)) & (s <= t)
    r8 = lax.broadcasted_iota(jnp.int32, (8, n), 0)
    s8 = lax.broadcasted_iota(jnp.int32, (8, n), 1)
    in_chunk = (s8 // c) == (r8 % HG_GROUP)
    stat_rows = in_chunk & ((r8 >= HG_GROUP) | ((s8 % c) < c // 2))
    cum_ops = jnp.concatenate([tri.astype(F32), stat_rows.astype(F32)], axis=0).astype(BF16)
    per_chunk = lambda rows: jnp.concatenate(
        [jnp.broadcast_to(rows[cc:cc + 1], (c, HG_DK)) for cc in range(HG_GROUP)], axis=0)
    zeros_rows = lambda rows: jnp.zeros((rows, HG_DK), BF16)

    def chunk_blocks(arr):
        return jnp.concatenate(
            [jnp.concatenate([p for p in (zeros_rows(cc * c), arr[cc * c:(cc + 1) * c],
                                          zeros_rows(n - (cc + 1) * c)) if p.shape[0]], axis=0)
             for cc in range(HG_GROUP)], axis=1)

    attend()
    gate_chunks(3)
    ks, qss, hilo = [], [], []
    for u in heads:
        hf = f_ref[:, lanes[u]].astype(F32)
        hq = q_ref[:, lanes[u]].astype(F32)
        f = lb[:, lanes[u]] + (1.0 - lb[:, lanes[u]]) * _sigmoid(hf)
        log_f = jnp.log2(f)
        ks.append(1.0 - f)
        qss.append(_silu(hq))
        hi = log_f.astype(BF16)
        lo = (log_f - hi.astype(F32)).astype(BF16)
        hilo.append(jnp.concatenate([hi, lo], axis=1))
    rs = [jnp.dot(cum_ops, hilo[u], preferred_element_type=F32) for u in heads]
    interleave(2)
    a_s, bms, kks, qes, decs = [], [], [], [], []
    for u in heads:
        r = rs[u][:, :HG_DK] + rs[u][:, HG_DK:]
        b, b_last = r[0:n], r[n + HG_GROUP:n + 2 * HG_GROUP]
        b_rel = b - per_chunk(r[n:n + HG_GROUP])
        a_s.append((qss[u] * jnp.exp2(b_rel)).astype(BF16))
        bms.append((ks[u] * jnp.exp2(-b_rel)).astype(BF16))
        kks.append((ks[u] * jnp.exp2(per_chunk(b_last) - b)).astype(BF16))
        qes.append((qss[u] * jnp.exp2(b)).astype(BF16))
        decs.append(jnp.exp2(b_last))
    scores = [lax.dot_general(a_s[u], bms[u], (((1,), (1,)), ((), ())), preferred_element_type=F32)
              for u in heads]
    interleave(2)
    vs = [i_ref[:, lanes[u]] for u in heads]
    os = [jnp.dot(jnp.where(tri, scores[u], 0.0).astype(BF16), vs[u], preferred_element_type=F32)
          for u in heads]
    interleave(2)
    upd_ts = [jnp.dot(vs[u].astype(F32).T.astype(BF16), chunk_blocks(kks[u]), preferred_element_type=F32)
              for u in heads]
    interleave(2)
    for u in heads:
        s_t = state_ref[u]
        states = []
        for cc in range(HG_GROUP):
            states.append(s_t.astype(BF16))
            s_t = s_t * decs[u][cc:cc + 1] + upd_ts[u][:, cc * HG_DK:(cc + 1) * HG_DK]
        state_ref[u] = s_t
        os[u] = os[u] + lax.dot_general(chunk_blocks(qes[u]), jnp.concatenate(states, axis=1),
                                        (((1,), (1,)), ((), ())), preferred_element_type=F32)
    interleave(len(pending) - 2)
    for _ in attention:
        pass
    gate_chunks(len(pending))
    for u in heads:
        o = os[u]
        hg = hg_ref[:, lanes[u]].astype(F32)
        y = o * lax.rsqrt(jnp.mean(o * o, axis=-1, keepdims=True) + EPS) * gn
        o_ref[:, lanes[u]] = (y * _silu(hg)).astype(o_ref.dtype)


def _mixer(x, norm_g, ada4, w_gate_bf16, proj, lb_logits, out_norm_g, bias, sinks, q_norm_g, k_norm_g):
    bsz, seq, d = x.shape
    n_gate = w_gate_bf16.shape[0] * w_gate_bf16.shape[2]
    tm = HG_ROWS
    blocks_per_step = tm // BLOCK
    rows = lambda w, colblk: pl.BlockSpec((None, tm, w), lambda b, i: (b, i, colblk))
    ada_row = lambda k: pl.BlockSpec((None, None, 1, d), lambda b, i: (b, k, 0, 0))
    const = lambda shape: pl.BlockSpec(shape, lambda b, i: (0,) * len(shape))
    w_gate = pl.BlockSpec(w_gate_bf16.shape, lambda b, i: (0, 0, 0), pipeline_mode=pl.Buffered(1))
    hg_base = HG_OFF // HG_WIDTH
    k_col = (AT_OFF + AT_WIDTH) // KV_WIDTH
    halo = lambda colblk: pl.BlockSpec(
        (None, BLOCK, KV_WIDTH), lambda b, i: (b, jnp.maximum(i * blocks_per_step - 1, 0), colblk))
    qg = jnp.tile(q_norm_g, (1, AT_GROUP))
    kg = jnp.tile(k_norm_g, (1, AT_KV_HEADS))
    return pl.pallas_call(
        _mixer_kernel,
        grid=(bsz, seq // tm),
        in_specs=[
            rows(d, 0), const((1, d)), ada_row(0), ada_row(1), w_gate,
            rows(HG_WIDTH, hg_base), rows(HG_WIDTH, hg_base + 1), rows(HG_WIDTH, hg_base + 2),
            rows(HG_WIDTH, hg_base + 3), const(lb_logits.shape), const((1, HG_DV)),
            rows(AT_WIDTH, AT_OFF // AT_WIDTH), rows(KV_WIDTH, k_col), rows(KV_WIDTH, k_col + 1),
            halo(k_col), halo(k_col + 1),
            pl.BlockSpec(bias.shape, lambda b, i: (0, 0, 0), pipeline_mode=pl.Buffered(1)),
            pl.BlockSpec(memory_space=pltpu.SMEM), const((1, AT_GROUP * AT_HEAD_DIM)), const((1, KV_WIDTH)),
        ],
        out_specs=[rows(n_gate, 0), rows(HG_WIDTH, 0), rows(AT_WIDTH, 0)],
        out_shape=[jax.ShapeDtypeStruct((bsz, seq, n_gate), BF16),
                   jax.ShapeDtypeStruct((bsz, seq, HG_WIDTH), BF16),
                   jax.ShapeDtypeStruct((bsz, seq, AT_WIDTH), BF16)],
        scratch_shapes=[pltpu.VMEM((tm, d), BF16), pltpu.VMEM((HG_HEADS, HG_DV, HG_DK), F32)],
        compiler_params=_params(2),
        name="mixer",
    )(x, norm_g, ada4, ada4, w_gate_bf16, proj, proj, proj, proj, lb_logits, out_norm_g,
      proj, proj, proj, proj, proj, bias, sinks, qg, kg)


def _t5_causal_bucket(n):
    nf = jnp.maximum(n, 1).astype(F32)
    large = MAX_EXACT + (jnp.log(nf / MAX_EXACT) / math.log(MAX_DISTANCE / MAX_EXACT)
                         * (N_BUCKETS - MAX_EXACT)).astype(jnp.int32)
    large = jnp.minimum(large, N_BUCKETS - 1)
    return jnp.where(n < MAX_EXACT, n, large)


def _bias_kernel(table_ref, bucket_t_ref, o_ref):
    bucket_t = bucket_t_ref[...]
    j = lax.broadcasted_iota(jnp.int32, bucket_t.shape, 0)
    i = lax.broadcasted_iota(jnp.int32, bucket_t.shape, 1)
    dist = i - j + BLOCK
    band = (dist >= 0) & (dist < WINDOW)
    for h in range(AT_HEADS):
        acc = jnp.zeros(bucket_t.shape, F32)
        for r in range(N_BUCKETS):
            acc = jnp.where(bucket_t == r, table_ref[r, h], acc)
        g = h % AT_GROUP
        o_ref[h // AT_GROUP, :, g * BLOCK:(g + 1) * BLOCK] = jnp.where(band, acc, NEG_INF)


def _attn_bias(rel_bias_table):
    i = jnp.arange(BLOCK, dtype=jnp.int32)[None, :]
    j = jnp.arange(2 * BLOCK, dtype=jnp.int32)[:, None]
    bucket_t = _t5_causal_bucket(jnp.maximum(i - j + BLOCK, 0))
    shape = (AT_KV_HEADS, 2 * BLOCK, AT_GROUP * BLOCK)
    return pl.pallas_call(
        _bias_kernel,
        in_specs=[
            pl.BlockSpec(memory_space=pltpu.SMEM),
            pl.BlockSpec((2 * BLOCK, BLOCK), lambda: (0, 0)),
        ],
        out_specs=pl.BlockSpec(shape, lambda: (0, 0, 0)),
        out_shape=jax.ShapeDtypeStruct(shape, F32),
        name="attn_bias",
    )(rel_bias_table, bucket_t)


def _head_mean_square(t, pool):
    return jnp.dot((t * t).astype(BF16), pool, preferred_element_type=F32)


def _attention_steps(q_ref, k_ref, v_ref, kh_ref, vh_ref, bias_ref, sink_ref, qg_ref, kg_ref, o_ref, first_block):
    n_blocks = q_ref.shape[0] // BLOCK
    d = AT_HEAD_DIM
    grp = AT_GROUP * d
    sub = lax.broadcasted_iota(jnp.int32, (grp, grp), 0)
    lane = lax.broadcasted_iota(jnp.int32, (grp, grp), 1)
    pool = jnp.where(sub // d == lane // d, 1.0 / d, 0.0).astype(BF16)

    k_all = jnp.concatenate([kh_ref[...], k_ref[...]], axis=0).astype(F32)
    k_n = (k_all * lax.rsqrt(_head_mean_square(k_all, pool) + EPS) * kg_ref[...]).astype(BF16)
    v_t = jnp.concatenate([vh_ref[...], v_ref[...]], axis=0).astype(F32).T.astype(BF16)

    lane_head = lax.broadcasted_iota(jnp.int32, (BLOCK, grp), 1) // d
    head_masks = [(lane_head == g).astype(BF16) for g in range(AT_GROUP)]
    pre_seq = jnp.where(first_block, NEG_INF, 0.0)

    scale = d ** -0.5
    k_reps, q_ns = [], []
    for kv in range(AT_KV_HEADS):
        spread = ((sub // d == kv) & (sub % d == lane % d)).astype(BF16)
        k_reps.append(jnp.dot(k_n, spread, preferred_element_type=F32).astype(BF16))
        q_g = q_ref[:, kv * grp:(kv + 1) * grp].astype(F32)
        q_ns.append((q_g * lax.rsqrt(_head_mean_square(q_g, pool) + EPS) * (qg_ref[...] * scale)).astype(BF16))
    yield

    def logits_t(kv, nb):
        rows = slice(nb * BLOCK, (nb + 1) * BLOCK)
        keys = slice(nb * BLOCK, (nb + 2) * BLOCK)
        q4 = jnp.concatenate([q_ns[kv][rows] * head_masks[g] for g in range(AT_GROUP)], axis=0)
        lg = lax.dot_general(k_reps[kv][keys], q4, (((1,), (1,)), ((), ())),
                             preferred_element_type=F32) + bias_ref[kv]
        if nb == 0:
            lg = jnp.concatenate([lg[:BLOCK] + pre_seq, lg[BLOCK:]], axis=0)
        return lg

    units = [(kv, nb) for kv in range(AT_KV_HEADS) for nb in range(n_blocks)]
    lg_next = logits_t(*units[0])
    yield
    for idx, (kv, nb) in enumerate(units):
        lg = lg_next
        if idx + 1 < len(units):
            lg_next = logits_t(*units[idx + 1])
            yield
        rows = slice(nb * BLOCK, (nb + 1) * BLOCK)
        keys = slice(nb * BLOCK, (nb + 2) * BLOCK)
        sink = jnp.concatenate(
            [jnp.full((1, BLOCK), sink_ref[kv * AT_GROUP + g], F32) for g in range(AT_GROUP)], axis=1)
        m = jnp.maximum(jnp.max(lg, axis=0, keepdims=True), sink)
        e = jnp.exp(lg - m)
        den = jnp.sum(e, axis=0, keepdims=True) + jnp.exp(sink - m)
        o_t = jnp.dot(v_t[kv * d:(kv + 1) * d, keys], e.astype(BF16),
                      preferred_element_type=F32) * (1.0 / den)
        o = jnp.concatenate([o_t[:, g * BLOCK:(g + 1) * BLOCK] for g in range(AT_GROUP)], axis=0).T
        o_ref[rows, kv * grp:(kv + 1) * grp] = o.astype(o_ref.dtype)
        yield


MERGE_SPLIT = 2


def _merge_kernel(x_ref, ohg_ref, oat_ref, ghg_ref, gat_ref, wbh_ref, wba_ref, wo_ref,
                  gate1_ref, n2g_ref, shift2_ref, scale2_ref, x1_ref, h2_ref):
    part = x_ref.shape[0] // MERGE_SPLIT
    parts = [slice(p * part, (p + 1) * part) for p in range(MERGE_SPLIT)]
    m_hg = [jnp.dot(ohg_ref[rows, :], wbh_ref[...], preferred_element_type=F32) for rows in parts]
    m_at = [jnp.dot(oat_ref[rows, :], wba_ref[...], preferred_element_type=F32) for rows in parts]
    ys = []
    for p, rows in enumerate(parts):
        merged = (jax.nn.sigmoid(ghg_ref[rows, :].astype(F32)) * m_hg[p]
                  + jax.nn.sigmoid(gat_ref[rows, :].astype(F32)) * m_at[p])
        ys.append(jnp.dot(merged.astype(BF16), wo_ref[...], preferred_element_type=F32))
    for p, rows in enumerate(parts):
        x1 = x_ref[rows, :] + gate1_ref[...] * ys[p]
        x1_ref[rows, :] = x1
        h2_ref[rows, :] = _rms_modulate(x1, n2g_ref[...], scale2_ref[...], shift2_ref[...]).astype(h2_ref.dtype)


def _merge(x, o_hg, o_at, gates, w_bh, w_ba, w_o, ada4, norm2_g):
    bsz, seq, d = x.shape
    tm = 512
    resident = lambda shape: pl.BlockSpec(shape, lambda b, i: (0, 0), pipeline_mode=pl.Buffered(1))
    ada_row = lambda k: pl.BlockSpec((None, None, 1, d), lambda b, i: (b, k, 0, 0))
    rows = lambda w, colblk: pl.BlockSpec((None, tm, w), lambda b, i: (b, i, colblk))
    return pl.pallas_call(
        _merge_kernel,
        grid=(bsz, seq // tm),
        in_specs=[
            rows(d, 0), rows(HG_WIDTH, 0), rows(AT_WIDTH, 0), rows(d, 0), rows(d, 1),
            resident(w_bh.shape), resident(w_ba.shape), resident(w_o.shape),
            ada_row(0), pl.BlockSpec((1, d), lambda b, i: (0, 0)), ada_row(1), ada_row(2),
        ],
        out_specs=[rows(d, 0), rows(d, 0)],
        out_shape=[jax.ShapeDtypeStruct((bsz, seq, d), F32), jax.ShapeDtypeStruct((bsz, seq, d), BF16)],
        compiler_params=_params(2),
        name="merge",
    )(x, o_hg, o_at, gates, gates, w_bh, w_ba, w_o, ada4, norm2_g, ada4, ada4)


def _ffn_kernel(h2_ref, w1_ref, w2_ref, x1_ref, gate2_ref, o_ref):
    j = pl.program_id(2)

    @pl.when(j == 0)
    def _():
        o_ref[...] = jnp.zeros_like(o_ref)

    hid = jnp.maximum(jnp.dot(h2_ref[...], w1_ref[...], preferred_element_type=F32), 0.0)
    o_ref[...] += jnp.dot((hid * hid).astype(BF16), w2_ref[...], preferred_element_type=F32)

    @pl.when(j == pl.num_programs(2) - 1)
    def _():
        o_ref[...] = x1_ref[...] + gate2_ref[...] * o_ref[...]


def _ffn(h2, x1, w1, w2, ada4):
    bsz, seq, d = x1.shape
    dff = w1.shape[-1]
    tm, tf = 512, 2048
    rows = lambda: pl.BlockSpec((None, tm, d), lambda b, i, j: (b, i, 0))
    return pl.pallas_call(
        _ffn_kernel,
        grid=(bsz, seq // tm, dff // tf),
        in_specs=[
            rows(),
            pl.BlockSpec((d, tf), lambda b, i, j: (0, j)),
            pl.BlockSpec((tf, d), lambda b, i, j: (j, 0)),
            rows(),
            pl.BlockSpec((None, None, 1, d), lambda b, i, j: (b, 3, 0, 0)),
        ],
        out_specs=rows(),
        out_shape=jax.ShapeDtypeStruct((bsz, seq, d), F32),
        compiler_params=_params(3),
        name="ffn",
    )(h2, w1, w2, x1, ada4)


def kernel(x, c, w_ada, b_ada, norm1_g, norm2_g, w_in, hg_lb_logits, hg_out_norm_g, q_norm_g, k_norm_g,
           attn_sinks, rel_bias_table, w_branch_hg, w_branch_attn, w_out, w_ff1, w_ff2):
    assert w_ada.shape[0] == 1, "single-layer block"
    bsz = x.shape[0]
    d = D_MODEL
    ada_in = _ada(c, w_ada, b_ada, 2 * d).reshape(bsz, 2, 1, d)
    proj, ada_rest, (w_gate, w_bh, w_ba, w_o, w1, w2) = _inproj(
        x, norm1_g, ada_in, w_in[0, :, :GATE_OFF].astype(BF16),
        ((w_in, (GATE_OFF, IN_WIDTH - GATE_OFF), GATE_CHUNK), (w_branch_hg, None, None),
         (w_branch_attn, None, None), (w_out, None, None), (w_ff1, None, None), (w_ff2, None, None)),
        c, w_ada, b_ada, 2 * d)
    ada_out = ada_rest.reshape(bsz, 4, 1, d)
    bias = _attn_bias(rel_bias_table)
    gates, o_hg, o_at = _mixer(x, norm1_g, ada_in, w_gate, proj, hg_lb_logits, hg_out_norm_g,
                               bias, attn_sinks[0], q_norm_g, k_norm_g)
    x1, h2 = _merge(x, o_hg, o_at, gates, w_bh, w_ba, w_o, ada_out, norm2_g)
    return _ffn(h2, x1, w1, w2, ada_out)
```

```python
import functools
import math

import jax
import jax.numpy as jnp
from jax import lax
from jax.experimental import pallas as pl
from jax.experimental.pallas import tpu as pltpu

F32 = jnp.float32
BF16 = jnp.bfloat16

D_MODEL = 2048
HG_HEADS = 8
HG_DK = 128
HG_DV = 128
HG_WIDTH = HG_HEADS * HG_DK
HG_CHUNK = 64
AT_HEADS = 16
AT_KV_HEADS = 4
AT_HEAD_DIM = 64
AT_GROUP = AT_HEADS // AT_KV_HEADS
AT_WIDTH = AT_HEADS * AT_HEAD_DIM
KV_WIDTH = AT_KV_HEADS * AT_HEAD_DIM
WINDOW = 128
BLOCK = 128
N_BUCKETS = 32
MAX_EXACT = N_BUCKETS // 2
MAX_DISTANCE = 128
D_FF = 4 * D_MODEL
EPS = 1e-6
NEG_INF = -1e30

HG_OFF = 0
AT_OFF = 4 * HG_WIDTH
GATE_OFF = AT_OFF + AT_WIDTH + 2 * KV_WIDTH
IN_WIDTH = GATE_OFF + 2 * D_MODEL
MXU_COLS = 256

V7X_VMEM_LIMIT_BYTES = 60 * 1024 * 1024
BF16_SUBLANES = 16


def _params(n_axes):
    return pltpu.CompilerParams(
        dimension_semantics=("arbitrary",) * n_axes,
        vmem_limit_bytes=V7X_VMEM_LIMIT_BYTES,
    )


def _rms_modulate(x, g, scale, shift):
    r = lax.rsqrt(jnp.mean(x * x, axis=-1, keepdims=True) + EPS)
    return (x * r * g) * (1.0 + scale) + shift


def _sigmoid(x):
    return 0.5 * jnp.tanh(0.5 * x) + 0.5


def _silu(x):
    h = 0.5 * x
    return h + h * jnp.tanh(h)


def _split_bf16(x):
    hi = x.astype(BF16)
    return hi, (x - hi.astype(F32)).astype(BF16)


def _ada_columns(c_ref, w_ref, b_ref):
    c = c_ref[...]
    c_hi, c_lo = _split_bf16(c * jax.nn.sigmoid(c))
    w_hi, w_lo = _split_bf16(w_ref[...])
    dot = functools.partial(jnp.dot, preferred_element_type=F32)
    return dot(c_hi, w_hi) + dot(c_lo, w_hi) + dot(c_hi, w_lo) + b_ref[...]


def _ada_kernel(c_ref, w_ref, b_ref, o_ref):
    o_ref[...] = _ada_columns(c_ref, w_ref, b_ref)


def _ada(c, w_ada, b_ada, n):
    bsz, d = c.shape
    tn = 1024
    return pl.pallas_call(
        _ada_kernel,
        grid=(n // tn,),
        in_specs=[
            pl.BlockSpec((bsz, d), lambda j: (0, 0)),
            pl.BlockSpec((None, d, tn), lambda j: (0, 0, j)),
            pl.BlockSpec((1, tn), lambda j: (0, j)),
        ],
        out_specs=pl.BlockSpec((bsz, tn), lambda j: (0, j)),
        out_shape=jax.ShapeDtypeStruct((bsz, n), F32),
        compiler_params=_params(1),
        name="ada",
    )(c, w_ada, b_ada)


INPROJ_SPLIT = 2


def _inproj_kernel(late_parts, x_ref, g_ref, shift_ref, scale_ref, w_ref, c_ref, wada_ref, bada_ref, *refs):
    n_in = sum(late_parts)
    late_in, (o_ref, ada_ref, *late_out) = refs[:n_in], refs[n_in:]
    part = x_ref.shape[0] // INPROJ_SPLIT
    for p in range(INPROJ_SPLIT):
        rows = slice(p * part, (p + 1) * part)
        h = _rms_modulate(x_ref[rows, :], g_ref[...], scale_ref[...], shift_ref[...])
        o_ref[rows, :] = jnp.dot(h.astype(BF16), w_ref[...], preferred_element_type=F32).astype(o_ref.dtype)
    ada_ref[...] = _ada_columns(c_ref, wada_ref, bada_ref)
    first = 0
    for n_parts, dst in zip(late_parts, late_out):
        parts = [ref[...] for ref in late_in[first:first + n_parts]]
        first += n_parts
        rows = parts[0] if n_parts == 1 else jnp.concatenate(parts, axis=1)
        if len(dst.shape) == 2:
            dst[...] = rows.astype(dst.dtype)
        else:
            ct = dst.shape[2]
            for t in range(dst.shape[0]):
                dst[t] = rows[:, t * ct:(t + 1) * ct].astype(dst.dtype)


def _inproj(x, norm_g, ada4, w_bf16, late_weights, c, w_ada, b_ada, ada_done):
    bsz, seq, d = x.shape
    n = w_bf16.shape[-1]
    tm, n_col_tiles = 512, 2
    tn = n // n_col_tiles
    assert tn % MXU_COLS == 0
    tiles_per_seq = seq // tm
    n_row_tiles = bsz * tiles_per_seq
    n_steps = n_col_tiles * n_row_tiles
    row = lambda j, i: (i // tiles_per_seq, i % tiles_per_seq)
    step = lambda j, i: j * n_row_tiles + i
    late_in_specs, late_out_specs, late_shapes, late_parts, late_args = [], [], [], [], []
    for w, window, col_tile in late_weights:
        _, rows, cols = w.shape
        first_col, width = window or (0, cols)
        part = math.gcd(first_col, width)
        slab = max(rows // n_steps, BF16_SUBLANES)
        assert rows % slab == 0
        idx = lambda j, i, last=rows // slab - 1: jnp.minimum(step(j, i), last)
        for k in range(width // part):
            late_in_specs.append(pl.BlockSpec(
                (None, slab, part), lambda j, i, idx=idx, blk=first_col // part + k: (0, idx(j, i), blk)))
        late_parts.append(width // part)
        late_args += [w] * (width // part)
        if col_tile is None:
            late_out_specs.append(pl.BlockSpec((slab, width), lambda j, i, idx=idx: (idx(j, i), 0)))
            late_shapes.append(jax.ShapeDtypeStruct((rows, width), BF16))
        else:
            late_out_specs.append(pl.BlockSpec((width // col_tile, slab, col_tile),
                                               lambda j, i, idx=idx: (0, idx(j, i), 0)))
            late_shapes.append(jax.ShapeDtypeStruct((width // col_tile, rows, col_tile), BF16))
    n_ada = w_ada.shape[-1] - ada_done
    ada_cols = n_ada // n_steps
    assert ada_cols % 128 == 0 and ada_done % ada_cols == 0
    ada_blk = lambda j, i: ada_done // ada_cols + step(j, i)
    outs = pl.pallas_call(
        functools.partial(_inproj_kernel, tuple(late_parts)),
        grid=(n_col_tiles, n_row_tiles),
        in_specs=[
            pl.BlockSpec((None, tm, d), lambda j, i: (*row(j, i), 0)),
            pl.BlockSpec((1, d), lambda j, i: (0, 0)),
            pl.BlockSpec((None, None, 1, d), lambda j, i: (i // tiles_per_seq, 0, 0, 0)),
            pl.BlockSpec((None, None, 1, d), lambda j, i: (i // tiles_per_seq, 1, 0, 0)),
            pl.BlockSpec((d, tn), lambda j, i: (0, j)),
            pl.BlockSpec(c.shape, lambda j, i: (0, 0)),
            pl.BlockSpec((None, d, ada_cols), lambda j, i: (0, 0, ada_blk(j, i))),
            pl.BlockSpec((1, ada_cols), lambda j, i: (0, ada_blk(j, i))),
            *late_in_specs,
        ],
        out_specs=[pl.BlockSpec((None, tm, tn), lambda j, i: (*row(j, i), j)),
                   pl.BlockSpec((bsz, ada_cols), lambda j, i: (0, step(j, i))), *late_out_specs],
        out_shape=[jax.ShapeDtypeStruct((bsz, seq, n), BF16), jax.ShapeDtypeStruct((bsz, n_ada), F32),
                   *late_shapes],
        compiler_params=_params(2),
        name="inproj",
    )(x, norm_g, ada4, ada4, w_bf16, c, w_ada, b_ada, *late_args)
    return outs[0], outs[1], outs[2:]


HG_GROUP = 4
HG_ROWS = HG_GROUP * HG_CHUNK
GATE_CHUNK = 256


def _mixer_kernel(x_ref, g_ref, shift_ref, scale_ref, *refs):
    (wg_ref, q_ref, f_ref, i_ref, hg_ref, lbl_ref, gn_ref, *attn_in,
     gates_ref, o_ref, oat_ref, h_ref, state_ref) = refs
    c, n = HG_CHUNK, HG_ROWS
    heads = range(HG_HEADS)
    lanes = [slice(u * HG_DK, (u + 1) * HG_DK) for u in heads]

    @pl.when(pl.program_id(1) == 0)
    def _():
        state_ref[...] = jnp.zeros_like(state_ref)

    h_ref[...] = _rms_modulate(x_ref[...], g_ref[...], scale_ref[...], shift_ref[...]).astype(BF16)
    pending = list(range(wg_ref.shape[0]))

    def gate_chunks(count):
        for _ in range(min(count, len(pending))):
            k = pending.pop(0)
            gates_ref[:, k * GATE_CHUNK:(k + 1) * GATE_CHUNK] = jnp.dot(
                h_ref[...], wg_ref[k], preferred_element_type=F32).astype(gates_ref.dtype)

    attention = _attention_steps(*attn_in, oat_ref, pl.program_id(1) == 0)
    attend = lambda: next(attention, None)

    def interleave(count):
        for _ in range(count):
            attend()
            gate_chunks(1)

    lbl = lbl_ref[...]
    e = jnp.exp(lbl - jnp.max(lbl, axis=0, keepdims=True))
    lb = e[0:1] / jnp.sum(e, axis=0, keepdims=True)
    gn = gn_ref[...]

    t = lax.broadcasted_iota(jnp.int32, (n, n), 0)
    s = lax.broadcasted_iota(jnp.int32, (n, n), 1)
    tri = ((t // c) == (s // c)) & (s <= t)
    tri_t = ((t // c) == (s // c)) & (t <= s)
    r8 = lax.broadcasted_iota(jnp.int32, (8, n), 0)
    s8 = lax.broadcasted_iota(jnp.int32, (8, n), 1)
    in_chunk = (s8 // c) == (r8 % HG_GROUP)
    stat_rows = in_chunk & ((r8 >= HG_GROUP) | ((s8 % c) < c // 2))
    cum_ops = jnp.concatenate([tri.astype(F32), stat_rows.astype(F32)], axis=0).astype(BF16)
    per_chunk = lambda rows: jnp.concatenate(
        [jnp.broadcast_to(rows[cc:cc + 1], (c, HG_DK)) for cc in range(HG_GROUP)], axis=0)
    zeros_rows = lambda rows: jnp.zeros((rows, HG_DK), BF16)
    col_chunk = lax.broadcasted_iota(jnp.int32, (HG_DK, n), 1) // c
    col_masks = [(col_chunk == cc).astype(BF16) for cc in range(HG_GROUP)]

    def chunk_blocks(arr):
        return jnp.concatenate(
            [jnp.concatenate([p for p in (zeros_rows(cc * c), arr[cc * c:(cc + 1) * c],
                                          zeros_rows(n - (cc + 1) * c)) if p.shape[0]], axis=0)
             for cc in range(HG_GROUP)], axis=1)

    attend()
    gate_chunks(3)
    ks, qss, hilo = [], [], []
    for u in heads:
        hf = f_ref[:, lanes[u]].astype(F32)
        hq = q_ref[:, lanes[u]].astype(F32)
        f = lb[:, lanes[u]] + (1.0 - lb[:, lanes[u]]) * _sigmoid(hf)
        log_f = jnp.log2(f)
        ks.append(1.0 - f)
        qss.append(_silu(hq))
        hi = log_f.astype(BF16)
        lo = (log_f - hi.astype(F32)).astype(BF16)
        hilo.append(jnp.concatenate([hi, lo], axis=1))
    rs = [jnp.dot(cum_ops, hilo[u], preferred_element_type=F32) for u in heads]
    interleave(2)
    a_s, bms, kks, qes, decs = [], [], [], [], []
    for u in heads:
        r = rs[u][:, :HG_DK] + rs[u][:, HG_DK:]
        b, b_last = r[0:n], r[n + HG_GROUP:n + 2 * HG_GROUP]
        b_rel = b - per_chunk(r[n:n + HG_GROUP])
        a_s.append((qss[u] * jnp.exp2(b_rel)).astype(BF16))
        bms.append((ks[u] * jnp.exp2(-b_rel)).astype(BF16))
        kks.append((ks[u] * jnp.exp2(per_chunk(b_last) - b)).astype(BF16))
        qes.append((qss[u] * jnp.exp2(b)).T.astype(BF16))
        decs.append(jnp.exp2(b_last))
    scores_t = [lax.dot_general(bms[u], a_s[u], (((1,), (1,)), ((), ())), preferred_element_type=F32)
                for u in heads]
    interleave(2)
    v_ts = [i_ref[:, lanes[u]].astype(F32).T.astype(BF16) for u in heads]
    os_t = [jnp.dot(v_ts[u], jnp.where(tri_t, scores_t[u], 0.0).astype(BF16), preferred_element_type=F32)
            for u in heads]
    interleave(2)
    upd_ts = [jnp.dot(v_ts[u], chunk_blocks(kks[u]), preferred_element_type=F32) for u in heads]
    interleave(2)
    for u in heads:
        s_t = state_ref[u]
        states = []
        for cc in range(HG_GROUP):
            states.append(s_t.astype(BF16))
            s_t = s_t * decs[u][cc:cc + 1] + upd_ts[u][:, cc * HG_DK:(cc + 1) * HG_DK]
        state_ref[u] = s_t
        qe_rows = jnp.concatenate([qes[u] * col_masks[cc] for cc in range(HG_GROUP)], axis=0)
        os_t[u] = os_t[u] + jnp.dot(jnp.concatenate(states, axis=1), qe_rows, preferred_element_type=F32)
    interleave(len(pending) - 2)
    for _ in attention:
        pass
    gate_chunks(len(pending))
    for u in heads:
        o = os_t[u].T
        hg = hg_ref[:, lanes[u]].astype(F32)
        y = o * lax.rsqrt(jnp.mean(o * o, axis=-1, keepdims=True) + EPS) * gn
        o_ref[:, lanes[u]] = (y * _silu(hg)).astype(o_ref.dtype)


def _mixer(x, norm_g, ada4, w_gate_bf16, proj, lb_logits, out_norm_g, bias, sinks, q_norm_g, k_norm_g):
    bsz, seq, d = x.shape
    n_gate = w_gate_bf16.shape[0] * w_gate_bf16.shape[2]
    tm = HG_ROWS
    blocks_per_step = tm // BLOCK
    rows = lambda w, colblk: pl.BlockSpec((None, tm, w), lambda b, i: (b, i, colblk))
    ada_row = lambda k: pl.BlockSpec((None, None, 1, d), lambda b, i: (b, k, 0, 0))
    const = lambda shape: pl.BlockSpec(shape, lambda b, i: (0,) * len(shape))
    w_gate = pl.BlockSpec(w_gate_bf16.shape, lambda b, i: (0, 0, 0), pipeline_mode=pl.Buffered(1))
    hg_base = HG_OFF // HG_WIDTH
    k_col = (AT_OFF + AT_WIDTH) // KV_WIDTH
    halo = lambda colblk: pl.BlockSpec(
        (None, BLOCK, KV_WIDTH), lambda b, i: (b, jnp.maximum(i * blocks_per_step - 1, 0), colblk))
    qg = jnp.tile(q_norm_g, (1, AT_GROUP))
    kg = jnp.tile(k_norm_g, (1, AT_KV_HEADS))
    return pl.pallas_call(
        _mixer_kernel,
        grid=(bsz, seq // tm),
        in_specs=[
            rows(d, 0), const((1, d)), ada_row(0), ada_row(1), w_gate,
            rows(HG_WIDTH, hg_base), rows(HG_WIDTH, hg_base + 1), rows(HG_WIDTH, hg_base + 2),
            rows(HG_WIDTH, hg_base + 3), const(lb_logits.shape), const((1, HG_DV)),
            rows(AT_WIDTH, AT_OFF // AT_WIDTH), rows(KV_WIDTH, k_col), rows(KV_WIDTH, k_col + 1),
            halo(k_col), halo(k_col + 1),
            pl.BlockSpec(bias.shape, lambda b, i: (0, 0, 0), pipeline_mode=pl.Buffered(1)),
            pl.BlockSpec(memory_space=pltpu.SMEM), const((1, AT_GROUP * AT_HEAD_DIM)), const((1, KV_WIDTH)),
        ],
        out_specs=[rows(n_gate, 0), rows(HG_WIDTH, 0), rows(AT_WIDTH, 0)],
        out_shape=[jax.ShapeDtypeStruct((bsz, seq, n_gate), BF16),
                   jax.ShapeDtypeStruct((bsz, seq, HG_WIDTH), BF16),
                   jax.ShapeDtypeStruct((bsz, seq, AT_WIDTH), BF16)],
        scratch_shapes=[pltpu.VMEM((tm, d), BF16), pltpu.VMEM((HG_HEADS, HG_DV, HG_DK), F32)],
        compiler_params=_params(2),
        name="mixer",
    )(x, norm_g, ada4, ada4, w_gate_bf16, proj, proj, proj, proj, lb_logits, out_norm_g,
      proj, proj, proj, proj, proj, bias, sinks, qg, kg)


def _t5_causal_bucket(n):
    nf = jnp.maximum(n, 1).astype(F32)
    large = MAX_EXACT + (jnp.log(nf / MAX_EXACT) / math.log(MAX_DISTANCE / MAX_EXACT)
                         * (N_BUCKETS - MAX_EXACT)).astype(jnp.int32)
    large = jnp.minimum(large, N_BUCKETS - 1)
    return jnp.where(n < MAX_EXACT, n, large)


def _bias_kernel(table_ref, bucket_t_ref, o_ref):
    bucket_t = bucket_t_ref[...]
    j = lax.broadcasted_iota(jnp.int32, bucket_t.shape, 0)
    i = lax.broadcasted_iota(jnp.int32, bucket_t.shape, 1)
    dist = i - j + BLOCK
    band = (dist >= 0) & (dist < WINDOW)
    for h in range(AT_HEADS):
        acc = jnp.zeros(bucket_t.shape, F32)
        for r in range(N_BUCKETS):
            acc = jnp.where(bucket_t == r, table_ref[r, h], acc)
        g = h % AT_GROUP
        o_ref[h // AT_GROUP, :, g * BLOCK:(g + 1) * BLOCK] = jnp.where(band, acc, NEG_INF)


def _attn_bias(rel_bias_table):
    i = jnp.arange(BLOCK, dtype=jnp.int32)[None, :]
    j = jnp.arange(2 * BLOCK, dtype=jnp.int32)[:, None]
    bucket_t = _t5_causal_bucket(jnp.maximum(i - j + BLOCK, 0))
    shape = (AT_KV_HEADS, 2 * BLOCK, AT_GROUP * BLOCK)
    return pl.pallas_call(
        _bias_kernel,
        in_specs=[
            pl.BlockSpec(memory_space=pltpu.SMEM),
            pl.BlockSpec((2 * BLOCK, BLOCK), lambda: (0, 0)),
        ],
        out_specs=pl.BlockSpec(shape, lambda: (0, 0, 0)),
        out_shape=jax.ShapeDtypeStruct(shape, F32),
        name="attn_bias",
    )(rel_bias_table, bucket_t)


def _head_mean_square(t, pool):
    return jnp.dot((t * t).astype(BF16), pool, preferred_element_type=F32)


def _attention_steps(q_ref, k_ref, v_ref, kh_ref, vh_ref, bias_ref, sink_ref, qg_ref, kg_ref, o_ref, first_block):
    n_blocks = q_ref.shape[0] // BLOCK
    d = AT_HEAD_DIM
    grp = AT_GROUP * d
    sub = lax.broadcasted_iota(jnp.int32, (grp, grp), 0)
    lane = lax.broadcasted_iota(jnp.int32, (grp, grp), 1)
    pool = jnp.where(sub // d == lane // d, 1.0 / d, 0.0).astype(BF16)

    k_all = jnp.concatenate([kh_ref[...], k_ref[...]], axis=0).astype(F32)
    k_n = (k_all * lax.rsqrt(_head_mean_square(k_all, pool) + EPS) * kg_ref[...]).astype(BF16)
    v_t = jnp.concatenate([vh_ref[...], v_ref[...]], axis=0).astype(F32).T.astype(BF16)

    lane_head = lax.broadcasted_iota(jnp.int32, (BLOCK, grp), 1) // d
    head_masks = [(lane_head == g).astype(BF16) for g in range(AT_GROUP)]
    pre_seq = jnp.where(first_block, NEG_INF, 0.0)

    scale = d ** -0.5
    k_reps, q_ns = [], []
    for kv in range(AT_KV_HEADS):
        spread = ((sub // d == kv) & (sub % d == lane % d)).astype(BF16)
        k_reps.append(jnp.dot(k_n, spread, preferred_element_type=F32).astype(BF16))
        q_g = q_ref[:, kv * grp:(kv + 1) * grp].astype(F32)
        q_ns.append((q_g * lax.rsqrt(_head_mean_square(q_g, pool) + EPS) * (qg_ref[...] * scale)).astype(BF16))
    yield

    def logits_t(kv, nb):
        rows = slice(nb * BLOCK, (nb + 1) * BLOCK)
        keys = slice(nb * BLOCK, (nb + 2) * BLOCK)
        q4 = jnp.concatenate([q_ns[kv][rows] * head_masks[g] for g in range(AT_GROUP)], axis=0)
        lg = lax.dot_general(k_reps[kv][keys], q4, (((1,), (1,)), ((), ())),
                             preferred_element_type=F32) + bias_ref[kv]
        if nb == 0:
            lg = jnp.concatenate([lg[:BLOCK] + pre_seq, lg[BLOCK:]], axis=0)
        return lg

    units = [(kv, nb) for kv in range(AT_KV_HEADS) for nb in range(n_blocks)]
    lg_next = logits_t(*units[0])
    yield
    for idx, (kv, nb) in enumerate(units):
        lg = lg_next
        if idx + 1 < len(units):
            lg_next = logits_t(*units[idx + 1])
            yield
        rows = slice(nb * BLOCK, (nb + 1) * BLOCK)
        keys = slice(nb * BLOCK, (nb + 2) * BLOCK)
        sink = jnp.concatenate(
            [jnp.full((1, BLOCK), sink_ref[kv * AT_GROUP + g], F32) for g in range(AT_GROUP)], axis=1)
        m = jnp.maximum(jnp.max(lg, axis=0, keepdims=True), sink)
        e = jnp.exp(lg - m)
        den = jnp.sum(e, axis=0, keepdims=True) + jnp.exp(sink - m)
        o_t = jnp.dot(v_t[kv * d:(kv + 1) * d, keys], e.astype(BF16),
                      preferred_element_type=F32) * (1.0 / den)
        o = jnp.concatenate([o_t[:, g * BLOCK:(g + 1) * BLOCK] for g in range(AT_GROUP)], axis=0).T
        o_ref[rows, kv * grp:(kv + 1) * grp] = o.astype(o_ref.dtype)
        yield


MERGE_SPLIT = 2


def _merge_kernel(x_ref, ohg_ref, oat_ref, ghg_ref, gat_ref, wbh_ref, wba_ref, wo_ref,
                  gate1_ref, n2g_ref, shift2_ref, scale2_ref, x1_ref, h2_ref):
    part = x_ref.shape[0] // MERGE_SPLIT
    parts = [slice(p * part, (p + 1) * part) for p in range(MERGE_SPLIT)]
    m_hg = [jnp.dot(ohg_ref[rows, :], wbh_ref[...], preferred_element_type=F32) for rows in parts]
    m_at = [jnp.dot(oat_ref[rows, :], wba_ref[...], preferred_element_type=F32) for rows in parts]
    ys = []
    for p, rows in enumerate(parts):
        merged = (jax.nn.sigmoid(ghg_ref[rows, :].astype(F32)) * m_hg[p]
                  + jax.nn.sigmoid(gat_ref[rows, :].astype(F32)) * m_at[p])
        ys.append(jnp.dot(merged.astype(BF16), wo_ref[...], preferred_element_type=F32))
    for p, rows in enumerate(parts):
        x1 = x_ref[rows, :] + gate1_ref[...] * ys[p]
        x1_ref[rows, :] = x1
        h2_ref[rows, :] = _rms_modulate(x1, n2g_ref[...], scale2_ref[...], shift2_ref[...]).astype(h2_ref.dtype)


def _merge(x, o_hg, o_at, gates, w_bh, w_ba, w_o, ada4, norm2_g):
    bsz, seq, d = x.shape
    tm = 512
    resident = lambda shape: pl.BlockSpec(shape, lambda b, i: (0, 0), pipeline_mode=pl.Buffered(1))
    ada_row = lambda k: pl.BlockSpec((None, None, 1, d), lambda b, i: (b, k, 0, 0))
    rows = lambda w, colblk: pl.BlockSpec((None, tm, w), lambda b, i: (b, i, colblk))
    return pl.pallas_call(
        _merge_kernel,
        grid=(bsz, seq // tm),
        in_specs=[
            rows(d, 0), rows(HG_WIDTH, 0), rows(AT_WIDTH, 0), rows(d, 0), rows(d, 1),
            resident(w_bh.shape), resident(w_ba.shape), resident(w_o.shape),
            ada_row(0), pl.BlockSpec((1, d), lambda b, i: (0, 0)), ada_row(1), ada_row(2),
        ],
        out_specs=[rows(d, 0), rows(d, 0)],
        out_shape=[jax.ShapeDtypeStruct((bsz, seq, d), F32), jax.ShapeDtypeStruct((bsz, seq, d), BF16)],
        compiler_params=_params(2),
        name="merge",
    )(x, o_hg, o_at, gates, gates, w_bh, w_ba, w_o, ada4, norm2_g, ada4, ada4)


def _ffn_kernel(h2_ref, w1_ref, w2_ref, x1_ref, gate2_ref, o_ref):
    @pl.when(pl.program_id(2) == 0)
    def _():
        o_ref[...] = x1_ref[...]

    hid = jnp.maximum(jnp.dot(h2_ref[...], w1_ref[...], preferred_element_type=F32), 0.0)
    o_ref[...] += gate2_ref[...] * jnp.dot((hid * hid).astype(BF16), w2_ref[...], preferred_element_type=F32)


def _ffn(h2, x1, w1, w2, ada4):
    bsz, seq, d = x1.shape
    dff = w1.shape[-1]
    tm, tf = 512, 2048
    rows = lambda: pl.BlockSpec((None, tm, d), lambda b, i, j: (b, i, 0))
    return pl.pallas_call(
        _ffn_kernel,
        grid=(bsz, seq // tm, dff // tf),
        in_specs=[
            rows(),
            pl.BlockSpec((d, tf), lambda b, i, j: (0, j)),
            pl.BlockSpec((tf, d), lambda b, i, j: (j, 0)),
            rows(),
            pl.BlockSpec((None, None, 1, d), lambda b, i, j: (b, 3, 0, 0)),
        ],
        out_specs=rows(),
        out_shape=jax.ShapeDtypeStruct((bsz, seq, d), F32),
        compiler_params=_params(3),
        name="ffn",
    )(h2, w1, w2, x1, ada4)


def kernel(x, c, w_ada, b_ada, norm1_g, norm2_g, w_in, hg_lb_logits, hg_out_norm_g, q_norm_g, k_norm_g,
           attn_sinks, rel_bias_table, w_branch_hg, w_branch_attn, w_out, w_ff1, w_ff2):
    assert w_ada.shape[0] == 1, "single-layer block"
    bsz = x.shape[0]
    d = D_MODEL
    ada_in = _ada(c, w_ada, b_ada, 2 * d).reshape(bsz, 2, 1, d)
    proj, ada_rest, (w_gate, w_bh, w_ba, w_o, w1, w2) = _inproj(
        x, norm1_g, ada_in, w_in[0, :, :GATE_OFF].astype(BF16),
        ((w_in, (GATE_OFF, IN_WIDTH - GATE_OFF), GATE_CHUNK), (w_branch_hg, None, None),
         (w_branch_attn, None, None), (w_out, None, None), (w_ff1, None, None), (w_ff2, None, None)),
        c, w_ada, b_ada, 2 * d)
    ada_out = ada_rest.reshape(bsz, 4, 1, d)
    bias = _attn_bias(rel_bias_table)
    gates, o_hg, o_at = _mixer(x, norm1_g, ada_in, w_gate, proj, hg_lb_logits, hg_out_norm_g,
                               bias, attn_sinks[0], q_norm_g, k_norm_g)
    x1, h2 = _merge(x, o_hg, o_at, gates, w_bh, w_ba, w_o, ada_out, norm2_g)
    return _ffn(h2, x1, w1, w2, ada_out)
```

```python
import functools
import math

import jax
import jax.numpy as jnp
from jax import lax
from jax.experimental import pallas as pl
from jax.experimental.pallas import tpu as pltpu

F32 = jnp.float32
BF16 = jnp.bfloat16

D_MODEL = 2048
HG_HEADS = 8
HG_DK = 128
HG_DV = 128
HG_WIDTH = HG_HEADS * HG_DK
HG_CHUNK = 64
AT_HEADS = 16
AT_KV_HEADS = 4
AT_HEAD_DIM = 64
AT_GROUP = AT_HEADS // AT_KV_HEADS
AT_WIDTH = AT_HEADS * AT_HEAD_DIM
KV_WIDTH = AT_KV_HEADS * AT_HEAD_DIM
WINDOW = 128
BLOCK = 128
N_BUCKETS = 32
MAX_EXACT = N_BUCKETS // 2
MAX_DISTANCE = 128
EPS = 1e-6
NEG_INF = -1e30
LOG2_E = math.log2(math.e)

HG_OFF = 0
AT_OFF = 4 * HG_WIDTH
GATE_OFF = AT_OFF + AT_WIDTH + 2 * KV_WIDTH
IN_WIDTH = GATE_OFF + 2 * D_MODEL
MXU_COLS = 256

V7X_VMEM_LIMIT_BYTES = 60 * 1024 * 1024
BF16_SUBLANES = 16


def _params(n_axes):
    return pltpu.CompilerParams(
        dimension_semantics=("arbitrary",) * n_axes,
        vmem_limit_bytes=V7X_VMEM_LIMIT_BYTES,
    )


def _rms_modulate(x, g, scale, shift):
    r = lax.rsqrt(jnp.mean(x * x, axis=-1, keepdims=True) + EPS)
    return (x * r * g) * (1.0 + scale) + shift


def _sigmoid(x):
    return 0.5 * jnp.tanh(0.5 * x) + 0.5


def _silu(x):
    h = 0.5 * x
    return h + h * jnp.tanh(h)


def _split_bf16(x):
    hi = x.astype(BF16)
    return hi, (x - hi.astype(F32)).astype(BF16)


def _ada_columns(c_ref, w_ref, b_ref):
    c = c_ref[...]
    c_hi, c_lo = _split_bf16(c * jax.nn.sigmoid(c))
    w_hi, w_lo = _split_bf16(w_ref[...])
    dot = functools.partial(jnp.dot, preferred_element_type=F32)
    return dot(c_hi, w_hi) + dot(c_lo, w_hi) + dot(c_hi, w_lo) + b_ref[...]


def _ada_kernel(c_ref, w_ref, b_ref, win_ref, o_ref, win_bf16_ref):
    o_ref[...] = _ada_columns(c_ref, w_ref, b_ref)
    win_bf16_ref[...] = win_ref[...].astype(win_bf16_ref.dtype)


def _ada(c, w_ada, b_ada, n, w_in, n_in):
    bsz, d = c.shape
    tn = 1024
    steps = n // tn
    rows = w_in.shape[1] // steps
    return pl.pallas_call(
        _ada_kernel,
        grid=(steps,),
        in_specs=[
            pl.BlockSpec((bsz, d), lambda j: (0, 0)),
            pl.BlockSpec((None, d, tn), lambda j: (0, 0, j)),
            pl.BlockSpec((1, tn), lambda j: (0, j)),
            pl.BlockSpec((None, rows, n_in), lambda j: (0, j, 0)),
        ],
        out_specs=[pl.BlockSpec((bsz, tn), lambda j: (0, j)), pl.BlockSpec((rows, n_in), lambda j: (j, 0))],
        out_shape=[jax.ShapeDtypeStruct((bsz, n), F32), jax.ShapeDtypeStruct((w_in.shape[1], n_in), BF16)],
        compiler_params=_params(1),
        name="ada",
    )(c, w_ada, b_ada, w_in)


INPROJ_SPLIT = 2


def _inproj_kernel(late_parts, x_ref, g_ref, shift_ref, scale_ref, w_ref, c_ref, wada_ref, bada_ref, *refs):
    n_in = sum(late_parts)
    late_in, (o_ref, ada_ref, *late_out) = refs[:n_in], refs[n_in:]
    part = x_ref.shape[0] // INPROJ_SPLIT
    for p in range(INPROJ_SPLIT):
        rows = slice(p * part, (p + 1) * part)
        h = _rms_modulate(x_ref[rows, :], g_ref[...], scale_ref[...], shift_ref[...])
        o_ref[rows, :] = jnp.dot(h.astype(BF16), w_ref[...], preferred_element_type=F32).astype(o_ref.dtype)
    ada_ref[...] = _ada_columns(c_ref, wada_ref, bada_ref)
    first = 0
    for n_parts, dst in zip(late_parts, late_out):
        parts = [ref[...] for ref in late_in[first:first + n_parts]]
        first += n_parts
        rows = parts[0] if n_parts == 1 else jnp.concatenate(parts, axis=1)
        if len(dst.shape) == 2:
            dst[...] = rows.astype(dst.dtype)
        else:
            ct = dst.shape[2]
            for t in range(dst.shape[0]):
                dst[t] = rows[:, t * ct:(t + 1) * ct].astype(dst.dtype)


def _inproj(x, norm_g, ada4, w_bf16, late_weights, c, w_ada, b_ada, ada_done):
    bsz, seq, d = x.shape
    n = w_bf16.shape[-1]
    tm, n_col_tiles = 512, 2
    tn = n // n_col_tiles
    assert tn % MXU_COLS == 0
    tiles_per_seq = seq // tm
    n_row_tiles = bsz * tiles_per_seq
    n_steps = n_col_tiles * n_row_tiles
    row = lambda j, i: (i // tiles_per_seq, i % tiles_per_seq)
    step = lambda j, i: j * n_row_tiles + i
    late_in_specs, late_out_specs, late_shapes, late_parts, late_args = [], [], [], [], []
    for w, window, col_tile in late_weights:
        _, rows, cols = w.shape
        first_col, width = window or (0, cols)
        part = math.gcd(first_col, width)
        slab = max(rows // n_steps, BF16_SUBLANES)
        assert rows % slab == 0
        idx = lambda j, i, last=rows // slab - 1: jnp.minimum(step(j, i), last)
        for k in range(width // part):
            late_in_specs.append(pl.BlockSpec(
                (None, slab, part), lambda j, i, idx=idx, blk=first_col // part + k: (0, idx(j, i), blk)))
        late_parts.append(width // part)
        late_args += [w] * (width // part)
        if col_tile is None:
            late_out_specs.append(pl.BlockSpec((slab, width), lambda j, i, idx=idx: (idx(j, i), 0)))
            late_shapes.append(jax.ShapeDtypeStruct((rows, width), BF16))
        else:
            late_out_specs.append(pl.BlockSpec((width // col_tile, slab, col_tile),
                                               lambda j, i, idx=idx: (0, idx(j, i), 0)))
            late_shapes.append(jax.ShapeDtypeStruct((width // col_tile, rows, col_tile), BF16))
    n_ada = w_ada.shape[-1] - ada_done
    ada_cols = n_ada // n_steps
    assert ada_cols % 128 == 0 and ada_done % ada_cols == 0
    ada_blk = lambda j, i: ada_done // ada_cols + step(j, i)
    outs = pl.pallas_call(
        functools.partial(_inproj_kernel, tuple(late_parts)),
        grid=(n_col_tiles, n_row_tiles),
        in_specs=[
            pl.BlockSpec((None, tm, d), lambda j, i: (*row(j, i), 0)),
            pl.BlockSpec((1, d), lambda j, i: (0, 0)),
            pl.BlockSpec((None, None, 1, d), lambda j, i: (i // tiles_per_seq, 0, 0, 0)),
            pl.BlockSpec((None, None, 1, d), lambda j, i: (i // tiles_per_seq, 1, 0, 0)),
            pl.BlockSpec((d, tn), lambda j, i: (0, j)),
            pl.BlockSpec(c.shape, lambda j, i: (0, 0)),
            pl.BlockSpec((None, d, ada_cols), lambda j, i: (0, 0, ada_blk(j, i))),
            pl.BlockSpec((1, ada_cols), lambda j, i: (0, ada_blk(j, i))),
            *late_in_specs,
        ],
        out_specs=[pl.BlockSpec((None, tm, tn), lambda j, i: (*row(j, i), j)),
                   pl.BlockSpec((bsz, ada_cols), lambda j, i: (0, step(j, i))), *late_out_specs],
        out_shape=[jax.ShapeDtypeStruct((bsz, seq, n), BF16), jax.ShapeDtypeStruct((bsz, n_ada), F32),
                   *late_shapes],
        compiler_params=_params(2),
        name="inproj",
    )(x, norm_g, ada4, ada4, w_bf16, c, w_ada, b_ada, *late_args)
    return outs[0], outs[1], outs[2:]


HG_GROUP = 4
HG_ROWS = HG_GROUP * HG_CHUNK
GATE_CHUNK = 256


def _mixer_kernel(x_ref, g_ref, shift_ref, scale_ref, *refs):
    (wg_ref, hgrn_ref, lbl_ref, gn_ref, aq_ref, kv_ref, kv_halo_ref, *attn_params,
     gates_ref, o_ref, oat_ref, h_ref, state_ref) = refs
    q_ref, f_ref, i_ref, hg_ref = (hgrn_ref.at[:, k * HG_WIDTH:(k + 1) * HG_WIDTH] for k in range(4))
    attn_in = (aq_ref, kv_ref.at[:, :KV_WIDTH], kv_ref.at[:, KV_WIDTH:],
               kv_halo_ref.at[:, :KV_WIDTH], kv_halo_ref.at[:, KV_WIDTH:], *attn_params)
    c, n = HG_CHUNK, HG_ROWS
    heads = range(HG_HEADS)
    lanes = [slice(u * HG_DK, (u + 1) * HG_DK) for u in heads]

    @pl.when(pl.program_id(1) == 0)
    def _():
        state_ref[...] = jnp.zeros_like(state_ref)

    h_ref[...] = _rms_modulate(x_ref[...], g_ref[...], scale_ref[...], shift_ref[...]).astype(BF16)
    pending = list(range(wg_ref.shape[0]))

    def gate_chunks(count):
        for _ in range(min(count, len(pending))):
            k = pending.pop(0)
            gates_ref[:, k * GATE_CHUNK:(k + 1) * GATE_CHUNK] = jnp.dot(
                h_ref[...], wg_ref[k], preferred_element_type=F32).astype(gates_ref.dtype)

    attention = _attention_steps(*attn_in, oat_ref, pl.program_id(1) == 0)
    attend = lambda: next(attention, None)

    def interleave(count):
        for _ in range(count):
            attend()
            gate_chunks(1)

    lbl = lbl_ref[...]
    e = jnp.exp(lbl - jnp.max(lbl, axis=0, keepdims=True))
    lb = e[0:1] / jnp.sum(e, axis=0, keepdims=True)
    gn = gn_ref[...]

    t = lax.broadcasted_iota(jnp.int32, (n, n), 0)
    s = lax.broadcasted_iota(jnp.int32, (n, n), 1)
    tri = ((t // c) == (s // c)) & (s <= t)
    tri_t = ((t // c) == (s // c)) & (t <= s)
    r8 = lax.broadcasted_iota(jnp.int32, (8, n), 0)
    s8 = lax.broadcasted_iota(jnp.int32, (8, n), 1)
    in_chunk = (s8 // c) == (r8 % HG_GROUP)
    stat_rows = in_chunk & ((r8 >= HG_GROUP) | ((s8 % c) < c // 2))
    cum_ops = jnp.concatenate([tri.astype(F32), stat_rows.astype(F32)], axis=0).astype(BF16)
    per_chunk = lambda rows: jnp.concatenate(
        [jnp.broadcast_to(rows[cc:cc + 1], (c, HG_DK)) for cc in range(HG_GROUP)], axis=0)
    zeros_rows = lambda rows: jnp.zeros((rows, HG_DK), BF16)
    col_chunk = lax.broadcasted_iota(jnp.int32, (HG_DK, n), 1) // c
    col_masks = [(col_chunk == cc).astype(BF16) for cc in range(HG_GROUP)]

    def chunk_blocks(arr):
        return jnp.concatenate(
            [jnp.concatenate([p for p in (zeros_rows(cc * c), arr[cc * c:(cc + 1) * c],
                                          zeros_rows(n - (cc + 1) * c)) if p.shape[0]], axis=0)
             for cc in range(HG_GROUP)], axis=1)

    attend()
    gate_chunks(3)
    ks, qss, hilo = [], [], []
    for u in heads:
        hf = f_ref[:, lanes[u]].astype(F32)
        hq = q_ref[:, lanes[u]].astype(F32)
        f = lb[:, lanes[u]] + (1.0 - lb[:, lanes[u]]) * _sigmoid(hf)
        log_f = jnp.log2(f)
        ks.append(1.0 - f)
        qss.append(_silu(hq))
        hi = log_f.astype(BF16)
        lo = (log_f - hi.astype(F32)).astype(BF16)
        hilo.append(jnp.concatenate([hi, lo], axis=1))
    rs = [jnp.dot(cum_ops, hilo[u], preferred_element_type=F32) for u in heads]
    interleave(2)
    a_s, bms, kks, qes, decs = [], [], [], [], []
    for u in heads:
        r = rs[u][:, :HG_DK] + rs[u][:, HG_DK:]
        b, b_last = r[0:n], r[n + HG_GROUP:n + 2 * HG_GROUP]
        b_rel = b - per_chunk(r[n:n + HG_GROUP])
        a_s.append((qss[u] * jnp.exp2(b_rel)).astype(BF16))
        bms.append((ks[u] * jnp.exp2(-b_rel)).astype(BF16))
        kks.append((ks[u] * jnp.exp2(per_chunk(b_last) - b)).astype(BF16))
        qes.append((qss[u] * jnp.exp2(b)).T.astype(BF16))
        decs.append(jnp.exp2(b_last))
    scores_t = [lax.dot_general(bms[u], a_s[u], (((1,), (1,)), ((), ())), preferred_element_type=F32)
                for u in heads]
    interleave(2)
    v_ts = [i_ref[:, lanes[u]].astype(F32).T.astype(BF16) for u in heads]
    os_t = [jnp.dot(v_ts[u], jnp.where(tri_t, scores_t[u], 0.0).astype(BF16), preferred_element_type=F32)
            for u in heads]
    interleave(2)
    upd_ts = [jnp.dot(v_ts[u], chunk_blocks(kks[u]), preferred_element_type=F32) for u in heads]
    interleave(2)
    for u in heads:
        s_t = state_ref[u]
        states = []
        for cc in range(HG_GROUP):
            states.append(s_t.astype(BF16))
            s_t = s_t * decs[u][cc:cc + 1] + upd_ts[u][:, cc * HG_DK:(cc + 1) * HG_DK]
        state_ref[u] = s_t
        qe_rows = jnp.concatenate([qes[u] * col_masks[cc] for cc in range(HG_GROUP)], axis=0)
        os_t[u] = os_t[u] + jnp.dot(jnp.concatenate(states, axis=1), qe_rows, preferred_element_type=F32)
    interleave(len(pending) - 2)
    for _ in attention:
        pass
    gate_chunks(len(pending))
    for u in heads:
        o = os_t[u].T
        hg = hg_ref[:, lanes[u]].astype(F32)
        y = o * lax.rsqrt(jnp.mean(o * o, axis=-1, keepdims=True) + EPS) * gn
        o_ref[:, lanes[u]] = (y * _silu(hg)).astype(o_ref.dtype)


def _mixer(x, norm_g, ada4, w_gate_bf16, proj, lb_logits, out_norm_g, bias, sinks, q_norm_g, k_norm_g):
    bsz, seq, d = x.shape
    n_gate = w_gate_bf16.shape[0] * w_gate_bf16.shape[2]
    tm = HG_ROWS
    blocks_per_step = tm // BLOCK
    rows = lambda w, colblk: pl.BlockSpec((None, tm, w), lambda b, i: (b, i, colblk))
    ada_row = lambda k: pl.BlockSpec((None, None, 1, d), lambda b, i: (b, k, 0, 0))
    const = lambda shape: pl.BlockSpec(shape, lambda b, i: (0,) * len(shape))
    w_gate = pl.BlockSpec(w_gate_bf16.shape, lambda b, i: (0, 0, 0), pipeline_mode=pl.Buffered(1))
    kv_col = (AT_OFF + AT_WIDTH) // (2 * KV_WIDTH)
    qg = jnp.tile(q_norm_g, (1, AT_GROUP))
    kg = jnp.tile(k_norm_g, (1, AT_KV_HEADS))
    return pl.pallas_call(
        _mixer_kernel,
        grid=(bsz, seq // tm),
        in_specs=[
            rows(d, 0), const((1, d)), ada_row(0), ada_row(1), w_gate,
            rows(4 * HG_WIDTH, HG_OFF // (4 * HG_WIDTH)), const(lb_logits.shape), const((1, HG_DV)),
            rows(AT_WIDTH, AT_OFF // AT_WIDTH), rows(2 * KV_WIDTH, kv_col),
            pl.BlockSpec((None, BLOCK, 2 * KV_WIDTH),
                         lambda b, i: (b, jnp.maximum(i * blocks_per_step - 1, 0), kv_col)),
            pl.BlockSpec(bias.shape, lambda b, i: (0, 0, 0), pipeline_mode=pl.Buffered(1)),
            pl.BlockSpec(memory_space=pltpu.SMEM), const((1, AT_GROUP * AT_HEAD_DIM)), const((1, KV_WIDTH)),
        ],
        out_specs=[rows(n_gate, 0), rows(HG_WIDTH, 0), rows(AT_WIDTH, 0)],
        out_shape=[jax.ShapeDtypeStruct((bsz, seq, n_gate), BF16),
                   jax.ShapeDtypeStruct((bsz, seq, HG_WIDTH), BF16),
                   jax.ShapeDtypeStruct((bsz, seq, AT_WIDTH), BF16)],
        scratch_shapes=[pltpu.VMEM((tm, d), BF16), pltpu.VMEM((HG_HEADS, HG_DV, HG_DK), F32)],
        compiler_params=_params(2),
        name="mixer",
    )(x, norm_g, ada4, ada4, w_gate_bf16, proj, lb_logits, out_norm_g, proj, proj, proj, bias, sinks, qg, kg)


def _t5_causal_bucket(n):
    nf = jnp.maximum(n, 1).astype(F32)
    large = MAX_EXACT + (jnp.log(nf / MAX_EXACT) / math.log(MAX_DISTANCE / MAX_EXACT)
                         * (N_BUCKETS - MAX_EXACT)).astype(jnp.int32)
    large = jnp.minimum(large, N_BUCKETS - 1)
    return jnp.where(n < MAX_EXACT, n, large)


def _bias_kernel(table_ref, bucket_t_ref, o_ref):
    bucket_t = bucket_t_ref[...]
    j = lax.broadcasted_iota(jnp.int32, bucket_t.shape, 0)
    i = lax.broadcasted_iota(jnp.int32, bucket_t.shape, 1)
    dist = i - j + BLOCK
    band = (dist >= 0) & (dist < WINDOW)
    for h in range(AT_HEADS):
        acc = jnp.zeros(bucket_t.shape, F32)
        for r in range(N_BUCKETS):
            acc = jnp.where(bucket_t == r, table_ref[r, h] * LOG2_E, acc)
        g = h % AT_GROUP
        o_ref[h // AT_GROUP, :, g * BLOCK:(g + 1) * BLOCK] = jnp.where(band, acc, NEG_INF)


def _attn_bias(rel_bias_table):
    i = jnp.arange(BLOCK, dtype=jnp.int32)[None, :]
    j = jnp.arange(2 * BLOCK, dtype=jnp.int32)[:, None]
    bucket_t = _t5_causal_bucket(jnp.maximum(i - j + BLOCK, 0))
    shape = (AT_KV_HEADS, 2 * BLOCK, AT_GROUP * BLOCK)
    return pl.pallas_call(
        _bias_kernel,
        in_specs=[
            pl.BlockSpec(memory_space=pltpu.SMEM),
            pl.BlockSpec((2 * BLOCK, BLOCK), lambda: (0, 0)),
        ],
        out_specs=pl.BlockSpec(shape, lambda: (0, 0, 0)),
        out_shape=jax.ShapeDtypeStruct(shape, F32),
        name="attn_bias",
    )(rel_bias_table, bucket_t)


def _head_mean_square(t, pool):
    return jnp.dot((t * t).astype(BF16), pool, preferred_element_type=F32)


def _attention_steps(q_ref, k_ref, v_ref, kh_ref, vh_ref, bias_ref, sink_ref, qg_ref, kg_ref, o_ref, first_block):
    n_blocks = q_ref.shape[0] // BLOCK
    d = AT_HEAD_DIM
    grp = AT_GROUP * d
    sub = lax.broadcasted_iota(jnp.int32, (grp, grp), 0)
    lane = lax.broadcasted_iota(jnp.int32, (grp, grp), 1)
    pool = jnp.where(sub // d == lane // d, 1.0 / d, 0.0).astype(BF16)

    k_all = jnp.concatenate([kh_ref[...], k_ref[...]], axis=0).astype(F32)
    k_n = (k_all * lax.rsqrt(_head_mean_square(k_all, pool) + EPS) * kg_ref[...]).astype(BF16)
    v_t = jnp.concatenate([vh_ref[...], v_ref[...]], axis=0).astype(F32).T.astype(BF16)

    lane_head = lax.broadcasted_iota(jnp.int32, (BLOCK, grp), 1) // d
    head_masks = [(lane_head == g).astype(BF16) for g in range(AT_GROUP)]
    pre_seq = jnp.where(first_block, NEG_INF, 0.0)

    scale = d ** -0.5 * LOG2_E
    k_reps, q_ns = [], []
    for kv in range(AT_KV_HEADS):
        spread = ((sub // d == kv) & (sub % d == lane % d)).astype(BF16)
        k_reps.append(jnp.dot(k_n, spread, preferred_element_type=F32).astype(BF16))
        q_g = q_ref[:, kv * grp:(kv + 1) * grp].astype(F32)
        q_ns.append((q_g * lax.rsqrt(_head_mean_square(q_g, pool) + EPS) * (qg_ref[...] * scale)).astype(BF16))
    yield

    def logits_t(kv, nb):
        rows = slice(nb * BLOCK, (nb + 1) * BLOCK)
        keys = slice(nb * BLOCK, (nb + 2) * BLOCK)
        q4 = jnp.concatenate([q_ns[kv][rows] * head_masks[g] for g in range(AT_GROUP)], axis=0)
        lg = lax.dot_general(k_reps[kv][keys], q4, (((1,), (1,)), ((), ())),
                             preferred_element_type=F32) + bias_ref[kv]
        if nb == 0:
            lg = jnp.concatenate([lg[:BLOCK] + pre_seq, lg[BLOCK:]], axis=0)
        return lg

    units = [(kv, nb) for kv in range(AT_KV_HEADS) for nb in range(n_blocks)]
    lg_next = logits_t(*units[0])
    yield
    for idx, (kv, nb) in enumerate(units):
        lg = lg_next
        if idx + 1 < len(units):
            lg_next = logits_t(*units[idx + 1])
            yield
        rows = slice(nb * BLOCK, (nb + 1) * BLOCK)
        keys = slice(nb * BLOCK, (nb + 2) * BLOCK)
        sink = jnp.concatenate(
            [jnp.full((1, BLOCK), sink_ref[kv * AT_GROUP + g] * LOG2_E, F32) for g in range(AT_GROUP)], axis=1)
        m = jnp.maximum(jnp.max(lg, axis=0, keepdims=True), sink)
        e = jnp.exp2(lg - m)
        den = jnp.sum(e, axis=0, keepdims=True) + jnp.exp2(sink - m)
        o_t = jnp.dot(v_t[kv * d:(kv + 1) * d, keys], e.astype(BF16),
                      preferred_element_type=F32) * (1.0 / den)
        o = jnp.concatenate([o_t[:, g * BLOCK:(g + 1) * BLOCK] for g in range(AT_GROUP)], axis=0).T
        o_ref[rows, kv * grp:(kv + 1) * grp] = o.astype(o_ref.dtype)
        yield


MERGE_SPLIT = 2


def _merge_kernel(x_ref, ohg_ref, oat_ref, gates_ref, wbh_ref, wba_ref, wo_ref,
                  gate1_ref, n2g_ref, shift2_ref, scale2_ref, x1_ref, h2_ref):
    d = x_ref.shape[1]
    ghg_ref, gat_ref = gates_ref.at[:, :d], gates_ref.at[:, d:]
    part = x_ref.shape[0] // MERGE_SPLIT
    parts = [slice(p * part, (p + 1) * part) for p in range(MERGE_SPLIT)]
    m_hg = [jnp.dot(ohg_ref[rows, :], wbh_ref[...], preferred_element_type=F32) for rows in parts]
    m_at = [jnp.dot(oat_ref[rows, :], wba_ref[...], preferred_element_type=F32) for rows in parts]
    ys = []
    for p, rows in enumerate(parts):
        merged = (jax.nn.sigmoid(ghg_ref[rows, :].astype(F32)) * m_hg[p]
                  + jax.nn.sigmoid(gat_ref[rows, :].astype(F32)) * m_at[p])
        ys.append(jnp.dot(merged.astype(BF16), wo_ref[...], preferred_element_type=F32))
    for p, rows in enumerate(parts):
        x1 = x_ref[rows, :] + gate1_ref[...] * ys[p]
        x1_ref[rows, :] = x1
        h2_ref[rows, :] = _rms_modulate(x1, n2g_ref[...], scale2_ref[...], shift2_ref[...]).astype(h2_ref.dtype)


def _merge(x, o_hg, o_at, gates, w_bh, w_ba, w_o, ada4, norm2_g):
    bsz, seq, d = x.shape
    tm = 512
    resident = lambda shape: pl.BlockSpec(shape, lambda b, i: (0, 0), pipeline_mode=pl.Buffered(1))
    ada_row = lambda k: pl.BlockSpec((None, None, 1, d), lambda b, i: (b, k, 0, 0))
    rows = lambda w, colblk: pl.BlockSpec((None, tm, w), lambda b, i: (b, i, colblk))
    return pl.pallas_call(
        _merge_kernel,
        grid=(bsz, seq // tm),
        in_specs=[
            rows(d, 0), rows(HG_WIDTH, 0), rows(AT_WIDTH, 0), rows(2 * d, 0),
            resident(w_bh.shape), resident(w_ba.shape), resident(w_o.shape),
            ada_row(0), pl.BlockSpec((1, d), lambda b, i: (0, 0)), ada_row(1), ada_row(2),
        ],
        out_specs=[rows(d, 0), rows(d, 0)],
        out_shape=[jax.ShapeDtypeStruct((bsz, seq, d), F32), jax.ShapeDtypeStruct((bsz, seq, d), BF16)],
        compiler_params=_params(2),
        name="merge",
    )(x, o_hg, o_at, gates, w_bh, w_ba, w_o, ada4, norm2_g, ada4, ada4)


def _ffn_kernel(h2_ref, w1_ref, w2_ref, x1_ref, gate2_ref, o_ref):
    @pl.when(pl.program_id(2) == 0)
    def _():
        o_ref[...] = x1_ref[...]

    hid = jnp.maximum(jnp.dot(h2_ref[...], w1_ref[...], preferred_element_type=F32), 0.0)
    o_ref[...] += gate2_ref[...] * jnp.dot((hid * hid).astype(BF16), w2_ref[...], preferred_element_type=F32)


def _ffn(h2, x1, w1, w2, ada4):
    bsz, seq, d = x1.shape
    dff = w1.shape[-1]
    tm, tf = 512, 2048
    rows = lambda: pl.BlockSpec((None, tm, d), lambda b, i, j: (b, i, 0))
    return pl.pallas_call(
        _ffn_kernel,
        grid=(bsz, seq // tm, dff // tf),
        in_specs=[
            rows(),
            pl.BlockSpec((d, tf), lambda b, i, j: (0, j)),
            pl.BlockSpec((tf, d), lambda b, i, j: (j, 0)),
            rows(),
            pl.BlockSpec((None, None, 1, d), lambda b, i, j: (b, 3, 0, 0)),
        ],
        out_specs=rows(),
        out_shape=jax.ShapeDtypeStruct((bsz, seq, d), F32),
        compiler_params=_params(3),
        name="ffn",
    )(h2, w1, w2, x1, ada4)


def kernel(x, c, w_ada, b_ada, norm1_g, norm2_g, w_in, hg_lb_logits, hg_out_norm_g, q_norm_g, k_norm_g,
           attn_sinks, rel_bias_table, w_branch_hg, w_branch_attn, w_out, w_ff1, w_ff2):
    assert w_ada.shape[0] == 1, "single-layer block"
    bsz = x.shape[0]
    d = D_MODEL
    ada_in, w_mix = _ada(c, w_ada, b_ada, 2 * d, w_in, GATE_OFF)
    ada_in = ada_in.reshape(bsz, 2, 1, d)
    proj, ada_rest, (w_gate, w_bh, w_ba, w_o, w1, w2) = _inproj(
        x, norm1_g, ada_in, w_mix,
        ((w_in, (GATE_OFF, IN_WIDTH - GATE_OFF), GATE_CHUNK), (w_branch_hg, None, None),
         (w_branch_attn, None, None), (w_out, None, None), (w_ff1, None, None), (w_ff2, None, None)),
        c, w_ada, b_ada, 2 * d)
    ada_out = ada_rest.reshape(bsz, 4, 1, d)
    bias = _attn_bias(rel_bias_table)
    gates, o_hg, o_at = _mixer(x, norm1_g, ada_in, w_gate, proj, hg_lb_logits, hg_out_norm_g,
                               bias, attn_sinks[0], q_norm_g, k_norm_g)
    x1, h2 = _merge(x, o_hg, o_at, gates, w_bh, w_ba, w_o, ada_out, norm2_g)
    return _ffn(h2, x1, w1, w2, ada_out)
```

```python
import functools
import math

import jax
import jax.numpy as jnp
from jax import lax
from jax.experimental import pallas as pl
from jax.experimental.pallas import tpu as pltpu

F32 = jnp.float32
BF16 = jnp.bfloat16

D_MODEL = 2048
HG_HEADS = 8
HG_DK = 128
HG_DV = 128
HG_WIDTH = HG_HEADS * HG_DK
HG_CHUNK = 64
AT_HEADS = 16
AT_KV_HEADS = 4
AT_HEAD_DIM = 64
AT_GROUP = AT_HEADS // AT_KV_HEADS
AT_WIDTH = AT_HEADS * AT_HEAD_DIM
KV_WIDTH = AT_KV_HEADS * AT_HEAD_DIM
WINDOW = 128
BLOCK = 128
N_BUCKETS = 32
MAX_EXACT = N_BUCKETS // 2
MAX_DISTANCE = 128
EPS = 1e-6
NEG_INF = -1e30
LOG2_E = math.log2(math.e)

HG_OFF = 0
AT_OFF = 4 * HG_WIDTH
GATE_OFF = AT_OFF + AT_WIDTH + 2 * KV_WIDTH
IN_WIDTH = GATE_OFF + 2 * D_MODEL
MXU_COLS = 256

V7X_VMEM_LIMIT_BYTES = 60 * 1024 * 1024
BF16_SUBLANES = 16


def _params(n_axes):
    return pltpu.CompilerParams(
        dimension_semantics=("arbitrary",) * n_axes,
        vmem_limit_bytes=V7X_VMEM_LIMIT_BYTES,
    )


def _rms_modulate(x, g, scale, shift):
    r = lax.rsqrt(jnp.mean(x * x, axis=-1, keepdims=True) + EPS)
    return (x * r * g) * (1.0 + scale) + shift


def _sigmoid(x):
    return 0.5 * jnp.tanh(0.5 * x) + 0.5


def _silu(x):
    h = 0.5 * x
    return h + h * jnp.tanh(h)


def _split_bf16(x):
    hi = x.astype(BF16)
    return hi, (x - hi.astype(F32)).astype(BF16)


def _ada_columns(c_ref, w_ref, b_ref):
    c = c_ref[...]
    c_hi, c_lo = _split_bf16(c * jax.nn.sigmoid(c))
    w_hi, w_lo = _split_bf16(w_ref[...])
    dot = functools.partial(jnp.dot, preferred_element_type=F32)
    return dot(c_hi, w_hi) + dot(c_lo, w_hi) + dot(c_hi, w_lo) + b_ref[...]


def _ada_kernel(c_ref, w_ref, b_ref, win_ref, table_ref, bucket_t_ref, o_ref, win_bf16_ref, bias_ref):
    o_ref[...] = _ada_columns(c_ref, w_ref, b_ref)
    win_bf16_ref[...] = win_ref[...].astype(win_bf16_ref.dtype)
    _bias_group(table_ref, bucket_t_ref, bias_ref, pl.program_id(0))


def _ada(c, w_ada, b_ada, n, w_in, n_in, rel_bias_table):
    bsz, d = c.shape
    steps = AT_KV_HEADS
    tn = n // steps
    rows = w_in.shape[1] // steps
    i = jnp.arange(BLOCK, dtype=jnp.int32)[None, :]
    j = jnp.arange(2 * BLOCK, dtype=jnp.int32)[:, None]
    bucket_t = _t5_causal_bucket(jnp.maximum(i - j + BLOCK, 0))
    bias_block = (2 * BLOCK, AT_GROUP * BLOCK)
    return pl.pallas_call(
        _ada_kernel,
        grid=(steps,),
        in_specs=[
            pl.BlockSpec((bsz, d), lambda j: (0, 0)),
            pl.BlockSpec((None, d, tn), lambda j: (0, 0, j)),
            pl.BlockSpec((1, tn), lambda j: (0, j)),
            pl.BlockSpec((None, rows, n_in), lambda j: (0, j, 0)),
            pl.BlockSpec(memory_space=pltpu.SMEM),
            pl.BlockSpec(bucket_t.shape, lambda j: (0, 0)),
        ],
        out_specs=[pl.BlockSpec((bsz, tn), lambda j: (0, j)), pl.BlockSpec((rows, n_in), lambda j: (j, 0)),
                   pl.BlockSpec((None, *bias_block), lambda j: (j, 0, 0))],
        out_shape=[jax.ShapeDtypeStruct((bsz, n), F32), jax.ShapeDtypeStruct((w_in.shape[1], n_in), BF16),
                   jax.ShapeDtypeStruct((AT_KV_HEADS, *bias_block), F32)],
        compiler_params=_params(1),
        name="ada",
    )(c, w_ada, b_ada, w_in, rel_bias_table, bucket_t)


INPROJ_SPLIT = 2


def _inproj_kernel(late_parts, x_ref, g_ref, shift_ref, scale_ref, w_ref, c_ref, wada_ref, bada_ref, *refs):
    n_in = sum(late_parts)
    late_in, (o_ref, ada_ref, *late_out) = refs[:n_in], refs[n_in:]
    part = x_ref.shape[0] // INPROJ_SPLIT
    for p in range(INPROJ_SPLIT):
        rows = slice(p * part, (p + 1) * part)
        h = _rms_modulate(x_ref[rows, :], g_ref[...], scale_ref[...], shift_ref[...])
        o_ref[rows, :] = jnp.dot(h.astype(BF16), w_ref[...], preferred_element_type=F32).astype(o_ref.dtype)
    ada_ref[...] = _ada_columns(c_ref, wada_ref, bada_ref)
    first = 0
    for n_parts, dst in zip(late_parts, late_out):
        parts = [ref[...] for ref in late_in[first:first + n_parts]]
        first += n_parts
        rows = parts[0] if n_parts == 1 else jnp.concatenate(parts, axis=1)
        if len(dst.shape) == 2:
            dst[...] = rows.astype(dst.dtype)
        else:
            ct = dst.shape[2]
            for t in range(dst.shape[0]):
                dst[t] = rows[:, t * ct:(t + 1) * ct].astype(dst.dtype)


def _inproj(x, norm_g, ada4, w_bf16, late_weights, c, w_ada, b_ada, ada_done):
    bsz, seq, d = x.shape
    n = w_bf16.shape[-1]
    tm, n_col_tiles = 512, 2
    tn = n // n_col_tiles
    assert tn % MXU_COLS == 0
    tiles_per_seq = seq // tm
    n_row_tiles = bsz * tiles_per_seq
    n_steps = n_col_tiles * n_row_tiles
    row = lambda j, i: (i // tiles_per_seq, i % tiles_per_seq)
    step = lambda j, i: j * n_row_tiles + i
    late_in_specs, late_out_specs, late_shapes, late_parts, late_args = [], [], [], [], []
    for w, window, col_tile in late_weights:
        _, rows, cols = w.shape
        first_col, width = window or (0, cols)
        part = math.gcd(first_col, width)
        slab = max(rows // n_steps, BF16_SUBLANES)
        assert rows % slab == 0
        idx = lambda j, i, last=rows // slab - 1: jnp.minimum(step(j, i), last)
        for k in range(width // part):
            late_in_specs.append(pl.BlockSpec(
                (None, slab, part), lambda j, i, idx=idx, blk=first_col // part + k: (0, idx(j, i), blk)))
        late_parts.append(width // part)
        late_args += [w] * (width // part)
        if col_tile is None:
            late_out_specs.append(pl.BlockSpec((slab, width), lambda j, i, idx=idx: (idx(j, i), 0)))
            late_shapes.append(jax.ShapeDtypeStruct((rows, width), BF16))
        else:
            late_out_specs.append(pl.BlockSpec((width // col_tile, slab, col_tile),
                                               lambda j, i, idx=idx: (0, idx(j, i), 0)))
            late_shapes.append(jax.ShapeDtypeStruct((width // col_tile, rows, col_tile), BF16))
    n_ada = w_ada.shape[-1] - ada_done
    ada_cols = n_ada // n_steps
    assert ada_cols % 128 == 0 and ada_done % ada_cols == 0
    ada_blk = lambda j, i: ada_done // ada_cols + step(j, i)
    outs = pl.pallas_call(
        functools.partial(_inproj_kernel, tuple(late_parts)),
        grid=(n_col_tiles, n_row_tiles),
        in_specs=[
            pl.BlockSpec((None, tm, d), lambda j, i: (*row(j, i), 0)),
            pl.BlockSpec((1, d), lambda j, i: (0, 0)),
            pl.BlockSpec((None, None, 1, d), lambda j, i: (i // tiles_per_seq, 0, 0, 0)),
            pl.BlockSpec((None, None, 1, d), lambda j, i: (i // tiles_per_seq, 1, 0, 0)),
            pl.BlockSpec((d, tn), lambda j, i: (0, j)),
            pl.BlockSpec(c.shape, lambda j, i: (0, 0)),
            pl.BlockSpec((None, d, ada_cols), lambda j, i: (0, 0, ada_blk(j, i))),
            pl.BlockSpec((1, ada_cols), lambda j, i: (0, ada_blk(j, i))),
            *late_in_specs,
        ],
        out_specs=[pl.BlockSpec((None, tm, tn), lambda j, i: (*row(j, i), j)),
                   pl.BlockSpec((bsz, ada_cols), lambda j, i: (0, step(j, i))), *late_out_specs],
        out_shape=[jax.ShapeDtypeStruct((bsz, seq, n), BF16), jax.ShapeDtypeStruct((bsz, n_ada), F32),
                   *late_shapes],
        compiler_params=_params(2),
        name="inproj",
    )(x, norm_g, ada4, ada4, w_bf16, c, w_ada, b_ada, *late_args)
    return outs[0], outs[1], outs[2:]


HG_GROUP = 4
HG_ROWS = HG_GROUP * HG_CHUNK
GATE_CHUNK = 256


def _mixer_kernel(x_ref, g_ref, shift_ref, scale_ref, *refs):
    (wg_ref, hgrn_ref, lbl_ref, gn_ref, aq_ref, kv_ref, kv_halo_ref, *attn_params,
     gates_ref, o_ref, oat_ref, h_ref, state_ref) = refs
    q_ref, f_ref, i_ref, hg_ref = (hgrn_ref.at[:, k * HG_WIDTH:(k + 1) * HG_WIDTH] for k in range(4))
    attn_in = (aq_ref, kv_ref.at[:, :KV_WIDTH], kv_ref.at[:, KV_WIDTH:],
               kv_halo_ref.at[:, :KV_WIDTH], kv_halo_ref.at[:, KV_WIDTH:], *attn_params)
    c, n = HG_CHUNK, HG_ROWS
    heads = range(HG_HEADS)
    lanes = [slice(u * HG_DK, (u + 1) * HG_DK) for u in heads]

    @pl.when(pl.program_id(1) == 0)
    def _():
        state_ref[...] = jnp.zeros_like(state_ref)

    h_ref[...] = _rms_modulate(x_ref[...], g_ref[...], scale_ref[...], shift_ref[...]).astype(BF16)
    pending = list(range(wg_ref.shape[0]))

    def gate_chunks(count):
        for _ in range(min(count, len(pending))):
            k = pending.pop(0)
            gates_ref[:, k * GATE_CHUNK:(k + 1) * GATE_CHUNK] = jnp.dot(
                h_ref[...], wg_ref[k], preferred_element_type=F32).astype(gates_ref.dtype)

    attention = _attention_steps(*attn_in, oat_ref, pl.program_id(1) == 0)
    attend = lambda: next(attention, None)

    def interleave(count):
        for _ in range(count):
            attend()
            gate_chunks(1)

    lbl = lbl_ref[...]
    e = jnp.exp(lbl - jnp.max(lbl, axis=0, keepdims=True))
    lb = e[0:1] / jnp.sum(e, axis=0, keepdims=True)
    gn = gn_ref[...]

    t = lax.broadcasted_iota(jnp.int32, (n, n), 0)
    s = lax.broadcasted_iota(jnp.int32, (n, n), 1)
    tri = ((t // c) == (s // c)) & (s <= t)
    tri_t = ((t // c) == (s // c)) & (t <= s)
    r8 = lax.broadcasted_iota(jnp.int32, (8, n), 0)
    s8 = lax.broadcasted_iota(jnp.int32, (8, n), 1)
    in_chunk = (s8 // c) == (r8 % HG_GROUP)
    stat_rows = in_chunk & ((r8 >= HG_GROUP) | ((s8 % c) < c // 2))
    cum_ops = jnp.concatenate([tri.astype(F32), stat_rows.astype(F32)], axis=0).astype(BF16)
    per_chunk = lambda rows: jnp.concatenate(
        [jnp.broadcast_to(rows[cc:cc + 1], (c, HG_DK)) for cc in range(HG_GROUP)], axis=0)
    zeros_rows = lambda rows: jnp.zeros((rows, HG_DK), BF16)
    col_chunk = lax.broadcasted_iota(jnp.int32, (HG_DK, n), 1) // c
    col_masks = [(col_chunk == cc).astype(BF16) for cc in range(HG_GROUP)]

    def chunk_blocks(arr):
        return jnp.concatenate(
            [jnp.concatenate([p for p in (zeros_rows(cc * c), arr[cc * c:(cc + 1) * c],
                                          zeros_rows(n - (cc + 1) * c)) if p.shape[0]], axis=0)
             for cc in range(HG_GROUP)], axis=1)

    attend()
    gate_chunks(3)
    ks, qss, hilo = [], [], []
    for u in heads:
        hf = f_ref[:, lanes[u]].astype(F32)
        hq = q_ref[:, lanes[u]].astype(F32)
        f = lb[:, lanes[u]] + (1.0 - lb[:, lanes[u]]) * _sigmoid(hf)
        log_f = jnp.log2(f)
        ks.append(1.0 - f)
        qss.append(_silu(hq))
        hi = log_f.astype(BF16)
        lo = (log_f - hi.astype(F32)).astype(BF16)
        hilo.append(jnp.concatenate([hi, lo], axis=1))
    rs = [jnp.dot(cum_ops, hilo[u], preferred_element_type=F32) for u in heads]
    interleave(2)
    a_s, bms, kks, qes, decs = [], [], [], [], []
    for u in heads:
        r = rs[u][:, :HG_DK] + rs[u][:, HG_DK:]
        b, b_last = r[0:n], r[n + HG_GROUP:n + 2 * HG_GROUP]
        b_rel = b - per_chunk(r[n:n + HG_GROUP])
        a_s.append((qss[u] * jnp.exp2(b_rel)).astype(BF16))
        bms.append((ks[u] * jnp.exp2(-b_rel)).astype(BF16))
        kks.append((ks[u] * jnp.exp2(per_chunk(b_last) - b)).astype(BF16))
        qes.append((qss[u] * jnp.exp2(b)).T.astype(BF16))
        decs.append(jnp.exp2(b_last))
    scores_t = [lax.dot_general(bms[u], a_s[u], (((1,), (1,)), ((), ())), preferred_element_type=F32)
                for u in heads]
    interleave(2)
    v_ts = [i_ref[:, lanes[u]].astype(F32).T.astype(BF16) for u in heads]
    os_t = [jnp.dot(v_ts[u], jnp.where(tri_t, scores_t[u], 0.0).astype(BF16), preferred_element_type=F32)
            for u in heads]
    interleave(2)
    upd_ts = [jnp.dot(v_ts[u], chunk_blocks(kks[u]), preferred_element_type=F32) for u in heads]
    interleave(2)
    for u in heads:
        s_t = state_ref[u]
        states = []
        for cc in range(HG_GROUP):
            states.append(s_t.astype(BF16))
            s_t = s_t * decs[u][cc:cc + 1] + upd_ts[u][:, cc * HG_DK:(cc + 1) * HG_DK]
        state_ref[u] = s_t
        qe_rows = jnp.concatenate([qes[u] * col_masks[cc] for cc in range(HG_GROUP)], axis=0)
        os_t[u] = os_t[u] + jnp.dot(jnp.concatenate(states, axis=1), qe_rows, preferred_element_type=F32)
    interleave(len(pending) - 2)
    for _ in attention:
        pass
    gate_chunks(len(pending))
    for u in heads:
        o = os_t[u].T
        hg = hg_ref[:, lanes[u]].astype(F32)
        y = o * lax.rsqrt(jnp.mean(o * o, axis=-1, keepdims=True) + EPS) * gn
        o_ref[:, lanes[u]] = (y * _silu(hg)).astype(o_ref.dtype)


def _mixer(x, norm_g, ada4, w_gate_bf16, proj, lb_logits, out_norm_g, bias, sinks, q_norm_g, k_norm_g):
    bsz, seq, d = x.shape
    n_gate = w_gate_bf16.shape[0] * w_gate_bf16.shape[2]
    tm = HG_ROWS
    blocks_per_step = tm // BLOCK
    rows = lambda w, colblk: pl.BlockSpec((None, tm, w), lambda b, i: (b, i, colblk))
    ada_row = lambda k: pl.BlockSpec((None, None, 1, d), lambda b, i: (b, k, 0, 0))
    const = lambda shape: pl.BlockSpec(shape, lambda b, i: (0,) * len(shape))
    w_gate = pl.BlockSpec(w_gate_bf16.shape, lambda b, i: (0, 0, 0), pipeline_mode=pl.Buffered(1))
    kv_col = (AT_OFF + AT_WIDTH) // (2 * KV_WIDTH)
    qg = jnp.tile(q_norm_g, (1, AT_GROUP))
    kg = jnp.tile(k_norm_g, (1, AT_KV_HEADS))
    return pl.pallas_call(
        _mixer_kernel,
        grid=(bsz, seq // tm),
        in_specs=[
            rows(d, 0), const((1, d)), ada_row(0), ada_row(1), w_gate,
            rows(4 * HG_WIDTH, HG_OFF // (4 * HG_WIDTH)), const(lb_logits.shape), const((1, HG_DV)),
            rows(AT_WIDTH, AT_OFF // AT_WIDTH), rows(2 * KV_WIDTH, kv_col),
            pl.BlockSpec((None, BLOCK, 2 * KV_WIDTH),
                         lambda b, i: (b, jnp.maximum(i * blocks_per_step - 1, 0), kv_col)),
            pl.BlockSpec(bias.shape, lambda b, i: (0, 0, 0), pipeline_mode=pl.Buffered(1)),
            pl.BlockSpec(memory_space=pltpu.SMEM), const((1, AT_GROUP * AT_HEAD_DIM)), const((1, KV_WIDTH)),
        ],
        out_specs=[rows(n_gate, 0), rows(HG_WIDTH, 0), rows(AT_WIDTH, 0)],
        out_shape=[jax.ShapeDtypeStruct((bsz, seq, n_gate), BF16),
                   jax.ShapeDtypeStruct((bsz, seq, HG_WIDTH), BF16),
                   jax.ShapeDtypeStruct((bsz, seq, AT_WIDTH), BF16)],
        scratch_shapes=[pltpu.VMEM((tm, d), BF16), pltpu.VMEM((HG_HEADS, HG_DV, HG_DK), F32)],
        compiler_params=_params(2),
        name="mixer",
    )(x, norm_g, ada4, ada4, w_gate_bf16, proj, lb_logits, out_norm_g, proj, proj, proj, bias, sinks, qg, kg)


def _t5_causal_bucket(n):
    nf = jnp.maximum(n, 1).astype(F32)
    large = MAX_EXACT + (jnp.log(nf / MAX_EXACT) / math.log(MAX_DISTANCE / MAX_EXACT)
                         * (N_BUCKETS - MAX_EXACT)).astype(jnp.int32)
    large = jnp.minimum(large, N_BUCKETS - 1)
    return jnp.where(n < MAX_EXACT, n, large)


def _bias_group(table_ref, bucket_t_ref, o_ref, kv):
    bucket_t = bucket_t_ref[...]
    j = lax.broadcasted_iota(jnp.int32, bucket_t.shape, 0)
    i = lax.broadcasted_iota(jnp.int32, bucket_t.shape, 1)
    dist = i - j + BLOCK
    band = (dist >= 0) & (dist < WINDOW)
    for g in range(AT_GROUP):
        acc = jnp.zeros(bucket_t.shape, F32)
        for r in range(N_BUCKETS):
            acc = jnp.where(bucket_t == r, table_ref[r, kv * AT_GROUP + g] * LOG2_E, acc)
        o_ref[:, g * BLOCK:(g + 1) * BLOCK] = jnp.where(band, acc, NEG_INF)


def _head_mean_square(t, pool):
    return jnp.dot((t * t).astype(BF16), pool, preferred_element_type=F32)


def _attention_steps(q_ref, k_ref, v_ref, kh_ref, vh_ref, bias_ref, sink_ref, qg_ref, kg_ref, o_ref, first_block):
    n_blocks = q_ref.shape[0] // BLOCK
    d = AT_HEAD_DIM
    grp = AT_GROUP * d
    sub = lax.broadcasted_iota(jnp.int32, (grp, grp), 0)
    lane = lax.broadcasted_iota(jnp.int32, (grp, grp), 1)
    pool = jnp.where(sub // d == lane // d, 1.0 / d, 0.0).astype(BF16)

    k_all = jnp.concatenate([kh_ref[...], k_ref[...]], axis=0).astype(F32)
    k_n = (k_all * lax.rsqrt(_head_mean_square(k_all, pool) + EPS) * kg_ref[...]).astype(BF16)
    v_t = jnp.concatenate([vh_ref[...], v_ref[...]], axis=0).astype(F32).T.astype(BF16)

    lane_head = lax.broadcasted_iota(jnp.int32, (BLOCK, grp), 1) // d
    head_masks = [(lane_head == g).astype(BF16) for g in range(AT_GROUP)]
    pre_seq = jnp.where(first_block, NEG_INF, 0.0)

    scale = d ** -0.5 * LOG2_E
    k_reps, q_ns = [], []
    for kv in range(AT_KV_HEADS):
        spread = ((sub // d == kv) & (sub % d == lane % d)).astype(BF16)
        k_reps.append(jnp.dot(k_n, spread, preferred_element_type=F32).astype(BF16))
        q_g = q_ref[:, kv * grp:(kv + 1) * grp].astype(F32)
        q_ns.append((q_g * lax.rsqrt(_head_mean_square(q_g, pool) + EPS) * (qg_ref[...] * scale)).astype(BF16))
    yield

    def logits_t(kv, nb):
        rows = slice(nb * BLOCK, (nb + 1) * BLOCK)
        keys = slice(nb * BLOCK, (nb + 2) * BLOCK)
        q4 = jnp.concatenate([q_ns[kv][rows] * head_masks[g] for g in range(AT_GROUP)], axis=0)
        lg = lax.dot_general(k_reps[kv][keys], q4, (((1,), (1,)), ((), ())),
                             preferred_element_type=F32) + bias_ref[kv]
        if nb == 0:
            lg = jnp.concatenate([lg[:BLOCK] + pre_seq, lg[BLOCK:]], axis=0)
        return lg

    units = [(kv, nb) for kv in range(AT_KV_HEADS) for nb in range(n_blocks)]
    lg_next = logits_t(*units[0])
    yield
    for idx, (kv, nb) in enumerate(units):
        lg = lg_next
        if idx + 1 < len(units):
            lg_next = logits_t(*units[idx + 1])
            yield
        rows = slice(nb * BLOCK, (nb + 1) * BLOCK)
        keys = slice(nb * BLOCK, (nb + 2) * BLOCK)
        sink = jnp.concatenate(
            [jnp.full((1, BLOCK), sink_ref[kv * AT_GROUP + g] * LOG2_E, F32) for g in range(AT_GROUP)], axis=1)
        m = jnp.maximum(jnp.max(lg, axis=0, keepdims=True), sink)
        e = jnp.exp2(lg - m)
        den = jnp.sum(e, axis=0, keepdims=True) + jnp.exp2(sink - m)
        o_t = jnp.dot(v_t[kv * d:(kv + 1) * d, keys], e.astype(BF16),
                      preferred_element_type=F32) * (1.0 / den)
        o = jnp.concatenate([o_t[:, g * BLOCK:(g + 1) * BLOCK] for g in range(AT_GROUP)], axis=0).T
        o_ref[rows, kv * grp:(kv + 1) * grp] = o.astype(o_ref.dtype)
        yield


MERGE_SPLIT = 2


def _merge_kernel(x_ref, ohg_ref, oat_ref, gates_ref, wbh_ref, wba_ref, wo_ref,
                  gate1_ref, n2g_ref, shift2_ref, scale2_ref, x1_ref, h2_ref):
    d = x_ref.shape[1]
    ghg_ref, gat_ref = gates_ref.at[:, :d], gates_ref.at[:, d:]
    part = x_ref.shape[0] // MERGE_SPLIT
    parts = [slice(p * part, (p + 1) * part) for p in range(MERGE_SPLIT)]
    m_hg = [jnp.dot(ohg_ref[rows, :], wbh_ref[...], preferred_element_type=F32) for rows in parts]
    m_at = [jnp.dot(oat_ref[rows, :], wba_ref[...], preferred_element_type=F32) for rows in parts]
    ys = []
    for p, rows in enumerate(parts):
        merged = (jax.nn.sigmoid(ghg_ref[rows, :].astype(F32)) * m_hg[p]
                  + jax.nn.sigmoid(gat_ref[rows, :].astype(F32)) * m_at[p])
        ys.append(jnp.dot(merged.astype(BF16), wo_ref[...], preferred_element_type=F32))
    for p, rows in enumerate(parts):
        x1 = x_ref[rows, :] + gate1_ref[...] * ys[p]
        x1_ref[rows, :] = x1
        h2_ref[rows, :] = _rms_modulate(x1, n2g_ref[...], scale2_ref[...], shift2_ref[...]).astype(h2_ref.dtype)


def _merge(x, o_hg, o_at, gates, w_bh, w_ba, w_o, ada4, norm2_g):
    bsz, seq, d = x.shape
    tm = 512
    resident = lambda shape: pl.BlockSpec(shape, lambda b, i: (0, 0), pipeline_mode=pl.Buffered(1))
    ada_row = lambda k: pl.BlockSpec((None, None, 1, d), lambda b, i: (b, k, 0, 0))
    rows = lambda w, colblk: pl.BlockSpec((None, tm, w), lambda b, i: (b, i, colblk))
    return pl.pallas_call(
        _merge_kernel,
        grid=(bsz, seq // tm),
        in_specs=[
            rows(d, 0), rows(HG_WIDTH, 0), rows(AT_WIDTH, 0), rows(2 * d, 0),
            resident(w_bh.shape), resident(w_ba.shape), resident(w_o.shape),
            ada_row(0), pl.BlockSpec((1, d), lambda b, i: (0, 0)), ada_row(1), ada_row(2),
        ],
        out_specs=[rows(d, 0), rows(d, 0)],
        out_shape=[jax.ShapeDtypeStruct((bsz, seq, d), F32), jax.ShapeDtypeStruct((bsz, seq, d), BF16)],
        compiler_params=_params(2),
        name="merge",
    )(x, o_hg, o_at, gates, w_bh, w_ba, w_o, ada4, norm2_g, ada4, ada4)


def _ffn_kernel(h2_ref, w1_ref, w2_ref, x1_ref, gate2_ref, o_ref):
    @pl.when(pl.program_id(2) == 0)
    def _():
        o_ref[...] = x1_ref[...]

    hid = jnp.maximum(jnp.dot(h2_ref[...], w1_ref[...], preferred_element_type=F32), 0.0)
    o_ref[...] += gate2_ref[...] * jnp.dot((hid * hid).astype(BF16), w2_ref[...], preferred_element_type=F32)


def _ffn(h2, x1, w1, w2, ada4):
    bsz, seq, d = x1.shape
    dff = w1.shape[-1]
    tm, tf = 512, 2048
    rows = lambda: pl.BlockSpec((None, tm, d), lambda b, i, j: (b, i, 0))
    return pl.pallas_call(
        _ffn_kernel,
        grid=(bsz, seq // tm, dff // tf),
        in_specs=[
            rows(),
            pl.BlockSpec((d, tf), lambda b, i, j: (0, j)),
            pl.BlockSpec((tf, d), lambda b, i, j: (j, 0)),
            rows(),
            pl.BlockSpec((None, None, 1, d), lambda b, i, j: (b, 3, 0, 0)),
        ],
        out_specs=rows(),
        out_shape=jax.ShapeDtypeStruct((bsz, seq, d), F32),
        compiler_params=_params(3),
        name="ffn",
    )(h2, w1, w2, x1, ada4)


def kernel(x, c, w_ada, b_ada, norm1_g, norm2_g, w_in, hg_lb_logits, hg_out_norm_g, q_norm_g, k_norm_g,
           attn_sinks, rel_bias_table, w_branch_hg, w_branch_attn, w_out, w_ff1, w_ff2):
    assert w_ada.shape[0] == 1, "single-layer block"
    bsz = x.shape[0]
    d = D_MODEL
    ada_in, w_mix, bias = _ada(c, w_ada, b_ada, 2 * d, w_in, GATE_OFF, rel_bias_table)
    ada_in = ada_in.reshape(bsz, 2, 1, d)
    proj, ada_rest, (w_gate, w_bh, w_ba, w_o, w1, w2) = _inproj(
        x, norm1_g, ada_in, w_mix,
        ((w_in, (GATE_OFF, IN_WIDTH - GATE_OFF), GATE_CHUNK), (w_branch_hg, None, None),
         (w_branch_attn, None, None), (w_out, None, None), (w_ff1, None, None), (w_ff2, None, None)),
        c, w_ada, b_ada, 2 * d)
    ada_out = ada_rest.reshape(bsz, 4, 1, d)
    gates, o_hg, o_at = _mixer(x, norm1_g, ada_in, w_gate, proj, hg_lb_logits, hg_out_norm_g,
                               bias, attn_sinks[0], q_norm_g, k_norm_g)
    x1, h2 = _merge(x, o_hg, o_at, gates, w_bh, w_ba, w_o, ada_out, norm2_g)
    return _ffn(h2, x1, w1, w2, ada_out)
```

```python
import functools
import math

import jax
import jax.numpy as jnp
from jax import lax
from jax.experimental import pallas as pl
from jax.experimental.pallas import tpu as pltpu

F32 = jnp.float32
BF16 = jnp.bfloat16

D_MODEL = 2048
HG_HEADS = 8
HG_DK = 128
HG_DV = 128
HG_WIDTH = HG_HEADS * HG_DK
HG_CHUNK = 64
AT_HEADS = 16
AT_KV_HEADS = 4
AT_HEAD_DIM = 64
AT_GROUP = AT_HEADS // AT_KV_HEADS
AT_WIDTH = AT_HEADS * AT_HEAD_DIM
KV_WIDTH = AT_KV_HEADS * AT_HEAD_DIM
WINDOW = 128
BLOCK = 128
N_BUCKETS = 32
MAX_EXACT = N_BUCKETS // 2
MAX_DISTANCE = 128
EPS = 1e-6
NEG_INF = -1e30
LOG2_E = math.log2(math.e)

HG_OFF = 0
AT_OFF = 4 * HG_WIDTH
GATE_OFF = AT_OFF + AT_WIDTH + 2 * KV_WIDTH
IN_WIDTH = GATE_OFF + 2 * D_MODEL
MXU_COLS = 256

V7X_VMEM_LIMIT_BYTES = 60 * 1024 * 1024
BF16_SUBLANES = 16


def _params(n_axes):
    return pltpu.CompilerParams(
        dimension_semantics=("arbitrary",) * n_axes,
        vmem_limit_bytes=V7X_VMEM_LIMIT_BYTES,
    )


def _rms_modulate(x, g, scale, shift):
    r = lax.rsqrt(jnp.mean(x * x, axis=-1, keepdims=True) + EPS)
    return (x * r * g) * (1.0 + scale) + shift


def _sigmoid(x):
    return 0.5 * jnp.tanh(0.5 * x) + 0.5


def _silu(x):
    h = 0.5 * x
    return h + h * jnp.tanh(h)


def _split_bf16(x):
    hi = x.astype(BF16)
    return hi, (x - hi.astype(F32)).astype(BF16)


def _ada_columns(c_ref, w_ref, b_ref):
    c = c_ref[...]
    c_hi, c_lo = _split_bf16(c * jax.nn.sigmoid(c))
    w_hi, w_lo = _split_bf16(w_ref[...])
    dot = functools.partial(jnp.dot, preferred_element_type=F32)
    return dot(c_hi, w_hi) + dot(c_lo, w_hi) + dot(c_hi, w_lo) + b_ref[...]


def _ada_kernel(c_ref, w_ref, b_ref, win_ref, table_ref, bucket_t_ref, o_ref, win_bf16_ref, bias_ref):
    o_ref[...] = _ada_columns(c_ref, w_ref, b_ref)
    win_bf16_ref[...] = win_ref[...].astype(win_bf16_ref.dtype)
    _bias_group(table_ref, bucket_t_ref, bias_ref, pl.program_id(0))


def _ada(c, w_ada, b_ada, n, w_in, n_in, rel_bias_table):
    bsz, d = c.shape
    steps = AT_KV_HEADS
    tn = n // steps
    rows = w_in.shape[1] // steps
    i = jnp.arange(BLOCK, dtype=jnp.int32)[None, :]
    j = jnp.arange(2 * BLOCK, dtype=jnp.int32)[:, None]
    bucket_t = _t5_causal_bucket(jnp.maximum(i - j + BLOCK, 0))
    bias_block = (2 * BLOCK, AT_GROUP * BLOCK)
    return pl.pallas_call(
        _ada_kernel,
        grid=(steps,),
        in_specs=[
            pl.BlockSpec((bsz, d), lambda j: (0, 0)),
            pl.BlockSpec((None, d, tn), lambda j: (0, 0, j)),
            pl.BlockSpec((1, tn), lambda j: (0, j)),
            pl.BlockSpec((None, rows, n_in), lambda j: (0, j, 0)),
            pl.BlockSpec(memory_space=pltpu.SMEM),
            pl.BlockSpec(bucket_t.shape, lambda j: (0, 0)),
        ],
        out_specs=[pl.BlockSpec((bsz, tn), lambda j: (0, j)), pl.BlockSpec((rows, n_in), lambda j: (j, 0)),
                   pl.BlockSpec((None, *bias_block), lambda j: (j, 0, 0))],
        out_shape=[jax.ShapeDtypeStruct((bsz, n), F32), jax.ShapeDtypeStruct((w_in.shape[1], n_in), BF16),
                   jax.ShapeDtypeStruct((AT_KV_HEADS, *bias_block), F32)],
        compiler_params=_params(1),
        name="ada",
    )(c, w_ada, b_ada, w_in, rel_bias_table, bucket_t)


INPROJ_SPLIT = 2


def _inproj_kernel(late_parts, x_ref, g_ref, shift_ref, scale_ref, w_ref, c_ref, wada_ref, bada_ref, *refs):
    n_in = sum(late_parts)
    late_in, (o_ref, ada_ref, *late_out) = refs[:n_in], refs[n_in:]
    part = x_ref.shape[0] // INPROJ_SPLIT
    for p in range(INPROJ_SPLIT):
        rows = slice(p * part, (p + 1) * part)
        h = _rms_modulate(x_ref[rows, :], g_ref[...], scale_ref[...], shift_ref[...])
        o_ref[rows, :] = jnp.dot(h.astype(BF16), w_ref[...], preferred_element_type=F32).astype(o_ref.dtype)
    ada_ref[...] = _ada_columns(c_ref, wada_ref, bada_ref)
    first = 0
    for n_parts, dst in zip(late_parts, late_out):
        parts = [ref[...] for ref in late_in[first:first + n_parts]]
        first += n_parts
        rows = parts[0] if n_parts == 1 else jnp.concatenate(parts, axis=1)
        if len(dst.shape) == 2:
            dst[...] = rows.astype(dst.dtype)
        else:
            ct = dst.shape[2]
            for t in range(dst.shape[0]):
                dst[t] = rows[:, t * ct:(t + 1) * ct].astype(dst.dtype)


def _inproj(x, norm_g, ada4, w_bf16, late_weights, c, w_ada, b_ada, ada_done):
    bsz, seq, d = x.shape
    n = w_bf16.shape[-1]
    tm, n_col_tiles = 512, 2
    tn = n // n_col_tiles
    assert tn % MXU_COLS == 0
    tiles_per_seq = seq // tm
    n_row_tiles = bsz * tiles_per_seq
    n_steps = n_col_tiles * n_row_tiles
    row = lambda j, i: (i // tiles_per_seq, i % tiles_per_seq)
    step = lambda j, i: j * n_row_tiles + i
    late_in_specs, late_out_specs, late_shapes, late_parts, late_args = [], [], [], [], []
    for w, window, col_tile in late_weights:
        _, rows, cols = w.shape
        first_col, width = window or (0, cols)
        part = math.gcd(first_col, width)
        slab = max(rows // n_steps, BF16_SUBLANES)
        assert rows % slab == 0
        idx = lambda j, i, last=rows // slab - 1: jnp.minimum(step(j, i), last)
        for k in range(width // part):
            late_in_specs.append(pl.BlockSpec(
                (None, slab, part), lambda j, i, idx=idx, blk=first_col // part + k: (0, idx(j, i), blk)))
        late_parts.append(width // part)
        late_args += [w] * (width // part)
        if col_tile is None:
            late_out_specs.append(pl.BlockSpec((slab, width), lambda j, i, idx=idx: (idx(j, i), 0)))
            late_shapes.append(jax.ShapeDtypeStruct((rows, width), BF16))
        else:
            late_out_specs.append(pl.BlockSpec((width // col_tile, slab, col_tile),
                                               lambda j, i, idx=idx: (0, idx(j, i), 0)))
            late_shapes.append(jax.ShapeDtypeStruct((width // col_tile, rows, col_tile), BF16))
    n_ada = w_ada.shape[-1] - ada_done
    ada_cols = n_ada // n_steps
    assert ada_cols % 128 == 0 and ada_done % ada_cols == 0
    ada_blk = lambda j, i: ada_done // ada_cols + step(j, i)
    outs = pl.pallas_call(
        functools.partial(_inproj_kernel, tuple(late_parts)),
        grid=(n_col_tiles, n_row_tiles),
        in_specs=[
            pl.BlockSpec((None, tm, d), lambda j, i: (*row(j, i), 0)),
            pl.BlockSpec((1, d), lambda j, i: (0, 0)),
            pl.BlockSpec((None, None, 1, d), lambda j, i: (i // tiles_per_seq, 0, 0, 0)),
            pl.BlockSpec((None, None, 1, d), lambda j, i: (i // tiles_per_seq, 1, 0, 0)),
            pl.BlockSpec((d, tn), lambda j, i: (0, j)),
            pl.BlockSpec(c.shape, lambda j, i: (0, 0)),
            pl.BlockSpec((None, d, ada_cols), lambda j, i: (0, 0, ada_blk(j, i))),
            pl.BlockSpec((1, ada_cols), lambda j, i: (0, ada_blk(j, i))),
            *late_in_specs,
        ],
        out_specs=[pl.BlockSpec((None, tm, tn), lambda j, i: (*row(j, i), j)),
                   pl.BlockSpec((bsz, ada_cols), lambda j, i: (0, step(j, i))), *late_out_specs],
        out_shape=[jax.ShapeDtypeStruct((bsz, seq, n), BF16), jax.ShapeDtypeStruct((bsz, n_ada), F32),
                   *late_shapes],
        compiler_params=_params(2),
        name="inproj",
    )(x, norm_g, ada4, ada4, w_bf16, c, w_ada, b_ada, *late_args)
    return outs[0], outs[1], outs[2:]


HG_GROUP = 4
HG_ROWS = HG_GROUP * HG_CHUNK
GATE_CHUNK = 256


def _mixer_kernel(x_ref, g_ref, shift_ref, scale_ref, *refs):
    (wg_ref, hgrn_ref, lbl_ref, gn_ref, aq_ref, kv_ref, kv_halo_ref, *attn_params,
     gates_ref, o_ref, oat_ref, h_ref, state_ref) = refs
    q_ref, f_ref, i_ref, hg_ref = (hgrn_ref.at[:, k * HG_WIDTH:(k + 1) * HG_WIDTH] for k in range(4))
    attn_in = (aq_ref, kv_ref.at[:, :KV_WIDTH], kv_ref.at[:, KV_WIDTH:],
               kv_halo_ref.at[:, :KV_WIDTH], kv_halo_ref.at[:, KV_WIDTH:], *attn_params)
    c, n = HG_CHUNK, HG_ROWS
    heads = range(HG_HEADS)
    lanes = [slice(u * HG_DK, (u + 1) * HG_DK) for u in heads]

    @pl.when(pl.program_id(1) == 0)
    def _():
        state_ref[...] = jnp.zeros_like(state_ref)

    h_ref[...] = _rms_modulate(x_ref[...], g_ref[...], scale_ref[...], shift_ref[...]).astype(BF16)
    pending = list(range(wg_ref.shape[0]))

    def gate_chunks(count):
        for _ in range(min(count, len(pending))):
            k = pending.pop(0)
            gates_ref[:, k * GATE_CHUNK:(k + 1) * GATE_CHUNK] = jnp.dot(
                h_ref[...], wg_ref[k], preferred_element_type=F32).astype(gates_ref.dtype)

    attention = _attention_steps(*attn_in, oat_ref, pl.program_id(1) == 0)
    attend = lambda: next(attention, None)

    def interleave(count):
        for _ in range(count):
            attend()
            gate_chunks(1)

    lbl = lbl_ref[...]
    e = jnp.exp(lbl - jnp.max(lbl, axis=0, keepdims=True))
    lb = e[0:1] / jnp.sum(e, axis=0, keepdims=True)
    gn = gn_ref[...]

    t = lax.broadcasted_iota(jnp.int32, (n, n), 0)
    s = lax.broadcasted_iota(jnp.int32, (n, n), 1)
    tri = ((t // c) == (s // c)) & (s <= t)
    tri_t = ((t // c) == (s // c)) & (t <= s)
    r8 = lax.broadcasted_iota(jnp.int32, (8, n), 0)
    s8 = lax.broadcasted_iota(jnp.int32, (8, n), 1)
    in_chunk = (s8 // c) == (r8 % HG_GROUP)
    stat_rows = in_chunk & ((r8 >= HG_GROUP) | ((s8 % c) < c // 2))
    cum_ops = jnp.concatenate([tri.astype(F32), stat_rows.astype(F32)], axis=0).astype(BF16)
    per_chunk = lambda rows: jnp.concatenate(
        [jnp.broadcast_to(rows[cc:cc + 1], (c, HG_DK)) for cc in range(HG_GROUP)], axis=0)
    zeros_rows = lambda rows: jnp.zeros((rows, HG_DK), BF16)
    col_chunk = lax.broadcasted_iota(jnp.int32, (HG_DK, n), 1) // c
    col_masks = [(col_chunk == cc).astype(BF16) for cc in range(HG_GROUP)]

    def chunk_blocks(arr):
        return jnp.concatenate(
            [jnp.concatenate([p for p in (zeros_rows(cc * c), arr[cc * c:(cc + 1) * c],
                                          zeros_rows(n - (cc + 1) * c)) if p.shape[0]], axis=0)
             for cc in range(HG_GROUP)], axis=1)

    attend()
    gate_chunks(3)
    ks, qss, hilo = [], [], []
    for u in heads:
        hf = f_ref[:, lanes[u]].astype(F32)
        hq = q_ref[:, lanes[u]].astype(F32)
        f = lb[:, lanes[u]] + (1.0 - lb[:, lanes[u]]) * _sigmoid(hf)
        log_f = jnp.log2(f)
        ks.append(1.0 - f)
        qss.append(_silu(hq))
        hi = log_f.astype(BF16)
        lo = (log_f - hi.astype(F32)).astype(BF16)
        hilo.append(jnp.concatenate([hi, lo], axis=1))
    rs = [jnp.dot(cum_ops, hilo[u], preferred_element_type=F32) for u in heads]
    interleave(2)
    a_s, bms, kks, qes, decs = [], [], [], [], []
    for u in heads:
        r = rs[u][:, :HG_DK] + rs[u][:, HG_DK:]
        b, b_last = r[0:n], r[n + HG_GROUP:n + 2 * HG_GROUP]
        b_rel = b - per_chunk(r[n:n + HG_GROUP])
        a_s.append((qss[u] * jnp.exp2(b_rel)).astype(BF16))
        bms.append((ks[u] * jnp.exp2(-b_rel)).astype(BF16))
        kks.append((ks[u] * jnp.exp2(per_chunk(b_last) - b)).astype(BF16))
        qes.append((qss[u] * jnp.exp2(b)).T.astype(BF16))
        decs.append(jnp.exp2(b_last))
    scores_t = [lax.dot_general(bms[u], a_s[u], (((1,), (1,)), ((), ())), preferred_element_type=F32)
                for u in heads]
    interleave(2)
    v_ts = [i_ref[:, lanes[u]].astype(F32).T.astype(BF16) for u in heads]
    os_t = [jnp.dot(v_ts[u], jnp.where(tri_t, scores_t[u], 0.0).astype(BF16), preferred_element_type=F32)
            for u in heads]
    interleave(2)
    upd_ts = [jnp.dot(v_ts[u], chunk_blocks(kks[u]), preferred_element_type=F32) for u in heads]
    interleave(2)
    for u in heads:
        s_t = state_ref[u]
        states = []
        for cc in range(HG_GROUP):
            states.append(s_t.astype(BF16))
            s_t = s_t * decs[u][cc:cc + 1] + upd_ts[u][:, cc * HG_DK:(cc + 1) * HG_DK]
        state_ref[u] = s_t
        qe_rows = jnp.concatenate([qes[u] * col_masks[cc] for cc in range(HG_GROUP)], axis=0)
        os_t[u] = os_t[u] + jnp.dot(jnp.concatenate(states, axis=1), qe_rows, preferred_element_type=F32)
    interleave(len(pending) - 2)
    for _ in attention:
        pass
    gate_chunks(len(pending))
    for u in heads:
        o = os_t[u].T
        hg = hg_ref[:, lanes[u]].astype(F32)
        y = o * lax.rsqrt(jnp.mean(o * o, axis=-1, keepdims=True) + EPS) * gn
        o_ref[:, lanes[u]] = (y * _silu(hg)).astype(o_ref.dtype)


def _mixer(x, norm_g, ada4, w_gate_bf16, proj, lb_logits, out_norm_g, bias, sinks, q_norm_g, k_norm_g):
    bsz, seq, d = x.shape
    n_gate = w_gate_bf16.shape[0] * w_gate_bf16.shape[2]
    tm = HG_ROWS
    blocks_per_step = tm // BLOCK
    rows = lambda w, colblk: pl.BlockSpec((None, tm, w), lambda b, i: (b, i, colblk))
    ada_row = lambda k: pl.BlockSpec((None, None, 1, d), lambda b, i: (b, k, 0, 0))
    const = lambda shape: pl.BlockSpec(shape, lambda b, i: (0,) * len(shape))
    w_gate = pl.BlockSpec(w_gate_bf16.shape, lambda b, i: (0, 0, 0), pipeline_mode=pl.Buffered(1))
    kv_col = (AT_OFF + AT_WIDTH) // (2 * KV_WIDTH)
    qg = jnp.tile(q_norm_g, (1, AT_GROUP))
    kg = jnp.tile(k_norm_g, (1, AT_KV_HEADS))
    return pl.pallas_call(
        _mixer_kernel,
        grid=(bsz, seq // tm),
        in_specs=[
            rows(d, 0), const((1, d)), ada_row(0), ada_row(1), w_gate,
            rows(4 * HG_WIDTH, HG_OFF // (4 * HG_WIDTH)), const(lb_logits.shape), const((1, HG_DV)),
            rows(AT_WIDTH, AT_OFF // AT_WIDTH), rows(2 * KV_WIDTH, kv_col),
            pl.BlockSpec((None, BLOCK, 2 * KV_WIDTH),
                         lambda b, i: (b, jnp.maximum(i * blocks_per_step - 1, 0), kv_col)),
            pl.BlockSpec(bias.shape, lambda b, i: (0, 0, 0), pipeline_mode=pl.Buffered(1)),
            pl.BlockSpec(memory_space=pltpu.SMEM), const((1, AT_GROUP * AT_HEAD_DIM)), const((1, KV_WIDTH)),
        ],
        out_specs=[rows(n_gate, 0), rows(HG_WIDTH, 0), rows(AT_WIDTH, 0)],
        out_shape=[jax.ShapeDtypeStruct((bsz, seq, n_gate), BF16),
                   jax.ShapeDtypeStruct((bsz, seq, HG_WIDTH), BF16),
                   jax.ShapeDtypeStruct((bsz, seq, AT_WIDTH), BF16)],
        scratch_shapes=[pltpu.VMEM((tm, d), BF16), pltpu.VMEM((HG_HEADS, HG_DV, HG_DK), F32)],
        compiler_params=_params(2),
        name="mixer",
    )(x, norm_g, ada4, ada4, w_gate_bf16, proj, lb_logits, out_norm_g, proj, proj, proj, bias, sinks, qg, kg)


def _t5_causal_bucket(n):
    nf = jnp.maximum(n, 1).astype(F32)
    large = MAX_EXACT + (jnp.log(nf / MAX_EXACT) / math.log(MAX_DISTANCE / MAX_EXACT)
                         * (N_BUCKETS - MAX_EXACT)).astype(jnp.int32)
    large = jnp.minimum(large, N_BUCKETS - 1)
    return jnp.where(n < MAX_EXACT, n, large)


def _bias_group(table_ref, bucket_t_ref, o_ref, kv):
    bucket_t = bucket_t_ref[...]
    j = lax.broadcasted_iota(jnp.int32, bucket_t.shape, 0)
    i = lax.broadcasted_iota(jnp.int32, bucket_t.shape, 1)
    dist = i - j + BLOCK
    band = (dist >= 0) & (dist < WINDOW)
    for g in range(AT_GROUP):
        acc = jnp.zeros(bucket_t.shape, F32)
        for r in range(N_BUCKETS):
            acc = jnp.where(bucket_t == r, table_ref[r, kv * AT_GROUP + g] * LOG2_E, acc)
        o_ref[:, g * BLOCK:(g + 1) * BLOCK] = jnp.where(band, acc, NEG_INF)


def _head_mean_square(t, pool):
    return jnp.dot((t * t).astype(BF16), pool, preferred_element_type=F32)


def _attention_steps(q_ref, k_ref, v_ref, kh_ref, vh_ref, bias_ref, sink_ref, qg_ref, kg_ref, o_ref, first_block):
    n_blocks = q_ref.shape[0] // BLOCK
    d = AT_HEAD_DIM
    grp = AT_GROUP * d
    sub = lax.broadcasted_iota(jnp.int32, (grp, grp), 0)
    lane = lax.broadcasted_iota(jnp.int32, (grp, grp), 1)
    pool = jnp.where(sub // d == lane // d, 1.0 / d, 0.0).astype(BF16)

    k_all = jnp.concatenate([kh_ref[...], k_ref[...]], axis=0).astype(F32)
    k_n = k_all * lax.rsqrt(_head_mean_square(k_all, pool) + EPS) * kg_ref[...]
    key_lane_head = lax.broadcasted_iota(jnp.int32, k_n.shape, 1) // d
    v_t = jnp.concatenate([vh_ref[...], v_ref[...]], axis=0).astype(F32).T.astype(BF16)

    lane_head = lax.broadcasted_iota(jnp.int32, (BLOCK, grp), 1) // d
    head_masks = [(lane_head == g).astype(BF16) for g in range(AT_GROUP)]
    pre_seq = jnp.where(first_block, NEG_INF, 0.0)

    scale = d ** -0.5 * LOG2_E
    k_reps, q_ns = [], []
    for kv in range(AT_KV_HEADS):
        k_kv = jnp.where(key_lane_head == kv, k_n, 0.0)
        k_rep = k_kv
        for g in range(1, AT_GROUP):
            k_rep = k_rep + pltpu.roll(k_kv, g * d, 1)
        k_reps.append(k_rep.astype(BF16))
        q_g = q_ref[:, kv * grp:(kv + 1) * grp].astype(F32)
        q_ns.append((q_g * lax.rsqrt(_head_mean_square(q_g, pool) + EPS) * (qg_ref[...] * scale)).astype(BF16))
    yield

    def logits_t(kv, nb):
        rows = slice(nb * BLOCK, (nb + 1) * BLOCK)
        keys = slice(nb * BLOCK, (nb + 2) * BLOCK)
        q4 = jnp.concatenate([q_ns[kv][rows] * head_masks[g] for g in range(AT_GROUP)], axis=0)
        lg = lax.dot_general(k_reps[kv][keys], q4, (((1,), (1,)), ((), ())),
                             preferred_element_type=F32) + bias_ref[kv]
        if nb == 0:
            lg = jnp.concatenate([lg[:BLOCK] + pre_seq, lg[BLOCK:]], axis=0)
        return lg

    units = [(kv, nb) for kv in range(AT_KV_HEADS) for nb in range(n_blocks)]
    lg_next = logits_t(*units[0])
    yield
    for idx, (kv, nb) in enumerate(units):
        lg = lg_next
        if idx + 1 < len(units):
            lg_next = logits_t(*units[idx + 1])
            yield
        rows = slice(nb * BLOCK, (nb + 1) * BLOCK)
        keys = slice(nb * BLOCK, (nb + 2) * BLOCK)
        sink = jnp.concatenate(
            [jnp.full((1, BLOCK), sink_ref[kv * AT_GROUP + g] * LOG2_E, F32) for g in range(AT_GROUP)], axis=1)
        m = jnp.maximum(jnp.max(lg, axis=0, keepdims=True), sink)
        e = jnp.exp2(lg - m)
        den = jnp.sum(e, axis=0, keepdims=True) + jnp.exp2(sink - m)
        o_t = jnp.dot(v_t[kv * d:(kv + 1) * d, keys], e.astype(BF16),
                      preferred_element_type=F32) * (1.0 / den)
        o = jnp.concatenate([o_t[:, g * BLOCK:(g + 1) * BLOCK] for g in range(AT_GROUP)], axis=0).T
        o_ref[rows, kv * grp:(kv + 1) * grp] = o.astype(o_ref.dtype)
        yield


MERGE_SPLIT = 2


def _merge_kernel(x_ref, ohg_ref, oat_ref, gates_ref, wbh_ref, wba_ref, wo_ref,
                  gate1_ref, n2g_ref, shift2_ref, scale2_ref, x1_ref, h2_ref):
    d = x_ref.shape[1]
    ghg_ref, gat_ref = gates_ref.at[:, :d], gates_ref.at[:, d:]
    part = x_ref.shape[0] // MERGE_SPLIT
    parts = [slice(p * part, (p + 1) * part) for p in range(MERGE_SPLIT)]
    m_hg = [jnp.dot(ohg_ref[rows, :], wbh_ref[...], preferred_element_type=F32) for rows in parts]
    m_at = [jnp.dot(oat_ref[rows, :], wba_ref[...], preferred_element_type=F32) for rows in parts]
    ys = []
    for p, rows in enumerate(parts):
        merged = (jax.nn.sigmoid(ghg_ref[rows, :].astype(F32)) * m_hg[p]
                  + jax.nn.sigmoid(gat_ref[rows, :].astype(F32)) * m_at[p])
        ys.append(jnp.dot(merged.astype(BF16), wo_ref[...], preferred_element_type=F32))
    for p, rows in enumerate(parts):
        x1 = x_ref[rows, :] + gate1_ref[...] * ys[p]
        x1_ref[rows, :] = x1
        h2_ref[rows, :] = _rms_modulate(x1, n2g_ref[...], scale2_ref[...], shift2_ref[...]).astype(h2_ref.dtype)


def _merge(x, o_hg, o_at, gates, w_bh, w_ba, w_o, ada4, norm2_g):
    bsz, seq, d = x.shape
    tm = 512
    resident = lambda shape: pl.BlockSpec(shape, lambda b, i: (0, 0), pipeline_mode=pl.Buffered(1))
    ada_row = lambda k: pl.BlockSpec((None, None, 1, d), lambda b, i: (b, k, 0, 0))
    rows = lambda w, colblk: pl.BlockSpec((None, tm, w), lambda b, i: (b, i, colblk))
    return pl.pallas_call(
        _merge_kernel,
        grid=(bsz, seq // tm),
        in_specs=[
            rows(d, 0), rows(HG_WIDTH, 0), rows(AT_WIDTH, 0), rows(2 * d, 0),
            resident(w_bh.shape), resident(w_ba.shape), resident(w_o.shape),
            ada_row(0), pl.BlockSpec((1, d), lambda b, i: (0, 0)), ada_row(1), ada_row(2),
        ],
        out_specs=[rows(d, 0), rows(d, 0)],
        out_shape=[jax.ShapeDtypeStruct((bsz, seq, d), F32), jax.ShapeDtypeStruct((bsz, seq, d), BF16)],
        compiler_params=_params(2),
        name="merge",
    )(x, o_hg, o_at, gates, w_bh, w_ba, w_o, ada4, norm2_g, ada4, ada4)


def _ffn_kernel(h2_ref, w1_ref, w2_ref, x1_ref, gate2_ref, o_ref):
    @pl.when(pl.program_id(2) == 0)
    def _():
        o_ref[...] = x1_ref[...]

    hid = jnp.maximum(jnp.dot(h2_ref[...], w1_ref[...], preferred_element_type=F32), 0.0)
    o_ref[...] += gate2_ref[...] * jnp.dot((hid * hid).astype(BF16), w2_ref[...], preferred_element_type=F32)


def _ffn(h2, x1, w1, w2, ada4):
    bsz, seq, d = x1.shape
    dff = w1.shape[-1]
    tm, tf = 512, 2048
    rows = lambda: pl.BlockSpec((None, tm, d), lambda b, i, j: (b, i, 0))
    return pl.pallas_call(
        _ffn_kernel,
        grid=(bsz, seq // tm, dff // tf),
        in_specs=[
            rows(),
            pl.BlockSpec((d, tf), lambda b, i, j: (0, j)),
            pl.BlockSpec((tf, d), lambda b, i, j: (j, 0)),
            rows(),
            pl.BlockSpec((None, None, 1, d), lambda b, i, j: (b, 3, 0, 0)),
        ],
        out_specs=rows(),
        out_shape=jax.ShapeDtypeStruct((bsz, seq, d), F32),
        compiler_params=_params(3),
        name="ffn",
    )(h2, w1, w2, x1, ada4)


def kernel(x, c, w_ada, b_ada, norm1_g, norm2_g, w_in, hg_lb_logits, hg_out_norm_g, q_norm_g, k_norm_g,
           attn_sinks, rel_bias_table, w_branch_hg, w_branch_attn, w_out, w_ff1, w_ff2):
    assert w_ada.shape[0] == 1, "single-layer block"
    bsz = x.shape[0]
    d = D_MODEL
    ada_in, w_mix, bias = _ada(c, w_ada, b_ada, 2 * d, w_in, GATE_OFF, rel_bias_table)
    ada_in = ada_in.reshape(bsz, 2, 1, d)
    proj, ada_rest, (w_gate, w_bh, w_ba, w_o, w1, w2) = _inproj(
        x, norm1_g, ada_in, w_mix,
        ((w_in, (GATE_OFF, IN_WIDTH - GATE_OFF), GATE_CHUNK), (w_branch_hg, None, None),
         (w_branch_attn, None, None), (w_out, None, None), (w_ff1, None, None), (w_ff2, None, None)),
        c, w_ada, b_ada, 2 * d)
    ada_out = ada_rest.reshape(bsz, 4, 1, d)
    gates, o_hg, o_at = _mixer(x, norm1_g, ada_in, w_gate, proj, hg_lb_logits, hg_out_norm_g,
                               bias, attn_sinks[0], q_norm_g, k_norm_g)
    x1, h2 = _merge(x, o_hg, o_at, gates, w_bh, w_ba, w_o, ada_out, norm2_g)
    return _ffn(h2, x1, w1, w2, ada_out)
```
